```python
import math, functools
import jax, jax.numpy as jnp
from jax import lax
import numpy as np

D_MODEL = 1024
BATCH = 8
SEQ = 4096
DEPTH = 2
DEC_BATCH = 128
DEC_SEQ = 1
PAST_LEN = 16384
PAGE_SIZE = 128

N_EVEN = (DEPTH + 1) // 2
N_ODD = DEPTH // 2
MIX_WIDTH = D_MODEL
HEAD_DIM = 64
QBLOCK = 128
H_A = MIX_WIDTH // 2 // HEAD_DIM
KVH_A = 2
WINDOW_A = 128
H_B = 4
DK_B = 64
DV_B = MIX_WIDTH // 2 // H_B
CHUNK_B = 64
P_C = 64
H_C = MIX_WIDTH // 2 // P_C
D_INNER_C = H_C * P_C
G_C = 2
N_C = 128
CONV_W = 4
CONV_DIM_C = D_INNER_C + 2 * G_C * N_C
CHUNK_C = 128
H_D = MIX_WIDTH // 2 // HEAD_DIM
KVH_D = 2
CMP_BLOCK = 32
CMP_HIDDEN = 128
SLC_BLOCK = 64
TOP_N = 16
WINDOW_D = 512
SLC_QBLOCK = 32
D_FF = 3584
N_EXP = 8
TOP_K = 2
IN_E_SIZES = (H_A * HEAD_DIM, KVH_A * HEAD_DIM, KVH_A * HEAD_DIM, H_B * DK_B, H_B * DK_B, H_B * DV_B, H_B, H_B, H_B * DV_B)
IN_O_SIZES = (D_INNER_C, CONV_DIM_C, H_C, H_D * HEAD_DIM, 6 * KVH_D * HEAD_DIM, 3 * H_D)
IN_E = sum(IN_E_SIZES)
IN_O = sum(IN_O_SIZES)

kernel_name = 'hybrid_swa_mlstm_ssd_nsa_step'


def rms_norm(x, g, eps=1e-6):
    xf = x.astype(jnp.float32)
    y = xf * lax.rsqrt(jnp.mean(xf * xf, axis=-1, keepdims=True) + eps)
    return (y * g.astype(jnp.float32)).astype(x.dtype)


def split_cols(x, sizes):
    return jnp.split(x, [int(s) for s in np.cumsum(sizes)[:-1]], axis=-1)


def swiglu(x, w_gu, w_down):
    g, u = jnp.split(x @ w_gu, 2, axis=-1)
    return (jax.nn.silu(g) * u) @ w_down


def moe_swiglu(x, router, w_gu, w_down):
    logits = (x @ router).astype(jnp.float32)
    top_v, top_i = lax.top_k(logits, TOP_K)
    gate = jnp.sum(jax.nn.one_hot(top_i, N_EXP, dtype=jnp.float32) * jax.nn.softmax(top_v, axis=-1)[..., None], axis=-2)
    y = jnp.zeros_like(x)
    for e in range(N_EXP):
        y = y + gate[..., e:e + 1].astype(x.dtype) * swiglu(x, w_gu[e], w_down[e])
    return y


def last_rows(a, n):
    t = a.shape[1]
    if t >= n:
        return a[:, t - n:]
    pad = [(0, 0)] * a.ndim
    pad[1] = (n - t, 0)
    return jnp.pad(a, pad)


def chunked_scan(step, init, xs, chunk):
    t = xs[0].shape[1]
    nc = t // chunk
    def to_chunks(a):
        return a.reshape(a.shape[0], nc, chunk, *a.shape[2:]).swapaxes(0, 1)
    final, ys = lax.scan(step, init, tuple(to_chunks(a) for a in xs))
    ys = ys.swapaxes(0, 1)
    return final, ys.reshape(ys.shape[0], t, *ys.shape[3:])


def softmax_attend(q, k, v, mask, sinks=None):
    s = jnp.einsum('...qgrd,...kgd->...grqk', q, k).astype(jnp.float32) * (HEAD_DIM ** -0.5)
    s = jnp.where(mask[..., None, None, :, :], s, -jnp.inf)
    m = jnp.max(s, axis=-1, keepdims=True)
    if sinks is not None:
        sk = sinks.astype(jnp.float32)[:, :, None, None]
        m = jnp.maximum(m, sk)
    m = jnp.where(jnp.isfinite(m), m, 0.0)
    p = jnp.exp(s - m)
    den = jnp.sum(p, axis=-1, keepdims=True)
    if sinks is not None:
        den = den + jnp.exp(sk - m)
    p = p / jnp.where(den > 0, den, 1.0)
    out = jnp.einsum('...grqk,...kgd->...qgrd', p.astype(v.dtype), v)
    return out, p


def banded_window_attention(q, k, v, window, sinks):
    bsz, s_len, g, r, d = q.shape
    nb = s_len // QBLOCK
    span = window + QBLOCK
    pad = ((0, 0), (window, 0), (0, 0), (0, 0))
    kp, vp = jnp.pad(k, pad), jnp.pad(v, pad)
    qb = q.reshape(bsz, nb, QBLOCK, g, r, d).swapaxes(0, 1)
    def block(args):
        i, qi = args
        start = i * QBLOCK
        ki = lax.dynamic_slice_in_dim(kp, start, span, axis=1)
        vi = lax.dynamic_slice_in_dim(vp, start, span, axis=1)
        q_pos = start + jnp.arange(QBLOCK)
        k_pos = start - window + jnp.arange(span)
        diff = q_pos[:, None] - k_pos[None, :]
        mask = (k_pos[None, :] >= 0) & (diff >= 0) & (diff < window)
        return softmax_attend(qi, ki, vi, mask, sinks)[0]
    out = lax.map(block, (jnp.arange(nb), qb))
    return out.swapaxes(0, 1).reshape(bsz, s_len, g, r, d)


def window_step_attention(q, k_new, v_new, buf, window, sinks):
    t = k_new.shape[1]
    k = jnp.concatenate([buf[:, :, 0], k_new], axis=1)
    v = jnp.concatenate([buf[:, :, 1], v_new], axis=1)
    q_pos = PAST_LEN + jnp.arange(t)
    k_pos = PAST_LEN - window + jnp.arange(window + t)
    diff = q_pos[:, None] - k_pos[None, :]
    mask = (k_pos[None, :] >= 0) & (diff >= 0) & (diff < window)
    out, _ = softmax_attend(q, k, v, mask, sinks)
    new_buf = jnp.concatenate([buf, jnp.stack([k_new, v_new], axis=2)], axis=1)[:, t:]
    return out, new_buf


def mlstm_chunk(state, inputs):
    c, n, m = state
    q, k, v, ig, lf = inputs
    L = q.shape[1]
    f_cum = jnp.cumsum(lf, axis=1)
    causal = jnp.tril(jnp.ones((L, L), dtype=bool))
    log_d = jnp.where(causal[None, :, :, None], f_cum[:, :, None, :] - f_cum[:, None, :, :] + ig[:, None, :, :], -jnp.inf)
    m_inter = f_cum + m[:, None, :]
    m_t = jnp.maximum(m_inter, jnp.max(log_d, axis=2))
    d_mat = jnp.exp(log_d - m_t[:, :, None, :])
    w_inter = jnp.exp(m_inter - m_t)
    qk = jnp.einsum('bthd,bshd->btsh', q, k) * d_mat
    num = jnp.einsum('btsh,bshv->bthv', qk, v) + w_inter[..., None] * jnp.einsum('bhvd,bthd->bthv', c, q)
    den = jnp.sum(qk, axis=2) + w_inter * jnp.einsum('bhd,bthd->bth', n, q)
    h = num / jnp.maximum(jnp.abs(den), jnp.exp(-m_t))[..., None]
    m_new = m_t[:, -1]
    w_end = jnp.exp(f_cum[:, -1:] - f_cum + ig - m_new[:, None, :])
    decay = jnp.exp(f_cum[:, -1] + m - m_new)
    c_new = decay[..., None, None] * c + jnp.einsum('bsh,bshv,bshd->bhvd', w_end, v, k)
    n_new = decay[..., None] * n + jnp.einsum('bsh,bshd->bhd', w_end, k)
    return (c_new, n_new, m_new), h


def ssd_chunk(h, inputs, a_neg):
    x, dt, bm, cm = inputs
    bsz, L = x.shape[:2]
    r = H_C // G_C
    cum = jnp.cumsum(dt * a_neg, axis=1)
    causal = jnp.tril(jnp.ones((L, L), dtype=bool))
    seg = jnp.where(causal[None, :, :, None], cum[:, :, None, :] - cum[:, None, :, :], -jnp.inf)
    decay = jnp.exp(seg).reshape(bsz, L, L, G_C, r)
    dtx = (dt[..., None] * x).reshape(bsz, L, G_C, r, P_C)
    cb = jnp.einsum('btgn,bsgn->btsg', cm, bm)
    y = jnp.einsum('btsg,btsgr,bsgrp->btgrp', cb, decay, dtx)
    hg = h.reshape(bsz, G_C, r, P_C, N_C)
    y = y + jnp.einsum('btgn,bgrpn->btgrp', cm, hg) * jnp.exp(cum).reshape(bsz, L, G_C, r)[..., None]
    w_end = jnp.exp(cum[:, -1:] - cum).reshape(bsz, L, G_C, r)
    h_new = hg * jnp.exp(cum[:, -1]).reshape(bsz, G_C, r)[..., None, None] + jnp.einsum('bsgr,bsgrp,bsgn->bgrpn', w_end, dtx, bm)
    return h_new.reshape(bsz, H_C, P_C, N_C), y.reshape(bsz, L, H_C, P_C)


def causal_conv(xbc, buf, w, b):
    t = xbc.shape[1]
    xp = jnp.concatenate([buf, xbc], axis=1)
    out = b + sum(xp[:, j:j + t] * w[j] for j in range(CONV_W))
    return jax.nn.silu(out), xp[:, t:]


def compress(rows, pos, w1, w2):
    bsz, t, g, d = rows.shape
    nb = t // CMP_BLOCK
    blk = rows.reshape(bsz, nb, CMP_BLOCK, g, d) + pos[:, None, :]
    blk = blk.transpose(0, 1, 3, 2, 4).reshape(bsz, nb, g, CMP_BLOCK * d)
    return jax.nn.silu(blk @ w1) @ w2


def select_attend(q, p_cmp, q_pos, gather, n_top):
    bsz, g, r, t, nbc = p_cmp.shape
    ratio = SLC_BLOCK // CMP_BLOCK
    imp = p_cmp.sum(axis=2).reshape(bsz, g, t, nbc // ratio, ratio).sum(axis=-1)
    blk = jnp.arange(nbc // ratio)[None, :]
    cur = (q_pos // SLC_BLOCK)[:, None]
    forced = (blk == 0) | (blk == cur) | (blk == cur - 1)
    score = jnp.where(blk > cur, -jnp.inf, jnp.where(forced, jnp.inf, imp))
    _, sel = lax.top_k(score, n_top)
    pos = (sel[..., None] * SLC_BLOCK + jnp.arange(SLC_BLOCK)).reshape(bsz, g, t, n_top * SLC_BLOCK)
    k, v = gather(pos)
    s = jnp.einsum('btgrd,bgtkd->bgrtk', q, k).astype(jnp.float32) * (HEAD_DIM ** -0.5)
    s = jnp.where((pos <= q_pos[None, None, :, None])[:, :, None], s, -jnp.inf)
    p = jax.nn.softmax(s, axis=-1)
    return jnp.einsum('bgrtk,bgtkd->btgrd', p.astype(v.dtype), v)


def even_layer(x, w, st, prompt):
    f32 = jnp.float32
    bsz, t, _ = x.shape
    r_a = H_A // KVH_A
    proj = rms_norm(x, w['norm_mix']) @ w['w_in']
    qa, ka, va, qb, kb, vb, ib, fb, ob = split_cols(proj, IN_E_SIZES)
    qa = rms_norm(qa.reshape(bsz, t, KVH_A, r_a, HEAD_DIM), w['q_norm'])
    ka = rms_norm(ka.reshape(bsz, t, KVH_A, HEAD_DIM), w['k_norm'])
    va = va.reshape(bsz, t, KVH_A, HEAD_DIM)
    sinks = w['sinks'].reshape(KVH_A, r_a)
    if prompt:
        o_a = banded_window_attention(qa, ka, va, WINDOW_A, sinks)
        new_win = last_rows(jnp.stack([ka, va], axis=2), WINDOW_A)
    else:
        o_a, new_win = window_step_attention(qa, ka, va, st['a_win'], WINDOW_A, sinks)
    qb = qb.reshape(bsz, t, H_B, DK_B).astype(f32)
    kb = kb.reshape(bsz, t, H_B, DK_B).astype(f32) * (DK_B ** -0.5)
    vb = vb.reshape(bsz, t, H_B, DV_B).astype(f32)
    ig = (ib + w['b_igate']).astype(f32)
    lf = jax.nn.log_sigmoid((fb + w['b_fgate']).astype(f32))
    if prompt:
        init = (jnp.zeros((bsz, H_B, DV_B, DK_B), f32), jnp.zeros((bsz, H_B, DK_B), f32), jnp.zeros((bsz, H_B), f32))
        (c_new, n_new, m_new), hb = chunked_scan(mlstm_chunk, init, (qb, kb, vb, ig, lf), CHUNK_B)
    else:
        init = (st['b_c'].astype(f32), st['b_n'].astype(f32), st['b_m'].astype(f32))
        (c_new, n_new, m_new), hb = mlstm_chunk(init, (qb, kb, vb, ig, lf))
    hb = rms_norm(hb, w['h_norm']) * jax.nn.sigmoid(ob.reshape(bsz, t, H_B, DV_B).astype(f32))
    mix = jnp.concatenate([o_a.reshape(bsz, t, H_A * HEAD_DIM), hb.reshape(bsz, t, H_B * DV_B).astype(x.dtype)], axis=-1)
    x = x + mix @ w['w_out']
    x = x + swiglu(rms_norm(x, w['norm_ffn']), w['w_gu'], w['w_down'])
    return x, (new_win, c_new.astype(x.dtype), n_new.astype(x.dtype), m_new.astype(x.dtype))


def odd_layer(x, w, st, page_table, prompt):
    f32 = jnp.float32
    bsz, t, _ = x.shape
    r_d = H_D // KVH_D
    proj = rms_norm(x, w['norm_mix']) @ w['w_in']
    zc, xbc, dtc, qd, kvd, gd = split_cols(proj, IN_O_SIZES)
    buf = jnp.zeros((bsz, CONV_W - 1, CONV_DIM_C), x.dtype) if prompt else st['c_conv']
    xbc, new_conv = causal_conv(xbc, buf, w['conv_w'], w['conv_b'])
    xc, bc, cc = split_cols(xbc.astype(f32), (D_INNER_C, G_C * N_C, G_C * N_C))
    xc = xc.reshape(bsz, t, H_C, P_C)
    bc = bc.reshape(bsz, t, G_C, N_C)
    cc = cc.reshape(bsz, t, G_C, N_C)
    dt = jax.nn.softplus((dtc + w['dt_bias']).astype(f32))
    step = functools.partial(ssd_chunk, a_neg=-jnp.exp(w['a_log'].astype(f32)))
    if prompt:
        h_new, yc = chunked_scan(step, jnp.zeros((bsz, H_C, P_C, N_C), f32), (xc, dt, bc, cc), CHUNK_C)
    else:
        h_new, yc = step(st['c_ssm'].astype(f32), (xc, dt, bc, cc))
    yc = yc + w['d_skip'].astype(f32)[:, None] * xc
    yc = yc.reshape(bsz, t, D_INNER_C) * jax.nn.silu(zc.astype(f32))
    yc = rms_norm(yc.reshape(bsz, t, G_C, D_INNER_C // G_C), w['y_norm'].reshape(G_C, D_INNER_C // G_C))
    qd = rms_norm(qd.reshape(bsz, t, KVH_D, r_d, HEAD_DIM), w['q_norm'])
    kc, vc, ks, vs, kw, vw = [a.reshape(bsz, t, KVH_D, HEAD_DIM) for a in split_cols(kvd, (KVH_D * HEAD_DIM,) * 6)]
    ks = rms_norm(ks, w['k_norm'])
    kw = rms_norm(kw, w['k_norm'])
    new_kv = jnp.stack([kc, vc, ks, vs], axis=2)
    q_pos = jnp.arange(t) + (0 if prompt else PAST_LEN)
    ks_t, vs_t = ks.transpose(0, 2, 1, 3), vs.transpose(0, 2, 1, 3)
    bi = jnp.arange(bsz)[:, None, None, None]
    gi = jnp.arange(KVH_D)[None, :, None, None]
    def comp(rows, j):
        return compress(rows, w['cmp_pos'][j], w['cmp_w1'][j], w['cmp_w2'][j])
    if prompt:
        kcmp, vcmp = comp(kc, 0), comp(vc, 1)
        def gather(pos):
            return ks_t[bi, gi, pos], vs_t[bi, gi, pos]
    else:
        pool, li = st['d_kv'], st['layer']
        n_pages = page_table.shape[1]
        past_k = pool[li, page_table, :, 0].reshape(bsz, PAST_LEN, KVH_D, HEAD_DIM)
        past_v = pool[li, page_table, :, 1].reshape(bsz, PAST_LEN, KVH_D, HEAD_DIM)
        pad_t = -(-t // SLC_BLOCK) * SLC_BLOCK
        padw = ((0, 0), (0, pad_t - t), (0, 0), (0, 0))
        kcmp = jnp.concatenate([comp(past_k, 0), comp(jnp.pad(kc, padw), 0)], axis=1)
        vcmp = jnp.concatenate([comp(past_v, 1), comp(jnp.pad(vc, padw), 1)], axis=1)
        def gather(pos):
            phys = page_table[bi, jnp.clip(pos // PAGE_SIZE, 0, n_pages - 1)]
            off = pos % PAGE_SIZE
            tail = jnp.clip(pos - PAST_LEN, 0, t - 1)
            past = (pos < PAST_LEN)[..., None]
            k = jnp.where(past, pool[li, phys, off, 2, gi], ks_t[bi, gi, tail])
            v = jnp.where(past, pool[li, phys, off, 3, gi], vs_t[bi, gi, tail])
            return k, v
    kcmp = rms_norm(kcmp, w['k_norm'])
    blk_end = jnp.arange(kcmp.shape[1]) * CMP_BLOCK + (CMP_BLOCK - 1)
    o_cmp, p_cmp = softmax_attend(qd, kcmp, vcmp, blk_end[None, :] <= q_pos[:, None])
    n_top = min(TOP_N, kcmp.shape[1] // (SLC_BLOCK // CMP_BLOCK))
    if prompt:
        nqb = t // SLC_QBLOCK
        q_blocks = qd.reshape(bsz, nqb, SLC_QBLOCK, KVH_D, r_d, HEAD_DIM).swapaxes(0, 1)
        p_blocks = p_cmp.reshape(bsz, KVH_D, r_d, nqb, SLC_QBLOCK, -1).transpose(3, 0, 1, 2, 4, 5)
        pos_blocks = q_pos.reshape(nqb, SLC_QBLOCK)
        o_slc = lax.map(lambda a: select_attend(a[0], a[1], a[2], gather, n_top), (q_blocks, p_blocks, pos_blocks))
        o_slc = o_slc.swapaxes(0, 1).reshape(bsz, t, KVH_D, r_d, HEAD_DIM)
        o_win = banded_window_attention(qd, kw, vw, WINDOW_D, None)
        new_win = last_rows(jnp.stack([kw, vw], axis=2), WINDOW_D)
    else:
        o_slc = select_attend(qd, p_cmp, q_pos, gather, n_top)
        o_win, new_win = window_step_attention(qd, kw, vw, st['d_win'], WINDOW_D, None)
    g = jax.nn.sigmoid(gd.astype(f32)).reshape(bsz, t, KVH_D, r_d, 3)
    o_d = g[..., 0:1] * o_cmp + g[..., 1:2] * o_slc + g[..., 2:3] * o_win
    mix = jnp.concatenate([yc.reshape(bsz, t, D_INNER_C).astype(x.dtype), o_d.reshape(bsz, t, H_D * HEAD_DIM).astype(x.dtype)], axis=-1)
    x = x + mix @ w['w_out']
    x = x + moe_swiglu(rms_norm(x, w['norm_ffn']), w['router'], w['w_gu'], w['w_down'])
    return x, (h_new.astype(x.dtype), new_conv, new_kv, new_win)


def setup_inputs(seed: int = 0) -> dict:
    key = jax.random.key(seed)
    keys = iter(jax.random.split(key, 64))
    f32 = jnp.float32
    def nrm(shape, scale=1.0):
        return scale * jax.random.normal(next(keys), shape, f32)
    def gain(shape):
        return 1.0 + nrm(shape, 0.05)
    NE, NO = N_EVEN, N_ODD
    n_pages = PAST_LEN // PAGE_SIZE
    used = DEC_BATCH * n_pages
    n_pool = used + max(1, used // 4)
    page_table = jax.random.permutation(next(keys), n_pool)[:used].reshape(DEC_BATCH, n_pages).astype(jnp.int32)
    dt0 = jnp.exp(jax.random.uniform(next(keys), (NO, H_C), f32, math.log(1e-3), math.log(1e-1)))
    dt_bias = dt0 + jnp.log(-jnp.expm1(-dt0))
    a_log = jnp.log(jax.random.uniform(next(keys), (NO, H_C), f32, 1.0, 16.0))
    return {
        'x_prompt': nrm((BATCH, SEQ, D_MODEL)),
        'x_sample': nrm((DEC_BATCH, DEC_SEQ, D_MODEL)),
        'cache_a_win': nrm((NE, DEC_BATCH, WINDOW_A, 2, KVH_A, HEAD_DIM)),
        'state_b_c': nrm((NE, DEC_BATCH, H_B, DV_B, DK_B), 0.1),
        'state_b_n': nrm((NE, DEC_BATCH, H_B, DK_B), 0.1),
        'state_b_m': nrm((NE, DEC_BATCH, H_B), 0.5),
        'state_c_ssm': nrm((NO, DEC_BATCH, H_C, P_C, N_C), 0.1),
        'state_c_conv': nrm((NO, DEC_BATCH, CONV_W - 1, CONV_DIM_C)),
        'cache_d_kv': nrm((NO, n_pool, PAGE_SIZE, 4, KVH_D, HEAD_DIM)),
        'cache_d_win': nrm((NO, DEC_BATCH, WINDOW_D, 2, KVH_D, HEAD_DIM)),
        'page_table': page_table,
        'e_norm_mix': gain((NE, D_MODEL)),
        'e_w_in': nrm((NE, D_MODEL, IN_E), D_MODEL ** -0.5),
        'e_q_norm': gain((NE, HEAD_DIM)),
        'e_k_norm': gain((NE, HEAD_DIM)),
        'e_sinks': nrm((NE, H_A), 0.5),
        'e_b_igate': nrm((NE, H_B), 0.1) - 1.0,
        'e_b_fgate': 3.0 + nrm((NE, H_B), 0.5),
        'e_h_norm': gain((NE, DV_B)),
        'e_w_out': nrm((NE, MIX_WIDTH, D_MODEL), MIX_WIDTH ** -0.5),
        'e_norm_ffn': gain((NE, D_MODEL)),
        'e_w_gu': nrm((NE, D_MODEL, 2 * D_FF), D_MODEL ** -0.5),
        'e_w_down': nrm((NE, D_FF, D_MODEL), D_FF ** -0.5),
        'o_norm_mix': gain((NO, D_MODEL)),
        'o_w_in': nrm((NO, D_MODEL, IN_O), D_MODEL ** -0.5),
        'o_conv_w': nrm((NO, CONV_W, CONV_DIM_C), CONV_W ** -0.5),
        'o_conv_b': nrm((NO, CONV_DIM_C), 0.02),
        'o_dt_bias': dt_bias,
        'o_a_log': a_log,
        'o_d_skip': gain((NO, H_C)),
        'o_y_norm': gain((NO, D_INNER_C)),
        'o_q_norm': gain((NO, HEAD_DIM)),
        'o_k_norm': gain((NO, HEAD_DIM)),
        'o_cmp_pos': nrm((NO, 2, CMP_BLOCK, HEAD_DIM), 0.1),
        'o_cmp_w1': nrm((NO, 2, CMP_BLOCK * HEAD_DIM, CMP_HIDDEN), (CMP_BLOCK * HEAD_DIM) ** -0.5),
        'o_cmp_w2': nrm((NO, 2, CMP_HIDDEN, HEAD_DIM), CMP_HIDDEN ** -0.5),
        'o_w_out': nrm((NO, MIX_WIDTH, D_MODEL), MIX_WIDTH ** -0.5),
        'o_norm_ffn': gain((NO, D_MODEL)),
        'o_router': nrm((NO, D_MODEL, N_EXP), D_MODEL ** -0.5),
        'o_w_gu': nrm((NO, N_EXP, D_MODEL, 2 * D_FF), D_MODEL ** -0.5),
        'o_w_down': nrm((NO, N_EXP, D_FF, D_MODEL), D_FF ** -0.5),
    }


def reference(x_prompt, x_sample, cache_a_win, state_b_c, state_b_n, state_b_m, state_c_ssm, state_c_conv,
              cache_d_kv, cache_d_win, page_table,
              e_norm_mix, e_w_in, e_q_norm, e_k_norm, e_sinks, e_b_igate, e_b_fgate, e_h_norm, e_w_out,
              e_norm_ffn, e_w_gu, e_w_down,
              o_norm_mix, o_w_in, o_conv_w, o_conv_b, o_dt_bias, o_a_log, o_d_skip, o_y_norm, o_q_norm, o_k_norm,
              o_cmp_pos, o_cmp_w1, o_cmp_w2, o_w_out, o_norm_ffn, o_router, o_w_gu, o_w_down):
    xp, xs = x_prompt, x_sample
    even_p, even_s, odd_p, odd_s = [], [], [], []
    for layer in range(DEPTH):
        i = layer // 2
        if layer % 2 == 0:
            w = {'norm_mix': e_norm_mix[i], 'w_in': e_w_in[i], 'q_norm': e_q_norm[i], 'k_norm': e_k_norm[i],
                 'sinks': e_sinks[i], 'b_igate': e_b_igate[i], 'b_fgate': e_b_fgate[i], 'h_norm': e_h_norm[i],
                 'w_out': e_w_out[i], 'norm_ffn': e_norm_ffn[i], 'w_gu': e_w_gu[i], 'w_down': e_w_down[i]}
            st = {'a_win': cache_a_win[i], 'b_c': state_b_c[i], 'b_n': state_b_n[i], 'b_m': state_b_m[i]}
            xp, sp = even_layer(xp, w, None, True)
            xs, ss = even_layer(xs, w, st, False)
            even_p.append(sp)
            even_s.append(ss)
        else:
            w = {'norm_mix': o_norm_mix[i], 'w_in': o_w_in[i], 'conv_w': o_conv_w[i], 'conv_b': o_conv_b[i],
                 'dt_bias': o_dt_bias[i], 'a_log': o_a_log[i], 'd_skip': o_d_skip[i], 'y_norm': o_y_norm[i],
                 'q_norm': o_q_norm[i], 'k_norm': o_k_norm[i], 'cmp_pos': o_cmp_pos[i], 'cmp_w1': o_cmp_w1[i],
                 'cmp_w2': o_cmp_w2[i], 'w_out': o_w_out[i], 'norm_ffn': o_norm_ffn[i], 'router': o_router[i],
                 'w_gu': o_w_gu[i], 'w_down': o_w_down[i]}
            st = {'c_ssm': state_c_ssm[i], 'c_conv': state_c_conv[i], 'd_kv': cache_d_kv, 'layer': i,
                  'd_win': cache_d_win[i]}
            xp, sp = odd_layer(xp, w, None, page_table, True)
            xs, ss = odd_layer(xs, w, st, page_table, False)
            odd_p.append(sp)
            odd_s.append(ss)
    def stacked(lst, j):
        return jnp.stack([s[j] for s in lst])
    y_prompt, y_sample = xp, xs
    new_a_win_p, new_a_win_s = stacked(even_p, 0), stacked(even_s, 0)
    new_b_c_p, new_b_c_s = stacked(even_p, 1), stacked(even_s, 1)
    new_b_n_p, new_b_n_s = stacked(even_p, 2), stacked(even_s, 2)
    new_b_m_p, new_b_m_s = stacked(even_p, 3), stacked(even_s, 3)
    new_c_ssm_p, new_c_ssm_s = stacked(odd_p, 0), stacked(odd_s, 0)
    new_c_conv_p, new_c_conv_s = stacked(odd_p, 1), stacked(odd_s, 1)
    new_d_kv_p, new_d_kv_s = stacked(odd_p, 2), stacked(odd_s, 2)
    new_d_win_p, new_d_win_s = stacked(odd_p, 3), stacked(odd_s, 3)
    return (y_prompt, y_sample, new_a_win_p, new_a_win_s, new_b_c_p, new_b_c_s, new_b_n_p, new_b_n_s,
            new_b_m_p, new_b_m_s, new_c_ssm_p, new_c_ssm_s, new_c_conv_p, new_c_conv_s,
            new_d_kv_p, new_d_kv_s, new_d_win_p, new_d_win_s)
```

```python
import functools
import math

import numpy as np
import jax
import jax.numpy as jnp
from jax import lax
from jax.experimental import pallas as pl
from jax.experimental.pallas import tpu as pltpu

F32 = jnp.float32
BF16 = jnp.bfloat16

D_MODEL = 1024
PAST_LEN = 16384
PAGE_SIZE = 128
HEAD_DIM = 64
QBLOCK = 128
H_A, KVH_A, WINDOW_A = 8, 2, 128
H_B, DK_B, DV_B, CHUNK_B = 4, 64, 128, 64
P_C, H_C, G_C, N_C, CONV_W, CHUNK_C = 64, 8, 2, 128, 4, 128
D_INNER_C = H_C * P_C
CONV_DIM_C = D_INNER_C + 2 * G_C * N_C
H_D, KVH_D = 8, 2
CMP_BLOCK, SLC_BLOCK, TOP_N, WINDOW_D, SLC_QBLOCK = 32, 64, 16, 512, 32
D_FF, N_EXP, TOP_K = 3584, 8, 2
IN_E_SIZES = (H_A * HEAD_DIM, KVH_A * HEAD_DIM, KVH_A * HEAD_DIM, H_B * DK_B, H_B * DK_B, H_B * DV_B, H_B, H_B, H_B * DV_B)
IN_O_SIZES = (D_INNER_C, CONV_DIM_C, H_C, H_D * HEAD_DIM, 6 * KVH_D * HEAD_DIM, 3 * H_D)

LANE = 128
VMEM_LIMIT = 48 * 1024 * 1024
RMS_EPS = 1e-6
FF_TILE = 896
ROW_TILE = 512


def _round_up(n, m):
    return -(-n // m) * m


def _rms(x, g):
    return x * lax.rsqrt(jnp.mean(x * x, axis=-1, keepdims=True) + RMS_EPS) * g


def _norm_matmul_body(x_ref, g_ref, w_ref, o_ref):
    xn = _rms(x_ref[...], g_ref[...])
    o_ref[...] = jnp.dot(xn.astype(BF16), w_ref[...], preferred_element_type=F32)


def _norm_matmul(x, gain, w):
    m, k = x.shape
    n = w.shape[1]
    tm = min(ROW_TILE, m)
    return pl.pallas_call(
        _norm_matmul_body,
        grid=(m // tm,),
        in_specs=[pl.BlockSpec((tm, k), lambda i: (i, 0)),
                  pl.BlockSpec((1, k), lambda i: (0, 0)),
                  pl.BlockSpec((k, n), lambda i: (0, 0))],
        out_specs=pl.BlockSpec((tm, n), lambda i: (i, 0)),
        out_shape=jax.ShapeDtypeStruct((m, n), F32),
        compiler_params=pltpu.CompilerParams(dimension_semantics=("arbitrary",), vmem_limit_bytes=VMEM_LIMIT),
        name="norm_in_proj",
    )(x, gain.reshape(1, k), w)


def _out_proj_body(mix_ref, w_ref, res_ref, o_ref):
    o_ref[...] = res_ref[...] + jnp.dot(mix_ref[...].astype(BF16), w_ref[...], preferred_element_type=F32)


def _out_proj(mix, w, res):
    m, k = mix.shape
    n = w.shape[1]
    tm = min(ROW_TILE, m)
    return pl.pallas_call(
        _out_proj_body,
        grid=(m // tm,),
        in_specs=[pl.BlockSpec((tm, k), lambda i: (i, 0)),
                  pl.BlockSpec((k, n), lambda i: (0, 0)),
                  pl.BlockSpec((tm, n), lambda i: (i, 0))],
        out_specs=pl.BlockSpec((tm, n), lambda i: (i, 0)),
        out_shape=jax.ShapeDtypeStruct((m, n), F32),
        compiler_params=pltpu.CompilerParams(dimension_semantics=("arbitrary",), vmem_limit_bytes=VMEM_LIMIT),
        name="out_proj",
    )(mix, w, res)


def _out_proj_route_body(mix_ref, w_ref, res_ref, g_ref, r_ref, o_ref, xn_ref, lg_ref):
    x = res_ref[...] + jnp.dot(mix_ref[...].astype(BF16), w_ref[...], preferred_element_type=F32)
    o_ref[...] = x
    xn = _rms(x, g_ref[...])
    xn_ref[...] = xn.astype(BF16)
    lg_ref[...] = jnp.dot(xn, r_ref[...], preferred_element_type=F32, precision=lax.Precision.HIGHEST)


def _out_proj_route(mix, w, res, gain, router_pad):
    m, k = mix.shape
    n = w.shape[1]
    tm = min(ROW_TILE, m)
    return pl.pallas_call(
        _out_proj_route_body,
        grid=(m // tm,),
        in_specs=[pl.BlockSpec((tm, k), lambda i: (i, 0)),
                  pl.BlockSpec((k, n), lambda i: (0, 0)),
                  pl.BlockSpec((tm, n), lambda i: (i, 0)),
                  pl.BlockSpec((1, n), lambda i: (0, 0)),
                  pl.BlockSpec((n, LANE), lambda i: (0, 0))],
        out_specs=[pl.BlockSpec((tm, n), lambda i: (i, 0)),
                   pl.BlockSpec((tm, n), lambda i: (i, 0)),
                   pl.BlockSpec((tm, LANE), lambda i: (i, 0))],
        out_shape=[jax.ShapeDtypeStruct((m, n), F32),
                   jax.ShapeDtypeStruct((m, n), BF16),
                   jax.ShapeDtypeStruct((m, LANE), F32)],
        compiler_params=pltpu.CompilerParams(dimension_semantics=("arbitrary",), vmem_limit_bytes=VMEM_LIMIT),
        name="out_proj_route",
    )(mix, w, res, gain.reshape(1, n), router_pad)


def _ffn_body(x_ref, g_ref, wg_ref, wu_ref, wd_ref, o_ref, xn_ref, acc_ref):
    c = pl.program_id(1)

    @pl.when(c == 0)
    def _():
        xn_ref[...] = _rms(x_ref[...], g_ref[...]).astype(BF16)
        acc_ref[...] = jnp.zeros_like(acc_ref)

    xn = xn_ref[...]
    gate = jnp.dot(xn, wg_ref[...], preferred_element_type=F32)
    up = jnp.dot(xn, wu_ref[...], preferred_element_type=F32)
    h = (gate * jax.nn.sigmoid(gate) * up).astype(BF16)
    acc_ref[...] += jnp.dot(h, wd_ref[...], preferred_element_type=F32)

    @pl.when(c == pl.num_programs(1) - 1)
    def _():
        o_ref[...] = x_ref[...] + acc_ref[...]


def _ffn(x, gain, w_gu, w_down):
    m, d = x.shape
    tm = min(ROW_TILE, m)
    nc = D_FF // FF_TILE
    return pl.pallas_call(
        _ffn_body,
        grid=(m // tm, nc),
        in_specs=[pl.BlockSpec((tm, d), lambda i, c: (i, 0)),
                  pl.BlockSpec((1, d), lambda i, c: (0, 0)),
                  pl.BlockSpec((d, FF_TILE), lambda i, c: (0, c)),
                  pl.BlockSpec((d, FF_TILE), lambda i, c: (0, c + nc)),
                  pl.BlockSpec((FF_TILE, d), lambda i, c: (c, 0))],
        out_specs=pl.BlockSpec((tm, d), lambda i, c: (i, 0)),
        out_shape=jax.ShapeDtypeStruct((m, d), F32),
        scratch_shapes=[pltpu.VMEM((tm, d), BF16), pltpu.VMEM((tm, d), F32)],
        compiler_params=pltpu.CompilerParams(dimension_semantics=("arbitrary", "arbitrary"), vmem_limit_bytes=VMEM_LIMIT),
        name="ffn_dense",
    )(x, gain.reshape(1, d), w_gu, w_gu, w_down)


def _moe_body(te_ref, tv_ref, x_ref, s_ref, wg_ref, wu_ref, wd_ref, o_ref, acc_ref):
    i = pl.program_id(0)
    c = pl.program_id(1)
    valid = tv_ref[i] > 0

    @pl.when(c == 0)
    def _():
        acc_ref[...] = jnp.zeros_like(acc_ref)

    @pl.when(valid)
    def _():
        x = x_ref[...]
        gate = jnp.dot(x, wg_ref[0], preferred_element_type=F32)
        up = jnp.dot(x, wu_ref[0], preferred_element_type=F32)
        h = (gate * jax.nn.sigmoid(gate) * up).astype(BF16)
        acc_ref[...] += jnp.dot(h, wd_ref[0], preferred_element_type=F32)

    @pl.when(c == pl.num_programs(1) - 1)
    def _():
        o_ref[...] = s_ref[...] * acc_ref[...]


def _moe_grouped(xs, scale, tile_expert, tile_valid, w_gu, w_down, tm):
    p, d = xs.shape
    nc = D_FF // FF_TILE
    grid_spec = pltpu.PrefetchScalarGridSpec(
        num_scalar_prefetch=2,
        grid=(p // tm, nc),
        in_specs=[pl.BlockSpec((tm, d), lambda i, c, te, tv: (i, 0)),
                  pl.BlockSpec((tm, 1), lambda i, c, te, tv: (i, 0)),
                  pl.BlockSpec((1, d, FF_TILE), lambda i, c, te, tv: (te[i], 0, c)),
                  pl.BlockSpec((1, d, FF_TILE), lambda i, c, te, tv: (te[i], 0, c + nc)),
                  pl.BlockSpec((1, FF_TILE, d), lambda i, c, te, tv: (te[i], c, 0))],
        out_specs=pl.BlockSpec((tm, d), lambda i, c, te, tv: (i, 0)),
        scratch_shapes=[pltpu.VMEM((tm, d), F32)],
    )
    return pl.pallas_call(
        _moe_body,
        grid_spec=grid_spec,
        out_shape=jax.ShapeDtypeStruct((p, d), F32),
        compiler_params=pltpu.CompilerParams(dimension_semantics=("arbitrary", "arbitrary"), vmem_limit_bytes=VMEM_LIMIT),
        name="moe_grouped",
    )(tile_expert, tile_valid, xs, scale, w_gu, w_gu, w_down)


def _moe_layer(x_rows, xn_rows, logits, w_gu, w_down, tm=ROW_TILE):
    n = x_rows.shape[0]
    top_v, top_i = lax.top_k(logits, TOP_K)
    gate = jax.nn.softmax(top_v, axis=-1)
    flat_e = top_i.reshape(-1).astype(jnp.int32)
    na = flat_e.shape[0]
    order = jnp.argsort(flat_e, stable=True).astype(jnp.int32)
    counts = jnp.sum(jax.nn.one_hot(flat_e, N_EXP, dtype=jnp.int32), axis=0)
    tiles_per_e = (counts + tm - 1) // tm
    tile_end = jnp.cumsum(tiles_per_e)
    tile_start = tile_end - tiles_per_e
    group_start = jnp.cumsum(counts) - counts
    se = flat_e[order]
    pos = tile_start[se] * tm + (jnp.arange(na, dtype=jnp.int32) - group_start[se])
    n_tiles = na // tm + N_EXP
    p = n_tiles * tm
    tok_of_pos = jnp.zeros((p,), jnp.int32).at[pos].set(order // TOP_K)
    gate_of_pos = jnp.zeros((p,), F32).at[pos].set(gate.reshape(-1)[order])
    pos_of_a = jnp.zeros((na,), jnp.int32).at[order].set(pos)
    tile_ids = jnp.arange(n_tiles, dtype=jnp.int32)
    total_tiles = tile_end[-1]
    last_e = jnp.searchsorted(tile_end, total_tiles - 1, side="right").astype(jnp.int32)
    tile_expert = jnp.where(tile_ids < total_tiles,
                            jnp.searchsorted(tile_end, tile_ids, side="right").astype(jnp.int32), last_e)
    tile_expert = jnp.clip(tile_expert, 0, N_EXP - 1)
    tile_valid = (tile_ids < total_tiles).astype(jnp.int32)
    xs = xn_rows[tok_of_pos]
    ys = _moe_grouped(xs, gate_of_pos.reshape(p, 1), tile_expert, tile_valid, w_gu, w_down, tm)
    pa = pos_of_a.reshape(n, TOP_K)
    return x_rows + (ys[pa[:, 0]] + ys[pa[:, 1]])


def _rms_norm(x, g, eps=RMS_EPS):
    xf = x.astype(F32)
    y = xf * lax.rsqrt(jnp.mean(xf * xf, axis=-1, keepdims=True) + eps)
    return (y * g.astype(F32)).astype(x.dtype)


def _split_cols(x, sizes):
    return jnp.split(x, [int(s) for s in np.cumsum(sizes)[:-1]], axis=-1)


def _last_rows(a, n):
    t = a.shape[1]
    if t >= n:
        return a[:, t - n:]
    pad = [(0, 0)] * a.ndim
    pad[1] = (n - t, 0)
    return jnp.pad(a, pad)


def _chunked_scan(step, init, xs, chunk):
    t = xs[0].shape[1]
    nc = t // chunk

    def to_chunks(a):
        return a.reshape(a.shape[0], nc, chunk, *a.shape[2:]).swapaxes(0, 1)
    final, ys = lax.scan(step, init, tuple(to_chunks(a) for a in xs))
    ys = ys.swapaxes(0, 1)
    return final, ys.reshape(ys.shape[0], t, *ys.shape[3:])


def _softmax_attend(q, k, v, mask, sinks=None):
    s = jnp.einsum('...qgrd,...kgd->...grqk', q, k).astype(F32) * (HEAD_DIM ** -0.5)
    s = jnp.where(mask[..., None, None, :, :], s, -jnp.inf)
    m = jnp.max(s, axis=-1, keepdims=True)
    if sinks is not None:
        sk = sinks.astype(F32)[:, :, None, None]
        m = jnp.maximum(m, sk)
    m = jnp.where(jnp.isfinite(m), m, 0.0)
    p = jnp.exp(s - m)
    den = jnp.sum(p, axis=-1, keepdims=True)
    if sinks is not None:
        den = den + jnp.exp(sk - m)
    p = p / jnp.where(den > 0, den, 1.0)
    out = jnp.einsum('...grqk,...kgd->...qgrd', p.astype(v.dtype), v)
    return out, p


def _banded_window_attention(q, k, v, window, sinks):
    bsz, s_len, g, r, d = q.shape
    nb = s_len // QBLOCK
    span = window + QBLOCK
    pad = ((0, 0), (window, 0), (0, 0), (0, 0))
    kp, vp = jnp.pad(k, pad), jnp.pad(v, pad)
    qb = q.reshape(bsz, nb, QBLOCK, g, r, d).swapaxes(0, 1)

    def block(args):
        i, qi = args
        start = i * QBLOCK
        ki = lax.dynamic_slice_in_dim(kp, start, span, axis=1)
        vi = lax.dynamic_slice_in_dim(vp, start, span, axis=1)
        q_pos = start + jnp.arange(QBLOCK)
        k_pos = start - window + jnp.arange(span)
        diff = q_pos[:, None] - k_pos[None, :]
        mask = (k_pos[None, :] >= 0) & (diff >= 0) & (diff < window)
        return _softmax_attend(qi, ki, vi, mask, sinks)[0]
    out = lax.map(block, (jnp.arange(nb), qb))
    return out.swapaxes(0, 1).reshape(bsz, s_len, g, r, d)


def _window_step_attention(q, k_new, v_new, buf, window, sinks):
    t = k_new.shape[1]
    k = jnp.concatenate([buf[:, :, 0], k_new], axis=1)
    v = jnp.concatenate([buf[:, :, 1], v_new], axis=1)
    q_pos = PAST_LEN + jnp.arange(t)
    k_pos = PAST_LEN - window + jnp.arange(window + t)
    diff = q_pos[:, None] - k_pos[None, :]
    mask = (k_pos[None, :] >= 0) & (diff >= 0) & (diff < window)
    out, _ = _softmax_attend(q, k, v, mask, sinks)
    new_buf = jnp.concatenate([buf, jnp.stack([k_new, v_new], axis=2)], axis=1)[:, t:]
    return out, new_buf


def _mlstm_chunk(state, inputs):
    c, n, m = state
    q, k, v, ig, lf = inputs
    L = q.shape[1]
    f_cum = jnp.cumsum(lf, axis=1)
    causal = jnp.tril(jnp.ones((L, L), dtype=bool))
    log_d = jnp.where(causal[None, :, :, None], f_cum[:, :, None, :] - f_cum[:, None, :, :] + ig[:, None, :, :], -jnp.inf)
    m_inter = f_cum + m[:, None, :]
    m_t = jnp.maximum(m_inter, jnp.max(log_d, axis=2))
    d_mat = jnp.exp(log_d - m_t[:, :, None, :])
    w_inter = jnp.exp(m_inter - m_t)
    qk = jnp.einsum('bthd,bshd->btsh', q, k) * d_mat
    num = jnp.einsum('btsh,bshv->bthv', qk, v) + w_inter[..., None] * jnp.einsum('bhvd,bthd->bthv', c, q)
    den = jnp.sum(qk, axis=2) + w_inter * jnp.einsum('bhd,bthd->bth', n, q)
    h = num / jnp.maximum(jnp.abs(den), jnp.exp(-m_t))[..., None]
    m_new = m_t[:, -1]
    w_end = jnp.exp(f_cum[:, -1:] - f_cum + ig - m_new[:, None, :])
    decay = jnp.exp(f_cum[:, -1] + m - m_new)
    c_new = decay[..., None, None] * c + jnp.einsum('bsh,bshv,bshd->bhvd', w_end, v, k)
    n_new = decay[..., None] * n + jnp.einsum('bsh,bshd->bhd', w_end, k)
    return (c_new, n_new, m_new), h


def _ssd_chunk(h, inputs, a_neg):
    x, dt, bm, cm = inputs
    bsz, L = x.shape[:2]
    r = H_C // G_C
    cum = jnp.cumsum(dt * a_neg, axis=1)
    causal = jnp.tril(jnp.ones((L, L), dtype=bool))
    seg = jnp.where(causal[None, :, :, None], cum[:, :, None, :] - cum[:, None, :, :], -jnp.inf)
    decay = jnp.exp(seg).reshape(bsz, L, L, G_C, r)
    dtx = (dt[..., None] * x).reshape(bsz, L, G_C, r, P_C)
    cb = jnp.einsum('btgn,bsgn->btsg', cm, bm)
    y = jnp.einsum('btsg,btsgr,bsgrp->btgrp', cb, decay, dtx)
    hg = h.reshape(bsz, G_C, r, P_C, N_C)
    y = y + jnp.einsum('btgn,bgrpn->btgrp', cm, hg) * jnp.exp(cum).reshape(bsz, L, G_C, r)[..., None]
    w_end = jnp.exp(cum[:, -1:] - cum).reshape(bsz, L, G_C, r)
    h_new = hg * jnp.exp(cum[:, -1]).reshape(bsz, G_C, r)[..., None, None] + jnp.einsum('bsgr,bsgrp,bsgn->bgrpn', w_end, dtx, bm)
    return h_new.reshape(bsz, H_C, P_C, N_C), y.reshape(bsz, L, H_C, P_C)


def _causal_conv(xbc, buf, w, b):
    t = xbc.shape[1]
    xp = jnp.concatenate([buf, xbc], axis=1)
    out = b + sum(xp[:, j:j + t] * w[j] for j in range(CONV_W))
    return jax.nn.silu(out), xp[:, t:]


def _compress(rows, pos, w1, w2):
    bsz, t, g, d = rows.shape
    nb = t // CMP_BLOCK
    blk = rows.reshape(bsz, nb, CMP_BLOCK, g, d) + pos[:, None, :]
    blk = blk.transpose(0, 1, 3, 2, 4).reshape(bsz, nb, g, CMP_BLOCK * d)
    return jax.nn.silu(blk @ w1) @ w2


def _select_attend(q, p_cmp, q_pos, gather, n_top):
    bsz, g, r, t, nbc = p_cmp.shape
    ratio = SLC_BLOCK // CMP_BLOCK
    imp = p_cmp.sum(axis=2).reshape(bsz, g, t, nbc // ratio, ratio).sum(axis=-1)
    blk = jnp.arange(nbc // ratio)[None, :]
    cur = (q_pos // SLC_BLOCK)[:, None]
    forced = (blk == 0) | (blk == cur) | (blk == cur - 1)
    score = jnp.where(blk > cur, -jnp.inf, jnp.where(forced, jnp.inf, imp))
    _, sel = lax.top_k(score, n_top)
    pos = (sel[..., None] * SLC_BLOCK + jnp.arange(SLC_BLOCK)).reshape(bsz, g, t, n_top * SLC_BLOCK)
    k, v = gather(pos)
    s = jnp.einsum('btgrd,bgtkd->bgrtk', q, k).astype(F32) * (HEAD_DIM ** -0.5)
    s = jnp.where((pos <= q_pos[None, None, :, None])[:, :, None], s, -jnp.inf)
    p = jax.nn.softmax(s, axis=-1)
    return jnp.einsum('bgrtk,bgtkd->btgrd', p.astype(v.dtype), v)


def _pad_cols(w, n):
    return jnp.pad(w, ((0, 0), (0, n - w.shape[1])))


def _even_layer(x, w, st, prompt):
    bsz, t, _ = x.shape
    r_a = H_A // KVH_A
    n_in = sum(IN_E_SIZES)
    proj = _norm_matmul(x.reshape(bsz * t, D_MODEL), w['norm_mix'], w['w_in_p'])[:, :n_in].reshape(bsz, t, n_in)
    qa, ka, va, qb, kb, vb, ib, fb, ob = _split_cols(proj, IN_E_SIZES)
    qa = _rms_norm(qa.reshape(bsz, t, KVH_A, r_a, HEAD_DIM), w['q_norm'])
    ka = _rms_norm(ka.reshape(bsz, t, KVH_A, HEAD_DIM), w['k_norm'])
    va = va.reshape(bsz, t, KVH_A, HEAD_DIM)
    sinks = w['sinks'].reshape(KVH_A, r_a)
    if prompt:
        o_a = _banded_window_attention(qa, ka, va, WINDOW_A, sinks)
        new_win = _last_rows(jnp.stack([ka, va], axis=2), WINDOW_A)
    else:
        o_a, new_win = _window_step_attention(qa, ka, va, st['a_win'], WINDOW_A, sinks)
    qb = qb.reshape(bsz, t, H_B, DK_B)
    kb = kb.reshape(bsz, t, H_B, DK_B) * (DK_B ** -0.5)
    vb = vb.reshape(bsz, t, H_B, DV_B)
    ig = ib + w['b_igate']
    lf = jax.nn.log_sigmoid(fb + w['b_fgate'])
    if prompt:
        init = (jnp.zeros((bsz, H_B, DV_B, DK_B), F32), jnp.zeros((bsz, H_B, DK_B), F32), jnp.zeros((bsz, H_B), F32))
        (c_new, n_new, m_new), hb = _chunked_scan(_mlstm_chunk, init, (qb, kb, vb, ig, lf), CHUNK_B)
    else:
        init = (st['b_c'], st['b_n'], st['b_m'])
        (c_new, n_new, m_new), hb = _mlstm_chunk(init, (qb, kb, vb, ig, lf))
    hb = _rms_norm(hb, w['h_norm']) * jax.nn.sigmoid(ob.reshape(bsz, t, H_B, DV_B))
    mix = jnp.concatenate([o_a.reshape(bsz, t, H_A * HEAD_DIM), hb.reshape(bsz, t, H_B * DV_B)], axis=-1)
    xr = _out_proj(mix.reshape(bsz * t, D_MODEL), w['w_out'], x.reshape(bsz * t, D_MODEL))
    xr = _ffn(xr, w['norm_ffn'], w['w_gu'], w['w_down'])
    return xr.reshape(bsz, t, D_MODEL), (new_win, c_new, n_new, m_new)


def _odd_mixers(x, w, st, page_table, prompt):
    bsz, t, _ = x.shape
    r_d = H_D // KVH_D
    n_in = sum(IN_O_SIZES)
    proj = _norm_matmul(x.reshape(bsz * t, D_MODEL), w['norm_mix'], w['w_in_p'])[:, :n_in].reshape(bsz, t, n_in)
    zc, xbc, dtc, qd, kvd, gd = _split_cols(proj, IN_O_SIZES)
    buf = jnp.zeros((bsz, CONV_W - 1, CONV_DIM_C), x.dtype) if prompt else st['c_conv']
    xbc, new_conv = _causal_conv(xbc, buf, w['conv_w'], w['conv_b'])
    xc, bc, cc = _split_cols(xbc, (D_INNER_C, G_C * N_C, G_C * N_C))
    xc = xc.reshape(bsz, t, H_C, P_C)
    bc = bc.reshape(bsz, t, G_C, N_C)
    cc = cc.reshape(bsz, t, G_C, N_C)
    dt = jax.nn.softplus(dtc + w['dt_bias'])
    step = functools.partial(_ssd_chunk, a_neg=-jnp.exp(w['a_log']))
    if prompt:
        h_new, yc = _chunked_scan(step, jnp.zeros((bsz, H_C, P_C, N_C), F32), (xc, dt, bc, cc), CHUNK_C)
    else:
        h_new, yc = step(st['c_ssm'], (xc, dt, bc, cc))
    yc = yc + w['d_skip'][:, None] * xc
    yc = yc.reshape(bsz, t, D_INNER_C) * jax.nn.silu(zc)
    yc = _rms_norm(yc.reshape(bsz, t, G_C, D_INNER_C // G_C), w['y_norm'].reshape(G_C, D_INNER_C // G_C))
    qd = _rms_norm(qd.reshape(bsz, t, KVH_D, r_d, HEAD_DIM), w['q_norm'])
    kc, vc, ks, vs, kw, vw = [a.reshape(bsz, t, KVH_D, HEAD_DIM) for a in _split_cols(kvd, (KVH_D * HEAD_DIM,) * 6)]
    ks = _rms_norm(ks, w['k_norm'])
    kw = _rms_norm(kw, w['k_norm'])
    new_kv = jnp.stack([kc, vc, ks, vs], axis=2)
    q_pos = jnp.arange(t) + (0 if prompt else PAST_LEN)
    ks_t, vs_t = ks.transpose(0, 2, 1, 3), vs.transpose(0, 2, 1, 3)
    bi = jnp.arange(bsz)[:, None, None, None]
    gi = jnp.arange(KVH_D)[None, :, None, None]

    def comp(rows, j):
        return _compress(rows, w['cmp_pos'][j], w['cmp_w1'][j], w['cmp_w2'][j])
    if prompt:
        kcmp, vcmp = comp(kc, 0), comp(vc, 1)

        def gather(pos):
            return ks_t[bi, gi, pos], vs_t[bi, gi, pos]
    else:
        pool, li = st['d_kv'], st['layer']
        n_pages = page_table.shape[1]
        past_k = pool[li, page_table, :, 0].reshape(bsz, PAST_LEN, KVH_D, HEAD_DIM)
        past_v = pool[li, page_table, :, 1].reshape(bsz, PAST_LEN, KVH_D, HEAD_DIM)
        pad_t = -(-t // SLC_BLOCK) * SLC_BLOCK
        padw = ((0, 0), (0, pad_t - t), (0, 0), (0, 0))
        kcmp = jnp.concatenate([comp(past_k, 0), comp(jnp.pad(kc, padw), 0)], axis=1)
        vcmp = jnp.concatenate([comp(past_v, 1), comp(jnp.pad(vc, padw), 1)], axis=1)

        def gather(pos):
            phys = page_table[bi, jnp.clip(pos // PAGE_SIZE, 0, n_pages - 1)]
            off = pos % PAGE_SIZE
            tail = jnp.clip(pos - PAST_LEN, 0, t - 1)
            past = (pos < PAST_LEN)[..., None]
            k = jnp.where(past, pool[li, phys, off, 2, gi], ks_t[bi, gi, tail])
            v = jnp.where(past, pool[li, phys, off, 3, gi], vs_t[bi, gi, tail])
            return k, v
    kcmp = _rms_norm(kcmp, w['k_norm'])
    blk_end = jnp.arange(kcmp.shape[1]) * CMP_BLOCK + (CMP_BLOCK - 1)
    o_cmp, p_cmp = _softmax_attend(qd, kcmp, vcmp, blk_end[None, :] <= q_pos[:, None])
    n_top = min(TOP_N, kcmp.shape[1] // (SLC_BLOCK // CMP_BLOCK))
    if prompt:
        nqb = t // SLC_QBLOCK
        q_blocks = qd.reshape(bsz, nqb, SLC_QBLOCK, KVH_D, r_d, HEAD_DIM).swapaxes(0, 1)
        p_blocks = p_cmp.reshape(bsz, KVH_D, r_d, nqb, SLC_QBLOCK, -1).transpose(3, 0, 1, 2, 4, 5)
        pos_blocks = q_pos.reshape(nqb, SLC_QBLOCK)
        o_slc = lax.map(lambda a: _select_attend(a[0], a[1], a[2], gather, n_top), (q_blocks, p_blocks, pos_blocks))
        o_slc = o_slc.swapaxes(0, 1).reshape(bsz, t, KVH_D, r_d, HEAD_DIM)
        o_win = _banded_window_attention(qd, kw, vw, WINDOW_D, None)
        new_win = _last_rows(jnp.stack([kw, vw], axis=2), WINDOW_D)
    else:
        o_slc = _select_attend(qd, p_cmp, q_pos, gather, n_top)
        o_win, new_win = _window_step_attention(qd, kw, vw, st['d_win'], WINDOW_D, None)
    g = jax.nn.sigmoid(gd).reshape(bsz, t, KVH_D, r_d, 3)
    o_d = g[..., 0:1] * o_cmp + g[..., 1:2] * o_slc + g[..., 2:3] * o_win
    mix = jnp.concatenate([yc.reshape(bsz, t, D_INNER_C), o_d.reshape(bsz, t, H_D * HEAD_DIM)], axis=-1)
    return mix.reshape(bsz * t, D_MODEL), (h_new, new_conv, new_kv, new_win)


def kernel(x_prompt, x_sample, cache_a_win, state_b_c, state_b_n, state_b_m, state_c_ssm, state_c_conv, cache_d_kv, cache_d_win, page_table, e_norm_mix, e_w_in, e_q_norm, e_k_norm, e_sinks, e_b_igate, e_b_fgate, e_h_norm, e_w_out, e_norm_ffn, e_w_gu, e_w_down, o_norm_mix, o_w_in, o_conv_w, o_conv_b, o_dt_bias, o_a_log, o_d_skip, o_y_norm, o_q_norm, o_k_norm, o_cmp_pos, o_cmp_w1, o_cmp_w2, o_w_out, o_norm_ffn, o_router, o_w_gu, o_w_down):
    i = 0
    we = {'norm_mix': e_norm_mix[i], 'w_in_p': _pad_cols(e_w_in[i], _round_up(sum(IN_E_SIZES), LANE)).astype(BF16),
          'q_norm': e_q_norm[i], 'k_norm': e_k_norm[i],
          'sinks': e_sinks[i], 'b_igate': e_b_igate[i], 'b_fgate': e_b_fgate[i], 'h_norm': e_h_norm[i],
          'w_out': e_w_out[i].astype(BF16), 'norm_ffn': e_norm_ffn[i], 'w_gu': e_w_gu[i].astype(BF16),
          'w_down': e_w_down[i].astype(BF16)}
    wo = {'norm_mix': o_norm_mix[i], 'w_in_p': _pad_cols(o_w_in[i], _round_up(sum(IN_O_SIZES), LANE)).astype(BF16),
          'conv_w': o_conv_w[i], 'conv_b': o_conv_b[i],
          'dt_bias': o_dt_bias[i], 'a_log': o_a_log[i], 'd_skip': o_d_skip[i], 'y_norm': o_y_norm[i],
          'q_norm': o_q_norm[i], 'k_norm': o_k_norm[i], 'cmp_pos': o_cmp_pos[i], 'cmp_w1': o_cmp_w1[i],
          'cmp_w2': o_cmp_w2[i], 'w_out': o_w_out[i].astype(BF16), 'norm_ffn': o_norm_ffn[i],
          'router': _pad_cols(o_router[i], LANE), 'w_gu': o_w_gu[i].astype(BF16), 'w_down': o_w_down[i].astype(BF16)}
    st_e = {'a_win': cache_a_win[i], 'b_c': state_b_c[i], 'b_n': state_b_n[i], 'b_m': state_b_m[i]}
    st_o = {'c_ssm': state_c_ssm[i], 'c_conv': state_c_conv[i], 'd_kv': cache_d_kv, 'layer': i,
            'd_win': cache_d_win[i]}

    xp, sp_e = _even_layer(x_prompt, we, None, True)
    xs, ss_e = _even_layer(x_sample, we, st_e, False)

    mix_p, sp_o = _odd_mixers(xp, wo, None, page_table, True)
    mix_s, ss_o = _odd_mixers(xs, wo, st_o, page_table, False)
    np_rows = mix_p.shape[0]
    xp2, xnp, lgp = _out_proj_route(mix_p, wo['w_out'], xp.reshape(np_rows, D_MODEL), wo['norm_ffn'], wo['router'])
    xs2, xns, lgs = _out_proj_route(mix_s, wo['w_out'], xs.reshape(-1, D_MODEL), wo['norm_ffn'], wo['router'])
    x_all = jnp.concatenate([xp2, xs2], axis=0)
    xn_all = jnp.concatenate([xnp, xns], axis=0)
    lg_all = jnp.concatenate([lgp, lgs], axis=0)[:, :N_EXP]
    y_all = _moe_layer(x_all, xn_all, lg_all, wo['w_gu'], wo['w_down'])
    y_prompt = y_all[:np_rows].reshape(x_prompt.shape)
    y_sample = y_all[np_rows:].reshape(x_sample.shape)

    def one(a):
        return a[None]
    return (y_prompt, y_sample, one(sp_e[0]), one(ss_e[0]), one(sp_e[1]), one(ss_e[1]), one(sp_e[2]), one(ss_e[2]),
            one(sp_e[3]), one(ss_e[3]), one(sp_o[0]), one(ss_o[0]), one(sp_o[1]), one(ss_o[1]),
            one(sp_o[2]), one(ss_o[2]), one(sp_o[3]), one(ss_o[3]))
```

```python
import functools
import math

import numpy as np
import jax
import jax.numpy as jnp
from jax import lax
from jax.experimental import pallas as pl
from jax.experimental.pallas import tpu as pltpu

F32 = jnp.float32
BF16 = jnp.bfloat16

D_MODEL = 1024
PAST_LEN = 16384
PAGE_SIZE = 128
HEAD_DIM = 64
QBLOCK = 128
H_A, KVH_A, WINDOW_A = 8, 2, 128
H_B, DK_B, DV_B, CHUNK_B = 4, 64, 128, 64
P_C, H_C, G_C, N_C, CONV_W, CHUNK_C = 64, 8, 2, 128, 4, 128
D_INNER_C = H_C * P_C
CONV_DIM_C = D_INNER_C + 2 * G_C * N_C
H_D, KVH_D = 8, 2
CMP_BLOCK, SLC_BLOCK, TOP_N, WINDOW_D, SLC_QBLOCK = 32, 64, 16, 512, 32
D_FF, N_EXP, TOP_K = 3584, 8, 2
IN_E_SIZES = (H_A * HEAD_DIM, KVH_A * HEAD_DIM, KVH_A * HEAD_DIM, H_B * DK_B, H_B * DK_B, H_B * DV_B, H_B, H_B, H_B * DV_B)
IN_O_SIZES = (D_INNER_C, CONV_DIM_C, H_C, H_D * HEAD_DIM, 6 * KVH_D * HEAD_DIM, 3 * H_D)

LANE = 128
VMEM_LIMIT = 48 * 1024 * 1024
RMS_EPS = 1e-6
FF_TILE = 896
ROW_TILE = 512


def _round_up(n, m):
    return -(-n // m) * m


def _rms(x, g):
    return x * lax.rsqrt(jnp.mean(x * x, axis=-1, keepdims=True) + RMS_EPS) * g


def _norm_matmul_body(x_ref, g_ref, w_ref, o_ref):
    xn = _rms(x_ref[...], g_ref[...])
    o_ref[...] = jnp.dot(xn.astype(BF16), w_ref[...], preferred_element_type=F32)


def _norm_matmul(x, gain, w):
    m, k = x.shape
    n = w.shape[1]
    tm = min(ROW_TILE, m)
    return pl.pallas_call(
        _norm_matmul_body,
        grid=(m // tm,),
        in_specs=[pl.BlockSpec((tm, k), lambda i: (i, 0)),
                  pl.BlockSpec((1, k), lambda i: (0, 0)),
                  pl.BlockSpec((k, n), lambda i: (0, 0))],
        out_specs=pl.BlockSpec((tm, n), lambda i: (i, 0)),
        out_shape=jax.ShapeDtypeStruct((m, n), F32),
        compiler_params=pltpu.CompilerParams(dimension_semantics=("arbitrary",), vmem_limit_bytes=VMEM_LIMIT),
        name="norm_in_proj",
    )(x, gain.reshape(1, k), w)


def _out_proj_body(mix_ref, w_ref, res_ref, o_ref):
    o_ref[...] = res_ref[...] + jnp.dot(mix_ref[...].astype(BF16), w_ref[...], preferred_element_type=F32)


def _out_proj(mix, w, res):
    m, k = mix.shape
    n = w.shape[1]
    tm = min(ROW_TILE, m)
    return pl.pallas_call(
        _out_proj_body,
        grid=(m // tm,),
        in_specs=[pl.BlockSpec((tm, k), lambda i: (i, 0)),
                  pl.BlockSpec((k, n), lambda i: (0, 0)),
                  pl.BlockSpec((tm, n), lambda i: (i, 0))],
        out_specs=pl.BlockSpec((tm, n), lambda i: (i, 0)),
        out_shape=jax.ShapeDtypeStruct((m, n), F32),
        compiler_params=pltpu.CompilerParams(dimension_semantics=("arbitrary",), vmem_limit_bytes=VMEM_LIMIT),
        name="out_proj",
    )(mix, w, res)


def _out_proj_route_body(mix_ref, w_ref, res_ref, g_ref, r_ref, o_ref, xn_ref, lg_ref):
    x = res_ref[...] + jnp.dot(mix_ref[...].astype(BF16), w_ref[...], preferred_element_type=F32)
    o_ref[...] = x
    xn = _rms(x, g_ref[...])
    xn_ref[...] = xn.astype(BF16)
    lg_ref[...] = jnp.dot(xn, r_ref[...], preferred_element_type=F32, precision=lax.Precision.HIGHEST)


def _out_proj_route(mix, w, res, gain, router_pad):
    m, k = mix.shape
    n = w.shape[1]
    tm = min(ROW_TILE, m)
    return pl.pallas_call(
        _out_proj_route_body,
        grid=(m // tm,),
        in_specs=[pl.BlockSpec((tm, k), lambda i: (i, 0)),
                  pl.BlockSpec((k, n), lambda i: (0, 0)),
                  pl.BlockSpec((tm, n), lambda i: (i, 0)),
                  pl.BlockSpec((1, n), lambda i: (0, 0)),
                  pl.BlockSpec((n, LANE), lambda i: (0, 0))],
        out_specs=[pl.BlockSpec((tm, n), lambda i: (i, 0)),
                   pl.BlockSpec((tm, n), lambda i: (i, 0)),
                   pl.BlockSpec((tm, LANE), lambda i: (i, 0))],
        out_shape=[jax.ShapeDtypeStruct((m, n), F32),
                   jax.ShapeDtypeStruct((m, n), BF16),
                   jax.ShapeDtypeStruct((m, LANE), F32)],
        compiler_params=pltpu.CompilerParams(dimension_semantics=("arbitrary",), vmem_limit_bytes=VMEM_LIMIT),
        name="out_proj_route",
    )(mix, w, res, gain.reshape(1, n), router_pad)


def _ffn_body(x_ref, g_ref, wg_ref, wu_ref, wd_ref, o_ref, xn_ref, acc_ref):
    c = pl.program_id(1)

    @pl.when(c == 0)
    def _():
        xn_ref[...] = _rms(x_ref[...], g_ref[...]).astype(BF16)
        acc_ref[...] = jnp.zeros_like(acc_ref)

    xn = xn_ref[...]
    gate = jnp.dot(xn, wg_ref[...], preferred_element_type=F32)
    up = jnp.dot(xn, wu_ref[...], preferred_element_type=F32)
    h = (gate * jax.nn.sigmoid(gate) * up).astype(BF16)
    acc_ref[...] += jnp.dot(h, wd_ref[...], preferred_element_type=F32)

    @pl.when(c == pl.num_programs(1) - 1)
    def _():
        o_ref[...] = x_ref[...] + acc_ref[...]


def _ffn(x, gain, w_gu, w_down):
    m, d = x.shape
    tm = min(ROW_TILE, m)
    nc = D_FF // FF_TILE
    return pl.pallas_call(
        _ffn_body,
        grid=(m // tm, nc),
        in_specs=[pl.BlockSpec((tm, d), lambda i, c: (i, 0)),
                  pl.BlockSpec((1, d), lambda i, c: (0, 0)),
                  pl.BlockSpec((d, FF_TILE), lambda i, c: (0, c)),
                  pl.BlockSpec((d, FF_TILE), lambda i, c: (0, c + nc)),
                  pl.BlockSpec((FF_TILE, d), lambda i, c: (c, 0))],
        out_specs=pl.BlockSpec((tm, d), lambda i, c: (i, 0)),
        out_shape=jax.ShapeDtypeStruct((m, d), F32),
        scratch_shapes=[pltpu.VMEM((tm, d), BF16), pltpu.VMEM((tm, d), F32)],
        compiler_params=pltpu.CompilerParams(dimension_semantics=("arbitrary", "arbitrary"), vmem_limit_bytes=VMEM_LIMIT),
        name="ffn_dense",
    )(x, gain.reshape(1, d), w_gu, w_gu, w_down)


def _moe_body(te_ref, tv_ref, x_ref, s_ref, wg_ref, wu_ref, wd_ref, o_ref, acc_ref):
    i = pl.program_id(0)
    c = pl.program_id(1)
    valid = tv_ref[i] > 0

    @pl.when(c == 0)
    def _():
        acc_ref[...] = jnp.zeros_like(acc_ref)

    @pl.when(valid)
    def _():
        x = x_ref[...]
        gate = jnp.dot(x, wg_ref[0], preferred_element_type=F32)
        up = jnp.dot(x, wu_ref[0], preferred_element_type=F32)
        h = (gate * jax.nn.sigmoid(gate) * up).astype(BF16)
        acc_ref[...] += jnp.dot(h, wd_ref[0], preferred_element_type=F32)

    @pl.when(c == pl.num_programs(1) - 1)
    def _():
        o_ref[...] = s_ref[...] * acc_ref[...]


def _moe_grouped(xs, scale, tile_expert, tile_valid, w_gu, w_down, tm):
    p, d = xs.shape
    nc = D_FF // FF_TILE
    grid_spec = pltpu.PrefetchScalarGridSpec(
        num_scalar_prefetch=2,
        grid=(p // tm, nc),
        in_specs=[pl.BlockSpec((tm, d), lambda i, c, te, tv: (i, 0)),
                  pl.BlockSpec((tm, 1), lambda i, c, te, tv: (i, 0)),
                  pl.BlockSpec((1, d, FF_TILE), lambda i, c, te, tv: (te[i], 0, c)),
                  pl.BlockSpec((1, d, FF_TILE), lambda i, c, te, tv: (te[i], 0, c + nc)),
                  pl.BlockSpec((1, FF_TILE, d), lambda i, c, te, tv: (te[i], c, 0))],
        out_specs=pl.BlockSpec((tm, d), lambda i, c, te, tv: (i, 0)),
        scratch_shapes=[pltpu.VMEM((tm, d), F32)],
    )
    return pl.pallas_call(
        _moe_body,
        grid_spec=grid_spec,
        out_shape=jax.ShapeDtypeStruct((p, d), F32),
        compiler_params=pltpu.CompilerParams(dimension_semantics=("arbitrary", "arbitrary"), vmem_limit_bytes=VMEM_LIMIT),
        name="moe_grouped",
    )(tile_expert, tile_valid, xs, scale, w_gu, w_gu, w_down)


def _moe_layer(x_rows, xn_rows, logits, w_gu, w_down, tm=ROW_TILE):
    n = x_rows.shape[0]
    top_v, top_i = lax.top_k(logits, TOP_K)
    gate = jax.nn.softmax(top_v, axis=-1)
    flat_e = top_i.reshape(-1).astype(jnp.int32)
    na = flat_e.shape[0]
    order = jnp.argsort(flat_e, stable=True).astype(jnp.int32)
    counts = jnp.sum(jax.nn.one_hot(flat_e, N_EXP, dtype=jnp.int32), axis=0)
    tiles_per_e = (counts + tm - 1) // tm
    tile_end = jnp.cumsum(tiles_per_e)
    tile_start = tile_end - tiles_per_e
    group_start = jnp.cumsum(counts) - counts
    se = flat_e[order]
    pos = tile_start[se] * tm + (jnp.arange(na, dtype=jnp.int32) - group_start[se])
    n_tiles = na // tm + N_EXP
    p = n_tiles * tm
    tok_of_pos = jnp.zeros((p,), jnp.int32).at[pos].set(order // TOP_K)
    gate_of_pos = jnp.zeros((p,), F32).at[pos].set(gate.reshape(-1)[order])
    pos_of_a = jnp.zeros((na,), jnp.int32).at[order].set(pos)
    tile_ids = jnp.arange(n_tiles, dtype=jnp.int32)
    total_tiles = tile_end[-1]
    last_e = jnp.searchsorted(tile_end, total_tiles - 1, side="right").astype(jnp.int32)
    tile_expert = jnp.where(tile_ids < total_tiles,
                            jnp.searchsorted(tile_end, tile_ids, side="right").astype(jnp.int32), last_e)
    tile_expert = jnp.clip(tile_expert, 0, N_EXP - 1)
    tile_valid = (tile_ids < total_tiles).astype(jnp.int32)
    xs = xn_rows[tok_of_pos]
    ys = _moe_grouped(xs, gate_of_pos.reshape(p, 1), tile_expert, tile_valid, w_gu, w_down, tm)
    pa = pos_of_a.reshape(n, TOP_K)
    return x_rows + (ys[pa[:, 0]] + ys[pa[:, 1]])


QT = 128
KT = 512
NEG = -1e30


def _cmp_select_body(q_ref, kc_ref, vc_ref, o_ref, sel_ref, *, n_top):
    qi = pl.program_id(2)
    r = q_ref.shape[2]
    q = q_ref[0, 0].reshape(r * QT, HEAD_DIM)
    s = lax.dot_general(q.astype(BF16), kc_ref[0, 0].astype(BF16), (((1,), (1,)), ((), ())),
                        preferred_element_type=F32) * (HEAD_DIM ** -0.5)
    row = lax.broadcasted_iota(jnp.int32, (r * QT, LANE), 0)
    lane = lax.broadcasted_iota(jnp.int32, (r * QT, LANE), 1)
    t_pos = qi * QT + (row & (QT - 1))
    s = jnp.where(lane * CMP_BLOCK + (CMP_BLOCK - 1) <= t_pos, s, -jnp.inf)
    m = jnp.max(s, axis=-1, keepdims=True)
    m = jnp.where(m > -jnp.inf, m, 0.0)
    p = jnp.exp(s - m)
    den = jnp.sum(p, axis=-1, keepdims=True)
    p = p / jnp.where(den > 0, den, 1.0)
    o = jnp.dot(p.astype(BF16), vc_ref[0, 0].astype(BF16), preferred_element_type=F32)
    o_ref[0, 0] = o.reshape(r, QT, HEAD_DIM)
    pt = p[0:QT]
    for h in range(1, r):
        pt = pt + p[h * QT:(h + 1) * QT]
    lane1 = lax.broadcasted_iota(jnp.int32, (QT, LANE), 1)
    row1 = lax.broadcasted_iota(jnp.int32, (QT, LANE), 0)
    even = (lane1 & 1) == 0
    lo = jnp.where(even, pt, pltpu.roll(pt, 1, 1))
    hi = jnp.where(even, pltpu.roll(pt, LANE - 1, 1), pt)
    imp = lo + hi
    blk = lane1 >> 1
    cur = (qi * QT + row1) >> 6
    forced = (blk == 0) | (blk == cur) | (blk == cur - 1)
    score = jnp.where(blk > cur, -jnp.inf, jnp.where(forced, jnp.inf, imp))
    gt_cnt = jnp.zeros((QT, LANE), F32)
    eq_cnt = jnp.zeros((QT, LANE), F32)
    lane_f = lane1.astype(F32)
    for j in range(LANE // 2):
        col = jnp.broadcast_to(score[:, 2 * j:2 * j + 1], (QT, LANE))
        gt_cnt = gt_cnt + jnp.where(col > score, 1.0, 0.0)
        later = jnp.where(lane_f >= 2.0 * j + 2.0, 1.0, 0.0)
        eq_cnt = eq_cnt + jnp.where(col == score, later, 0.0)
    rank = gt_cnt + eq_cnt
    sel_ref[0, 0] = jnp.where((rank < n_top) & (blk <= cur), 1.0, 0.0)


def _cmp_select(q, kcmp, vcmp, n_top):
    b, g, r, t, d = q.shape
    assert kcmp.shape == (b, g, LANE, d) and t // CMP_BLOCK == LANE
    return pl.pallas_call(
        functools.partial(_cmp_select_body, n_top=n_top),
        grid=(b, g, t // QT),
        in_specs=[pl.BlockSpec((1, 1, r, QT, d), lambda i, j, k: (i, j, 0, k, 0)),
                  pl.BlockSpec((1, 1, LANE, d), lambda i, j, k: (i, j, 0, 0)),
                  pl.BlockSpec((1, 1, LANE, d), lambda i, j, k: (i, j, 0, 0))],
        out_specs=[pl.BlockSpec((1, 1, r, QT, d), lambda i, j, k: (i, j, 0, k, 0)),
                   pl.BlockSpec((1, 1, QT, LANE), lambda i, j, k: (i, j, k, 0))],
        out_shape=[jax.ShapeDtypeStruct((b, g, r, t, d), F32),
                   jax.ShapeDtypeStruct((b, g, t, LANE), F32)],
        compiler_params=pltpu.CompilerParams(dimension_semantics=("arbitrary",) * 3, vmem_limit_bytes=VMEM_LIMIT),
        name="nsa_cmp_select",
    )(q, kcmp, vcmp)


def _slc_attend_body(q_ref, sel_ref, k_ref, v_ref, o_ref):
    qi = pl.program_id(2)
    r = q_ref.shape[2]
    rows = r * QT
    q = (q_ref[0, 0].reshape(rows, HEAD_DIM) * (HEAD_DIM ** -0.5)).astype(BF16)
    sel = sel_ref[0, 0]
    e_row = lax.broadcasted_iota(jnp.int32, (LANE, KT), 0)
    e_col = lax.broadcasted_iota(jnp.int32, (LANE, KT), 1)
    expand = jnp.where(e_row == 2 * (e_col >> 6), 1.0, 0.0).astype(BF16)
    t_pos = qi * QT + lax.broadcasted_iota(jnp.int32, (QT, KT), 0)
    k_off = lax.broadcasted_iota(jnp.int32, (QT, KT), 1)
    n_kt = (qi * QT + QT - 1) // KT + 1
    lanes_per_kt = 2 * (KT // SLC_BLOCK)

    def body(kt, carry):
        m, l, acc = carry
        start = pl.multiple_of(kt * KT, KT)
        k = k_ref[0, 0, pl.ds(start, KT), :].astype(BF16)
        v = v_ref[0, 0, pl.ds(start, KT), :].astype(BF16)
        s = lax.dot_general(q, k, (((1,), (1,)), ((), ())), preferred_element_type=F32)
        sel_kt = pltpu.roll(sel, (LANE - kt * lanes_per_kt) % LANE, 1).astype(BF16)
        picked = jnp.dot(sel_kt, expand, preferred_element_type=F32)
        bias = jnp.where((picked > 0.5) & (start + k_off <= t_pos), 0.0, NEG)
        s = s + jnp.concatenate([bias] * r, axis=0)
        m_new = jnp.maximum(m, jnp.max(s, axis=-1, keepdims=True))
        alpha = jnp.exp(m - m_new)
        p = jnp.exp(s - m_new)
        l = alpha * l + jnp.sum(p, axis=-1, keepdims=True)
        acc = alpha * acc + jnp.dot(p.astype(BF16), v, preferred_element_type=F32)
        return m_new, l, acc

    init = (jnp.full((rows, 1), NEG, F32), jnp.zeros((rows, 1), F32), jnp.zeros((rows, HEAD_DIM), F32))
    m, l, acc = lax.fori_loop(0, n_kt, body, init)
    o_ref[0, 0] = (acc / l).reshape(r, QT, HEAD_DIM)


def _slc_attend(q, sel, ks, vs):
    b, g, r, t, d = q.shape
    return pl.pallas_call(
        _slc_attend_body,
        grid=(b, g, t // QT),
        in_specs=[pl.BlockSpec((1, 1, r, QT, d), lambda i, j, k: (i, j, 0, k, 0)),
                  pl.BlockSpec((1, 1, QT, LANE), lambda i, j, k: (i, j, k, 0)),
                  pl.BlockSpec((1, 1, t, d), lambda i, j, k: (i, j, 0, 0)),
                  pl.BlockSpec((1, 1, t, d), lambda i, j, k: (i, j, 0, 0))],
        out_specs=pl.BlockSpec((1, 1, r, QT, d), lambda i, j, k: (i, j, 0, k, 0)),
        out_shape=jax.ShapeDtypeStruct((b, g, r, t, d), F32),
        compiler_params=pltpu.CompilerParams(dimension_semantics=("arbitrary",) * 3, vmem_limit_bytes=VMEM_LIMIT),
        name="nsa_slc_attend",
    )(q, sel, ks, vs)


CMP_HIDDEN = 128
CMP_W = 2 * KVH_D * HEAD_DIM
SLC_W = 2 * KVH_D * HEAD_DIM
BLK_PER_PAGE = PAGE_SIZE // CMP_BLOCK
PAGES_PER_STEP = 64


def _page_copy(pool_ref, buf_ref, sem_ref, layer, page, slot, j, br):
    return pltpu.make_async_copy(
        pool_ref.at[layer, page, :, pl.ds(br * LANE, LANE)],
        buf_ref[br].at[pl.ds((slot * PAGES_PER_STEP + j) * PAGE_SIZE, PAGE_SIZE), :],
        sem_ref.at[slot])


def _compress_pages_body(pt_ref, pool_ref, pos_ref, w1_ref, w2_ref, o_ref, bufk_ref, bufv_ref, sem_ref, *, layer, n_steps):
    buf_ref = (bufk_ref, bufv_ref)
    s = pl.program_id(0)
    slot = s % 2

    def start(step, slot_):
        for j in range(PAGES_PER_STEP):
            page = pt_ref[step * PAGES_PER_STEP + j]
            for br in range(2):
                _page_copy(pool_ref, buf_ref, sem_ref, layer, page, slot_, j, br).start()

    @pl.when(s == 0)
    def _():
        start(0, 0)

    @pl.when(s + 1 < n_steps)
    def _():
        start(s + 1, 1 - slot)

    for j in range(PAGES_PER_STEP):
        for br in range(2):
            _page_copy(pool_ref, buf_ref, sem_ref, layer, 0, slot, j, br).wait()

    m = PAGES_PER_STEP * BLK_PER_PAGE
    base = slot * (PAGES_PER_STEP * PAGE_SIZE)
    for br in range(2):
        acc = jnp.zeros((m, KVH_D * CMP_HIDDEN), F32)
        for r in range(CMP_BLOCK):
            x = buf_ref[br][pl.ds(base + r, m, stride=CMP_BLOCK), :] + pos_ref[br, r:r + 1, :]
            acc = acc + jnp.dot(x.astype(BF16), w1_ref[br, r], preferred_element_type=F32)
        h = acc * jax.nn.sigmoid(acc)
        o_ref[:, br * LANE:(br + 1) * LANE] = jnp.dot(h.astype(BF16), w2_ref[br], preferred_element_type=F32)


def _block_diag2(w):
    z = jnp.zeros_like(w)
    return jnp.concatenate([jnp.concatenate([w, z], axis=-1), jnp.concatenate([z, w], axis=-1)], axis=-2)


def _compress_pages(pool, page_table, cmp_pos, cmp_w1, cmp_w2, layer):
    bsz, n_pages = page_table.shape
    total = bsz * n_pages
    n_steps = total // PAGES_PER_STEP
    assert total % PAGES_PER_STEP == 0 and KVH_D == 2
    w1 = cmp_w1.reshape(2, CMP_BLOCK, HEAD_DIM, CMP_HIDDEN)
    w1_bd = _block_diag2(w1).astype(BF16)
    w2_bd = _block_diag2(cmp_w2).astype(BF16)
    pos = jnp.concatenate([cmp_pos, cmp_pos], axis=-1)
    m = PAGES_PER_STEP * BLK_PER_PAGE
    grid_spec = pltpu.PrefetchScalarGridSpec(
        num_scalar_prefetch=1,
        grid=(n_steps,),
        in_specs=[pl.BlockSpec(memory_space=pl.ANY),
                  pl.BlockSpec((2, CMP_BLOCK, LANE), lambda s, pt: (0, 0, 0)),
                  pl.BlockSpec((2, CMP_BLOCK, LANE, KVH_D * CMP_HIDDEN), lambda s, pt: (0, 0, 0, 0)),
                  pl.BlockSpec((2, KVH_D * CMP_HIDDEN, LANE), lambda s, pt: (0, 0, 0))],
        out_specs=pl.BlockSpec((m, CMP_W), lambda s, pt: (s, 0)),
        scratch_shapes=[pltpu.VMEM((2 * PAGES_PER_STEP * PAGE_SIZE, LANE), F32),
                        pltpu.VMEM((2 * PAGES_PER_STEP * PAGE_SIZE, LANE), F32),
                        pltpu.SemaphoreType.DMA((2,))],
    )
    return pl.pallas_call(
        functools.partial(_compress_pages_body, layer=layer, n_steps=n_steps),
        grid_spec=grid_spec,
        out_shape=jax.ShapeDtypeStruct((total * BLK_PER_PAGE, CMP_W), F32),
        compiler_params=pltpu.CompilerParams(dimension_semantics=("arbitrary",), vmem_limit_bytes=VMEM_LIMIT),
        name="nsa_compress_pages",
    )(page_table.reshape(-1), pool, pos, w1_bd, w2_bd)


def _slc_copy(pool_ref, buf_ref, sem_ref, layer, page, row0, slot, idx):
    return pltpu.make_async_copy(
        pool_ref.at[layer, page, pl.ds(row0, SLC_BLOCK), pl.ds(CMP_W, SLC_W)],
        buf_ref.at[slot, pl.ds(idx * SLC_BLOCK, SLC_BLOCK), :],
        sem_ref.at[slot])


def _slc_step_body(pt_ref, sel_ref, pool_ref, q_ref, kn_ref, vn_ref, o_ref, buf_ref, sem_ref, *,
                   layer, n_seq, n_pages, new_blk):
    b = pl.program_id(0)
    slot = b % 2
    n_sel = KVH_D * TOP_N

    def start(seq, slot_):
        for i in range(n_sel):
            blk = jnp.minimum(sel_ref[seq * n_sel + i], new_blk - 1)
            page = pt_ref[seq * n_pages + (blk >> 1)]
            _slc_copy(pool_ref, buf_ref, sem_ref, layer, page, (blk & 1) * SLC_BLOCK, slot_, i).start()

    @pl.when(b == 0)
    def _():
        start(0, 0)

    @pl.when(b + 1 < n_seq)
    def _():
        start(b + 1, 1 - slot)

    for i in range(n_sel):
        _slc_copy(pool_ref, buf_ref, sem_ref, layer, 0, 0, slot, i).wait()

    nk = TOP_N * SLC_BLOCK
    lane = lax.broadcasted_iota(jnp.int32, (8, LANE), 1)
    col_blk = lax.broadcasted_iota(jnp.int32, (8, nk), 1) >> 6
    for g in range(KVH_D):
        in_g = (lane >= g * HEAD_DIM) & (lane < (g + 1) * HEAD_DIM)
        q = jnp.where(in_g, q_ref[0, g], 0.0) * (HEAD_DIM ** -0.5)
        kv = buf_ref[slot, pl.ds(g * nk, nk), :]
        k = kv[:, 0:LANE]
        v = kv[:, LANE:2 * LANE]
        s = lax.dot_general(q.astype(BF16), k.astype(BF16), (((1,), (1,)), ((), ())), preferred_element_type=F32)
        bias = jnp.zeros((8, nk), F32)
        for i in range(TOP_N):
            is_new = sel_ref[b * n_sel + g * TOP_N + i] >= new_blk
            bias = jnp.where((col_blk == i) & is_new, NEG, bias)
        s = s + bias
        kn = kn_ref[0]
        s_new = jnp.sum(q * kn, axis=-1, keepdims=True)
        m = jnp.maximum(jnp.max(s, axis=-1, keepdims=True), s_new)
        p = jnp.exp(s - m)
        p_new = jnp.exp(s_new - m)
        den = jnp.sum(p, axis=-1, keepdims=True) + p_new
        o = jnp.dot(p.astype(BF16), v.astype(BF16), preferred_element_type=F32)
        o = o + p_new * vn_ref[0]
        o_ref[0, g] = o / den


def _slc_step(pool, page_table, sel, q, k_new, v_new, layer):
    bsz, n_pages = page_table.shape
    n_sel = KVH_D * TOP_N
    grid_spec = pltpu.PrefetchScalarGridSpec(
        num_scalar_prefetch=2,
        grid=(bsz,),
        in_specs=[pl.BlockSpec(memory_space=pl.ANY),
                  pl.BlockSpec((1, KVH_D, 8, LANE), lambda b, pt, sl: (b, 0, 0, 0)),
                  pl.BlockSpec((1, 1, LANE), lambda b, pt, sl: (b, 0, 0)),
                  pl.BlockSpec((1, 1, LANE), lambda b, pt, sl: (b, 0, 0))],
        out_specs=pl.BlockSpec((1, KVH_D, 8, LANE), lambda b, pt, sl: (b, 0, 0, 0)),
        scratch_shapes=[pltpu.VMEM((2, n_sel * SLC_BLOCK, SLC_W), F32),
                        pltpu.SemaphoreType.DMA((2,))],
    )
    return pl.pallas_call(
        functools.partial(_slc_step_body, layer=layer, n_seq=bsz, n_pages=n_pages, new_blk=2 * n_pages),
        grid_spec=grid_spec,
        out_shape=jax.ShapeDtypeStruct((bsz, KVH_D, 8, LANE), F32),
        compiler_params=pltpu.CompilerParams(dimension_semantics=("arbitrary",), vmem_limit_bytes=VMEM_LIMIT),
        name="nsa_slc_step",
    )(page_table.reshape(-1), sel.reshape(-1), pool, q, k_new, v_new)


def _rms_norm(x, g, eps=RMS_EPS):
    xf = x.astype(F32)
    y = xf * lax.rsqrt(jnp.mean(xf * xf, axis=-1, keepdims=True) + eps)
    return (y * g.astype(F32)).astype(x.dtype)


def _split_cols(x, sizes):
    return jnp.split(x, [int(s) for s in np.cumsum(sizes)[:-1]], axis=-1)


def _last_rows(a, n):
    t = a.shape[1]
    if t >= n:
        return a[:, t - n:]
    pad = [(0, 0)] * a.ndim
    pad[1] = (n - t, 0)
    return jnp.pad(a, pad)


def _chunked_scan(step, init, xs, chunk):
    t = xs[0].shape[1]
    nc = t // chunk

    def to_chunks(a):
        return a.reshape(a.shape[0], nc, chunk, *a.shape[2:]).swapaxes(0, 1)
    final, ys = lax.scan(step, init, tuple(to_chunks(a) for a in xs))
    ys = ys.swapaxes(0, 1)
    return final, ys.reshape(ys.shape[0], t, *ys.shape[3:])


def _softmax_attend(q, k, v, mask, sinks=None):
    s = jnp.einsum('...qgrd,...kgd->...grqk', q, k).astype(F32) * (HEAD_DIM ** -0.5)
    s = jnp.where(mask[..., None, None, :, :], s, -jnp.inf)
    m = jnp.max(s, axis=-1, keepdims=True)
    if sinks is not None:
        sk = sinks.astype(F32)[:, :, None, None]
        m = jnp.maximum(m, sk)
    m = jnp.where(jnp.isfinite(m), m, 0.0)
    p = jnp.exp(s - m)
    den = jnp.sum(p, axis=-1, keepdims=True)
    if sinks is not None:
        den = den + jnp.exp(sk - m)
    p = p / jnp.where(den > 0, den, 1.0)
    out = jnp.einsum('...grqk,...kgd->...qgrd', p.astype(v.dtype), v)
    return out, p


def _banded_window_attention(q, k, v, window, sinks):
    bsz, s_len, g, r, d = q.shape
    nb = s_len // QBLOCK
    span = window + QBLOCK
    pad = ((0, 0), (window, 0), (0, 0), (0, 0))
    kp, vp = jnp.pad(k, pad), jnp.pad(v, pad)
    qb = q.reshape(bsz, nb, QBLOCK, g, r, d).swapaxes(0, 1)

    def block(args):
        i, qi = args
        start = i * QBLOCK
        ki = lax.dynamic_slice_in_dim(kp, start, span, axis=1)
        vi = lax.dynamic_slice_in_dim(vp, start, span, axis=1)
        q_pos = start + jnp.arange(QBLOCK)
        k_pos = start - window + jnp.arange(span)
        diff = q_pos[:, None] - k_pos[None, :]
        mask = (k_pos[None, :] >= 0) & (diff >= 0) & (diff < window)
        return _softmax_attend(qi, ki, vi, mask, sinks)[0]
    out = lax.map(block, (jnp.arange(nb), qb))
    return out.swapaxes(0, 1).reshape(bsz, s_len, g, r, d)


def _window_step_attention(q, k_new, v_new, buf, window, sinks):
    t = k_new.shape[1]
    k = jnp.concatenate([buf[:, :, 0], k_new], axis=1)
    v = jnp.concatenate([buf[:, :, 1], v_new], axis=1)
    q_pos = PAST_LEN + jnp.arange(t)
    k_pos = PAST_LEN - window + jnp.arange(window + t)
    diff = q_pos[:, None] - k_pos[None, :]
    mask = (k_pos[None, :] >= 0) & (diff >= 0) & (diff < window)
    out, _ = _softmax_attend(q, k, v, mask, sinks)
    new_buf = jnp.concatenate([buf, jnp.stack([k_new, v_new], axis=2)], axis=1)[:, t:]
    return out, new_buf


def _mlstm_chunk(state, inputs):
    c, n, m = state
    q, k, v, ig, lf = inputs
    L = q.shape[1]
    f_cum = jnp.cumsum(lf, axis=1)
    causal = jnp.tril(jnp.ones((L, L), dtype=bool))
    log_d = jnp.where(causal[None, :, :, None], f_cum[:, :, None, :] - f_cum[:, None, :, :] + ig[:, None, :, :], -jnp.inf)
    m_inter = f_cum + m[:, None, :]
    m_t = jnp.maximum(m_inter, jnp.max(log_d, axis=2))
    d_mat = jnp.exp(log_d - m_t[:, :, None, :])
    w_inter = jnp.exp(m_inter - m_t)
    qk = jnp.einsum('bthd,bshd->btsh', q, k) * d_mat
    num = jnp.einsum('btsh,bshv->bthv', qk, v) + w_inter[..., None] * jnp.einsum('bhvd,bthd->bthv', c, q)
    den = jnp.sum(qk, axis=2) + w_inter * jnp.einsum('bhd,bthd->bth', n, q)
    h = num / jnp.maximum(jnp.abs(den), jnp.exp(-m_t))[..., None]
    m_new = m_t[:, -1]
    w_end = jnp.exp(f_cum[:, -1:] - f_cum + ig - m_new[:, None, :])
    decay = jnp.exp(f_cum[:, -1] + m - m_new)
    c_new = decay[..., None, None] * c + jnp.einsum('bsh,bshv,bshd->bhvd', w_end, v, k)
    n_new = decay[..., None] * n + jnp.einsum('bsh,bshd->bhd', w_end, k)
    return (c_new, n_new, m_new), h


def _ssd_chunk(h, inputs, a_neg):
    x, dt, bm, cm = inputs
    bsz, L = x.shape[:2]
    r = H_C // G_C
    cum = jnp.cumsum(dt * a_neg, axis=1)
    causal = jnp.tril(jnp.ones((L, L), dtype=bool))
    seg = jnp.where(causal[None, :, :, None], cum[:, :, None, :] - cum[:, None, :, :], -jnp.inf)
    decay = jnp.exp(seg).reshape(bsz, L, L, G_C, r)
    dtx = (dt[..., None] * x).reshape(bsz, L, G_C, r, P_C)
    cb = jnp.einsum('btgn,bsgn->btsg', cm, bm)
    y = jnp.einsum('btsg,btsgr,bsgrp->btgrp', cb, decay, dtx)
    hg = h.reshape(bsz, G_C, r, P_C, N_C)
    y = y + jnp.einsum('btgn,bgrpn->btgrp', cm, hg) * jnp.exp(cum).reshape(bsz, L, G_C, r)[..., None]
    w_end = jnp.exp(cum[:, -1:] - cum).reshape(bsz, L, G_C, r)
    h_new = hg * jnp.exp(cum[:, -1]).reshape(bsz, G_C, r)[..., None, None] + jnp.einsum('bsgr,bsgrp,bsgn->bgrpn', w_end, dtx, bm)
    return h_new.reshape(bsz, H_C, P_C, N_C), y.reshape(bsz, L, H_C, P_C)


def _causal_conv(xbc, buf, w, b):
    t = xbc.shape[1]
    xp = jnp.concatenate([buf, xbc], axis=1)
    out = b + sum(xp[:, j:j + t] * w[j] for j in range(CONV_W))
    return jax.nn.silu(out), xp[:, t:]


def _compress(rows, pos, w1, w2):
    bsz, t, g, d = rows.shape
    nb = t // CMP_BLOCK
    blk = rows.reshape(bsz, nb, CMP_BLOCK, g, d) + pos[:, None, :]
    blk = blk.transpose(0, 1, 3, 2, 4).reshape(bsz, nb, g, CMP_BLOCK * d)
    return jax.nn.silu(blk @ w1) @ w2


def _select_blocks(p_cmp, q_pos, n_top):
    bsz, g, r, t, nbc = p_cmp.shape
    ratio = SLC_BLOCK // CMP_BLOCK
    imp = p_cmp.sum(axis=2).reshape(bsz, g, t, nbc // ratio, ratio).sum(axis=-1)
    blk = jnp.arange(nbc // ratio)[None, :]
    cur = (q_pos // SLC_BLOCK)[:, None]
    forced = (blk == 0) | (blk == cur) | (blk == cur - 1)
    score = jnp.where(blk > cur, -jnp.inf, jnp.where(forced, jnp.inf, imp))
    _, sel = lax.top_k(score, n_top)
    return sel.astype(jnp.int32)


def _pad_cols(w, n):
    return jnp.pad(w, ((0, 0), (0, n - w.shape[1])))


def _even_layer(x, w, st, prompt):
    bsz, t, _ = x.shape
    r_a = H_A // KVH_A
    n_in = sum(IN_E_SIZES)
    proj = _norm_matmul(x.reshape(bsz * t, D_MODEL), w['norm_mix'], w['w_in_p'])[:, :n_in].reshape(bsz, t, n_in)
    qa, ka, va, qb, kb, vb, ib, fb, ob = _split_cols(proj, IN_E_SIZES)
    qa = _rms_norm(qa.reshape(bsz, t, KVH_A, r_a, HEAD_DIM), w['q_norm'])
    ka = _rms_norm(ka.reshape(bsz, t, KVH_A, HEAD_DIM), w['k_norm'])
    va = va.reshape(bsz, t, KVH_A, HEAD_DIM)
    sinks = w['sinks'].reshape(KVH_A, r_a)
    if prompt:
        o_a = _banded_window_attention(qa, ka, va, WINDOW_A, sinks)
        new_win = _last_rows(jnp.stack([ka, va], axis=2), WINDOW_A)
    else:
        o_a, new_win = _window_step_attention(qa, ka, va, st['a_win'], WINDOW_A, sinks)
    qb = qb.reshape(bsz, t, H_B, DK_B)
    kb = kb.reshape(bsz, t, H_B, DK_B) * (DK_B ** -0.5)
    vb = vb.reshape(bsz, t, H_B, DV_B)
    ig = ib + w['b_igate']
    lf = jax.nn.log_sigmoid(fb + w['b_fgate'])
    if prompt:
        init = (jnp.zeros((bsz, H_B, DV_B, DK_B), F32), jnp.zeros((bsz, H_B, DK_B), F32), jnp.zeros((bsz, H_B), F32))
        (c_new, n_new, m_new), hb = _chunked_scan(_mlstm_chunk, init, (qb, kb, vb, ig, lf), CHUNK_B)
    else:
        init = (st['b_c'], st['b_n'], st['b_m'])
        (c_new, n_new, m_new), hb = _mlstm_chunk(init, (qb, kb, vb, ig, lf))
    hb = _rms_norm(hb, w['h_norm']) * jax.nn.sigmoid(ob.reshape(bsz, t, H_B, DV_B))
    mix = jnp.concatenate([o_a.reshape(bsz, t, H_A * HEAD_DIM), hb.reshape(bsz, t, H_B * DV_B)], axis=-1)
    xr = _out_proj(mix.reshape(bsz * t, D_MODEL), w['w_out'], x.reshape(bsz * t, D_MODEL))
    xr = _ffn(xr, w['norm_ffn'], w['w_gu'], w['w_down'])
    return xr.reshape(bsz, t, D_MODEL), (new_win, c_new, n_new, m_new)


def _odd_mixers(x, w, st, page_table, prompt):
    bsz, t, _ = x.shape
    r_d = H_D // KVH_D
    n_in = sum(IN_O_SIZES)
    proj = _norm_matmul(x.reshape(bsz * t, D_MODEL), w['norm_mix'], w['w_in_p'])[:, :n_in].reshape(bsz, t, n_in)
    zc, xbc, dtc, qd, kvd, gd = _split_cols(proj, IN_O_SIZES)
    buf = jnp.zeros((bsz, CONV_W - 1, CONV_DIM_C), x.dtype) if prompt else st['c_conv']
    xbc, new_conv = _causal_conv(xbc, buf, w['conv_w'], w['conv_b'])
    xc, bc, cc = _split_cols(xbc, (D_INNER_C, G_C * N_C, G_C * N_C))
    xc = xc.reshape(bsz, t, H_C, P_C)
    bc = bc.reshape(bsz, t, G_C, N_C)
    cc = cc.reshape(bsz, t, G_C, N_C)
    dt = jax.nn.softplus(dtc + w['dt_bias'])
    step = functools.partial(_ssd_chunk, a_neg=-jnp.exp(w['a_log']))
    if prompt:
        h_new, yc = _chunked_scan(step, jnp.zeros((bsz, H_C, P_C, N_C), F32), (xc, dt, bc, cc), CHUNK_C)
    else:
        h_new, yc = step(st['c_ssm'], (xc, dt, bc, cc))
    yc = yc + w['d_skip'][:, None] * xc
    yc = yc.reshape(bsz, t, D_INNER_C) * jax.nn.silu(zc)
    yc = _rms_norm(yc.reshape(bsz, t, G_C, D_INNER_C // G_C), w['y_norm'].reshape(G_C, D_INNER_C // G_C))
    qd = _rms_norm(qd.reshape(bsz, t, KVH_D, r_d, HEAD_DIM), w['q_norm'])
    kc, vc, ks, vs, kw, vw = [a.reshape(bsz, t, KVH_D, HEAD_DIM) for a in _split_cols(kvd, (KVH_D * HEAD_DIM,) * 6)]
    ks = _rms_norm(ks, w['k_norm'])
    kw = _rms_norm(kw, w['k_norm'])
    new_kv = jnp.stack([kc, vc, ks, vs], axis=2)
    q_pos = jnp.arange(t) + (0 if prompt else PAST_LEN)
    ks_t, vs_t = ks.transpose(0, 2, 1, 3), vs.transpose(0, 2, 1, 3)

    def comp(rows, j):
        return _compress(rows, w['cmp_pos'][j], w['cmp_w1'][j], w['cmp_w2'][j])
    if prompt:
        kcmp, vcmp = comp(kc, 0), comp(vc, 1)
    else:
        assert t == 1
        pool = st['d_kv'].reshape(st['d_kv'].shape[0], st['d_kv'].shape[1], PAGE_SIZE, CMP_W + SLC_W)
        n_pages = page_table.shape[1]
        past = _compress_pages(pool, page_table, w['cmp_pos'], w['cmp_w1'], w['cmp_w2'], st['layer'])
        past = past.reshape(bsz, n_pages * BLK_PER_PAGE, 2, KVH_D, HEAD_DIM)
        pad_t = -(-t // SLC_BLOCK) * SLC_BLOCK
        padw = ((0, 0), (0, pad_t - t), (0, 0), (0, 0))
        kcmp = jnp.concatenate([past[:, :, 0], comp(jnp.pad(kc, padw), 0)], axis=1)
        vcmp = jnp.concatenate([past[:, :, 1], comp(jnp.pad(vc, padw), 1)], axis=1)
    kcmp = _rms_norm(kcmp, w['k_norm'])
    n_top = min(TOP_N, kcmp.shape[1] // (SLC_BLOCK // CMP_BLOCK))
    if prompt:
        qh = qd.transpose(0, 2, 3, 1, 4)
        o_cmp, sel = _cmp_select(qh, kcmp.transpose(0, 2, 1, 3), vcmp.transpose(0, 2, 1, 3), n_top)
        o_slc = _slc_attend(qh, sel, ks_t, vs_t).transpose(0, 3, 1, 2, 4)
        o_cmp = o_cmp.transpose(0, 3, 1, 2, 4)
        o_win = _banded_window_attention(qd, kw, vw, WINDOW_D, None)
        new_win = _last_rows(jnp.stack([kw, vw], axis=2), WINDOW_D)
    else:
        blk_end = jnp.arange(kcmp.shape[1]) * CMP_BLOCK + (CMP_BLOCK - 1)
        o_cmp, p_cmp = _softmax_attend(qd, kcmp, vcmp, blk_end[None, :] <= q_pos[:, None])
        sel = _select_blocks(p_cmp, q_pos, n_top)[:, :, 0]
        q_rows = jnp.pad(qd[:, 0], ((0, 0), (0, 0), (0, 8 - r_d), (0, 0)))
        q_rows = jnp.concatenate([q_rows, q_rows], axis=-1)
        o8 = _slc_step(pool, page_table, sel, q_rows, ks.reshape(bsz, 1, KVH_D * HEAD_DIM),
                       vs.reshape(bsz, 1, KVH_D * HEAD_DIM), st['layer'])
        o_slc = jnp.stack([o8[:, g_, :r_d, g_ * HEAD_DIM:(g_ + 1) * HEAD_DIM] for g_ in range(KVH_D)], axis=1)[:, None]
        o_win, new_win = _window_step_attention(qd, kw, vw, st['d_win'], WINDOW_D, None)
    g = jax.nn.sigmoid(gd).reshape(bsz, t, KVH_D, r_d, 3)
    o_d = g[..., 0:1] * o_cmp + g[..., 1:2] * o_slc + g[..., 2:3] * o_win
    mix = jnp.concatenate([yc.reshape(bsz, t, D_INNER_C), o_d.reshape(bsz, t, H_D * HEAD_DIM)], axis=-1)
    return mix.reshape(bsz * t, D_MODEL), (h_new, new_conv, new_kv, new_win)


def kernel(x_prompt, x_sample, cache_a_win, state_b_c, state_b_n, state_b_m, state_c_ssm, state_c_conv, cache_d_kv, cache_d_win, page_table, e_norm_mix, e_w_in, e_q_norm, e_k_norm, e_sinks, e_b_igate, e_b_fgate, e_h_norm, e_w_out, e_norm_ffn, e_w_gu, e_w_down, o_norm_mix, o_w_in, o_conv_w, o_conv_b, o_dt_bias, o_a_log, o_d_skip, o_y_norm, o_q_norm, o_k_norm, o_cmp_pos, o_cmp_w1, o_cmp_w2, o_w_out, o_norm_ffn, o_router, o_w_gu, o_w_down):
    i = 0
    we = {'norm_mix': e_norm_mix[i], 'w_in_p': _pad_cols(e_w_in[i], _round_up(sum(IN_E_SIZES), LANE)).astype(BF16),
          'q_norm': e_q_norm[i], 'k_norm': e_k_norm[i],
          'sinks': e_sinks[i], 'b_igate': e_b_igate[i], 'b_fgate': e_b_fgate[i], 'h_norm': e_h_norm[i],
          'w_out': e_w_out[i].astype(BF16), 'norm_ffn': e_norm_ffn[i], 'w_gu': e_w_gu[i].astype(BF16),
          'w_down': e_w_down[i].astype(BF16)}
    wo = {'norm_mix': o_norm_mix[i], 'w_in_p': _pad_cols(o_w_in[i], _round_up(sum(IN_O_SIZES), LANE)).astype(BF16),
          'conv_w': o_conv_w[i], 'conv_b': o_conv_b[i],
          'dt_bias': o_dt_bias[i], 'a_log': o_a_log[i], 'd_skip': o_d_skip[i], 'y_norm': o_y_norm[i],
          'q_norm': o_q_norm[i], 'k_norm': o_k_norm[i], 'cmp_pos': o_cmp_pos[i], 'cmp_w1': o_cmp_w1[i],
          'cmp_w2': o_cmp_w2[i], 'w_out': o_w_out[i].astype(BF16), 'norm_ffn': o_norm_ffn[i],
          'router': _pad_cols(o_router[i], LANE), 'w_gu': o_w_gu[i].astype(BF16), 'w_down': o_w_down[i].astype(BF16)}
    st_e = {'a_win': cache_a_win[i], 'b_c': state_b_c[i], 'b_n': state_b_n[i], 'b_m': state_b_m[i]}
    st_o = {'c_ssm': state_c_ssm[i], 'c_conv': state_c_conv[i], 'd_kv': cache_d_kv, 'layer': i,
            'd_win': cache_d_win[i]}

    xp, sp_e = _even_layer(x_prompt, we, None, True)
    xs, ss_e = _even_layer(x_sample, we, st_e, False)

    mix_p, sp_o = _odd_mixers(xp, wo, None, page_table, True)
    mix_s, ss_o = _odd_mixers(xs, wo, st_o, page_table, False)
    np_rows = mix_p.shape[0]
    xp2, xnp, lgp = _out_proj_route(mix_p, wo['w_out'], xp.reshape(np_rows, D_MODEL), wo['norm_ffn'], wo['router'])
    xs2, xns, lgs = _out_proj_route(mix_s, wo['w_out'], xs.reshape(-1, D_MODEL), wo['norm_ffn'], wo['router'])
    x_all = jnp.concatenate([xp2, xs2], axis=0)
    xn_all = jnp.concatenate([xnp, xns], axis=0)
    lg_all = jnp.concatenate([lgp, lgs], axis=0)[:, :N_EXP]
    y_all = _moe_layer(x_all, xn_all, lg_all, wo['w_gu'], wo['w_down'])
    y_prompt = y_all[:np_rows].reshape(x_prompt.shape)
    y_sample = y_all[np_rows:].reshape(x_sample.shape)

    def one(a):
        return a[None]
    return (y_prompt, y_sample, one(sp_e[0]), one(ss_e[0]), one(sp_e[1]), one(ss_e[1]), one(sp_e[2]), one(ss_e[2]),
            one(sp_e[3]), one(ss_e[3]), one(sp_o[0]), one(ss_o[0]), one(sp_o[1]), one(ss_o[1]),
            one(sp_o[2]), one(ss_o[2]), one(sp_o[3]), one(ss_o[3]))
```

```python
import functools
import math

import numpy as np
import jax
import jax.numpy as jnp
from jax import lax
from jax.experimental import pallas as pl
from jax.experimental.pallas import tpu as pltpu

F32 = jnp.float32
BF16 = jnp.bfloat16

D_MODEL = 1024
PAST_LEN = 16384
PAGE_SIZE = 128
HEAD_DIM = 64
QBLOCK = 128
H_A, KVH_A, WINDOW_A = 8, 2, 128
H_B, DK_B, DV_B, CHUNK_B = 4, 64, 128, 64
P_C, H_C, G_C, N_C, CONV_W, CHUNK_C = 64, 8, 2, 128, 4, 128
D_INNER_C = H_C * P_C
CONV_DIM_C = D_INNER_C + 2 * G_C * N_C
H_D, KVH_D = 8, 2
CMP_BLOCK, SLC_BLOCK, TOP_N, WINDOW_D, SLC_QBLOCK = 32, 64, 16, 512, 32
D_FF, N_EXP, TOP_K = 3584, 8, 2
IN_E_SIZES = (H_A * HEAD_DIM, KVH_A * HEAD_DIM, KVH_A * HEAD_DIM, H_B * DK_B, H_B * DK_B, H_B * DV_B, H_B, H_B, H_B * DV_B)
IN_O_SIZES = (D_INNER_C, CONV_DIM_C, H_C, H_D * HEAD_DIM, 6 * KVH_D * HEAD_DIM, 3 * H_D)

LANE = 128
VMEM_LIMIT = 48 * 1024 * 1024
RMS_EPS = 1e-6
FF_TILE = 896
ROW_TILE = 512


def _round_up(n, m):
    return -(-n // m) * m


def _rms(x, g):
    return x * lax.rsqrt(jnp.mean(x * x, axis=-1, keepdims=True) + RMS_EPS) * g


def _norm_matmul_body(x_ref, g_ref, w_ref, o_ref):
    xn = _rms(x_ref[...], g_ref[...])
    o_ref[...] = jnp.dot(xn.astype(BF16), w_ref[...], preferred_element_type=F32)


def _norm_matmul(x, gain, w):
    m, k = x.shape
    n = w.shape[1]
    tm = min(ROW_TILE, m)
    return pl.pallas_call(
        _norm_matmul_body,
        grid=(m // tm,),
        in_specs=[pl.BlockSpec((tm, k), lambda i: (i, 0)),
                  pl.BlockSpec((1, k), lambda i: (0, 0)),
                  pl.BlockSpec((k, n), lambda i: (0, 0))],
        out_specs=pl.BlockSpec((tm, n), lambda i: (i, 0)),
        out_shape=jax.ShapeDtypeStruct((m, n), F32),
        compiler_params=pltpu.CompilerParams(dimension_semantics=("arbitrary",), vmem_limit_bytes=VMEM_LIMIT),
        name="norm_in_proj",
    )(x, gain.reshape(1, k), w)


def _out_proj_body(mix_ref, w_ref, res_ref, o_ref):
    o_ref[...] = res_ref[...] + jnp.dot(mix_ref[...].astype(BF16), w_ref[...], preferred_element_type=F32)


def _out_proj(mix, w, res):
    m, k = mix.shape
    n = w.shape[1]
    tm = min(ROW_TILE, m)
    return pl.pallas_call(
        _out_proj_body,
        grid=(m // tm,),
        in_specs=[pl.BlockSpec((tm, k), lambda i: (i, 0)),
                  pl.BlockSpec((k, n), lambda i: (0, 0)),
                  pl.BlockSpec((tm, n), lambda i: (i, 0))],
        out_specs=pl.BlockSpec((tm, n), lambda i: (i, 0)),
        out_shape=jax.ShapeDtypeStruct((m, n), F32),
        compiler_params=pltpu.CompilerParams(dimension_semantics=("arbitrary",), vmem_limit_bytes=VMEM_LIMIT),
        name="out_proj",
    )(mix, w, res)


def _out_proj_route_body(mix_ref, w_ref, res_ref, g_ref, r_ref, o_ref, xn_ref, lg_ref):
    x = res_ref[...] + jnp.dot(mix_ref[...].astype(BF16), w_ref[...], preferred_element_type=F32)
    o_ref[...] = x
    xn = _rms(x, g_ref[...])
    xn_ref[...] = xn.astype(BF16)
    lg_ref[...] = jnp.dot(xn, r_ref[...], preferred_element_type=F32, precision=lax.Precision.HIGHEST)


def _out_proj_route(mix, w, res, gain, router_pad):
    m, k = mix.shape
    n = w.shape[1]
    tm = min(ROW_TILE, m)
    return pl.pallas_call(
        _out_proj_route_body,
        grid=(m // tm,),
        in_specs=[pl.BlockSpec((tm, k), lambda i: (i, 0)),
                  pl.BlockSpec((k, n), lambda i: (0, 0)),
                  pl.BlockSpec((tm, n), lambda i: (i, 0)),
                  pl.BlockSpec((1, n), lambda i: (0, 0)),
                  pl.BlockSpec((n, LANE), lambda i: (0, 0))],
        out_specs=[pl.BlockSpec((tm, n), lambda i: (i, 0)),
                   pl.BlockSpec((tm, n), lambda i: (i, 0)),
                   pl.BlockSpec((tm, LANE), lambda i: (i, 0))],
        out_shape=[jax.ShapeDtypeStruct((m, n), F32),
                   jax.ShapeDtypeStruct((m, n), BF16),
                   jax.ShapeDtypeStruct((m, LANE), F32)],
        compiler_params=pltpu.CompilerParams(dimension_semantics=("arbitrary",), vmem_limit_bytes=VMEM_LIMIT),
        name="out_proj_route",
    )(mix, w, res, gain.reshape(1, n), router_pad)


def _ffn_body(x_ref, g_ref, wg_ref, wu_ref, wd_ref, o_ref, xn_ref, acc_ref):
    c = pl.program_id(1)

    @pl.when(c == 0)
    def _():
        xn_ref[...] = _rms(x_ref[...], g_ref[...]).astype(BF16)
        acc_ref[...] = jnp.zeros_like(acc_ref)

    xn = xn_ref[...]
    gate = jnp.dot(xn, wg_ref[...], preferred_element_type=F32)
    up = jnp.dot(xn, wu_ref[...], preferred_element_type=F32)
    h = (gate * jax.nn.sigmoid(gate) * up).astype(BF16)
    acc_ref[...] += jnp.dot(h, wd_ref[...], preferred_element_type=F32)

    @pl.when(c == pl.num_programs(1) - 1)
    def _():
        o_ref[...] = x_ref[...] + acc_ref[...]


def _ffn(x, gain, w_gu, w_down):
    m, d = x.shape
    tm = min(ROW_TILE, m)
    nc = D_FF // FF_TILE
    return pl.pallas_call(
        _ffn_body,
        grid=(m // tm, nc),
        in_specs=[pl.BlockSpec((tm, d), lambda i, c: (i, 0)),
                  pl.BlockSpec((1, d), lambda i, c: (0, 0)),
                  pl.BlockSpec((d, FF_TILE), lambda i, c: (0, c)),
                  pl.BlockSpec((d, FF_TILE), lambda i, c: (0, c + nc)),
                  pl.BlockSpec((FF_TILE, d), lambda i, c: (c, 0))],
        out_specs=pl.BlockSpec((tm, d), lambda i, c: (i, 0)),
        out_shape=jax.ShapeDtypeStruct((m, d), F32),
        scratch_shapes=[pltpu.VMEM((tm, d), BF16), pltpu.VMEM((tm, d), F32)],
        compiler_params=pltpu.CompilerParams(dimension_semantics=("arbitrary", "arbitrary"), vmem_limit_bytes=VMEM_LIMIT),
        name="ffn_dense",
    )(x, gain.reshape(1, d), w_gu, w_gu, w_down)


def _moe_body(te_ref, tv_ref, x_ref, s_ref, wg_ref, wu_ref, wd_ref, o_ref, acc_ref):
    i = pl.program_id(0)
    c = pl.program_id(1)
    valid = tv_ref[i] > 0

    @pl.when(c == 0)
    def _():
        acc_ref[...] = jnp.zeros_like(acc_ref)

    @pl.when(valid)
    def _():
        x = x_ref[...]
        gate = jnp.dot(x, wg_ref[0], preferred_element_type=F32)
        up = jnp.dot(x, wu_ref[0], preferred_element_type=F32)
        h = (gate * jax.nn.sigmoid(gate) * up).astype(BF16)
        acc_ref[...] += jnp.dot(h, wd_ref[0], preferred_element_type=F32)

    @pl.when(c == pl.num_programs(1) - 1)
    def _():
        o_ref[...] = s_ref[...] * acc_ref[...]


def _moe_grouped(xs, scale, tile_expert, tile_valid, w_gu, w_down, tm):
    p, d = xs.shape
    nc = D_FF // FF_TILE
    grid_spec = pltpu.PrefetchScalarGridSpec(
        num_scalar_prefetch=2,
        grid=(p // tm, nc),
        in_specs=[pl.BlockSpec((tm, d), lambda i, c, te, tv: (i, 0)),
                  pl.BlockSpec((tm, 1), lambda i, c, te, tv: (i, 0)),
                  pl.BlockSpec((1, d, FF_TILE), lambda i, c, te, tv: (te[i], 0, c)),
                  pl.BlockSpec((1, d, FF_TILE), lambda i, c, te, tv: (te[i], 0, c + nc)),
                  pl.BlockSpec((1, FF_TILE, d), lambda i, c, te, tv: (te[i], c, 0))],
        out_specs=pl.BlockSpec((tm, d), lambda i, c, te, tv: (i, 0)),
        scratch_shapes=[pltpu.VMEM((tm, d), F32)],
    )
    return pl.pallas_call(
        _moe_body,
        grid_spec=grid_spec,
        out_shape=jax.ShapeDtypeStruct((p, d), F32),
        compiler_params=pltpu.CompilerParams(dimension_semantics=("arbitrary", "arbitrary"), vmem_limit_bytes=VMEM_LIMIT),
        name="moe_grouped",
    )(tile_expert, tile_valid, xs, scale, w_gu, w_gu, w_down)


def _moe_layer(x_rows, xn_rows, logits, w_gu, w_down, tm=ROW_TILE):
    n = x_rows.shape[0]
    top_v, top_i = lax.top_k(logits, TOP_K)
    gate = jax.nn.softmax(top_v, axis=-1)
    flat_e = top_i.reshape(-1).astype(jnp.int32)
    na = flat_e.shape[0]
    order = jnp.argsort(flat_e, stable=True).astype(jnp.int32)
    counts = jnp.sum(jax.nn.one_hot(flat_e, N_EXP, dtype=jnp.int32), axis=0)
    tiles_per_e = (counts + tm - 1) // tm
    tile_end = jnp.cumsum(tiles_per_e)
    tile_start = tile_end - tiles_per_e
    group_start = jnp.cumsum(counts) - counts
    se = flat_e[order]
    pos = tile_start[se] * tm + (jnp.arange(na, dtype=jnp.int32) - group_start[se])
    n_tiles = na // tm + N_EXP
    p = n_tiles * tm
    tok_of_pos = jnp.zeros((p,), jnp.int32).at[pos].set(order // TOP_K)
    gate_of_pos = jnp.zeros((p,), F32).at[pos].set(gate.reshape(-1)[order])
    pos_of_a = jnp.zeros((na,), jnp.int32).at[order].set(pos)
    tile_ids = jnp.arange(n_tiles, dtype=jnp.int32)
    total_tiles = tile_end[-1]
    last_e = jnp.searchsorted(tile_end, total_tiles - 1, side="right").astype(jnp.int32)
    tile_expert = jnp.where(tile_ids < total_tiles,
                            jnp.searchsorted(tile_end, tile_ids, side="right").astype(jnp.int32), last_e)
    tile_expert = jnp.clip(tile_expert, 0, N_EXP - 1)
    tile_valid = (tile_ids < total_tiles).astype(jnp.int32)
    xs = xn_rows[tok_of_pos]
    ys = _moe_grouped(xs, gate_of_pos.reshape(p, 1), tile_expert, tile_valid, w_gu, w_down, tm)
    pa = pos_of_a.reshape(n, TOP_K)
    return x_rows + (ys[pa[:, 0]] + ys[pa[:, 1]])


QT = 128
KT = 512
NEG = -1e30


def _cmp_select_body(q_ref, kc_ref, vc_ref, o_ref, sel_ref, *, n_top):
    qi = pl.program_id(2)
    r = q_ref.shape[2]
    q = q_ref[0, 0].reshape(r * QT, HEAD_DIM)
    s = lax.dot_general(q.astype(BF16), kc_ref[0, 0].astype(BF16), (((1,), (1,)), ((), ())),
                        preferred_element_type=F32) * (HEAD_DIM ** -0.5)
    row = lax.broadcasted_iota(jnp.int32, (r * QT, LANE), 0)
    lane = lax.broadcasted_iota(jnp.int32, (r * QT, LANE), 1)
    t_pos = qi * QT + (row & (QT - 1))
    s = jnp.where(lane * CMP_BLOCK + (CMP_BLOCK - 1) <= t_pos, s, -jnp.inf)
    m = jnp.max(s, axis=-1, keepdims=True)
    m = jnp.where(m > -jnp.inf, m, 0.0)
    p = jnp.exp(s - m)
    den = jnp.sum(p, axis=-1, keepdims=True)
    p = p / jnp.where(den > 0, den, 1.0)
    o = jnp.dot(p.astype(BF16), vc_ref[0, 0].astype(BF16), preferred_element_type=F32)
    o_ref[0, 0] = o.reshape(r, QT, HEAD_DIM)
    pt = p[0:QT]
    for h in range(1, r):
        pt = pt + p[h * QT:(h + 1) * QT]
    lane1 = lax.broadcasted_iota(jnp.int32, (QT, LANE), 1)
    row1 = lax.broadcasted_iota(jnp.int32, (QT, LANE), 0)
    even = (lane1 & 1) == 0
    lo = jnp.where(even, pt, pltpu.roll(pt, 1, 1))
    hi = jnp.where(even, pltpu.roll(pt, LANE - 1, 1), pt)
    imp = lo + hi
    blk = lane1 >> 1
    cur = (qi * QT + row1) >> 6
    forced = (blk == 0) | (blk == cur) | (blk == cur - 1)
    score = jnp.where(blk > cur, -jnp.inf, jnp.where(forced, jnp.inf, imp))
    gt_cnt = jnp.zeros((QT, LANE), F32)
    eq_cnt = jnp.zeros((QT, LANE), F32)
    lane_f = lane1.astype(F32)
    for j in range(LANE // 2):
        col = jnp.broadcast_to(score[:, 2 * j:2 * j + 1], (QT, LANE))
        gt_cnt = gt_cnt + jnp.where(col > score, 1.0, 0.0)
        later = jnp.where(lane_f >= 2.0 * j + 2.0, 1.0, 0.0)
        eq_cnt = eq_cnt + jnp.where(col == score, later, 0.0)
    rank = gt_cnt + eq_cnt
    sel_ref[0, 0] = jnp.where((rank < n_top) & (blk <= cur), 1.0, 0.0)


def _cmp_select(q, kcmp, vcmp, n_top):
    b, g, r, t, d = q.shape
    assert kcmp.shape == (b, g, LANE, d) and t // CMP_BLOCK == LANE
    return pl.pallas_call(
        functools.partial(_cmp_select_body, n_top=n_top),
        grid=(b, g, t // QT),
        in_specs=[pl.BlockSpec((1, 1, r, QT, d), lambda i, j, k: (i, j, 0, k, 0)),
                  pl.BlockSpec((1, 1, LANE, d), lambda i, j, k: (i, j, 0, 0)),
                  pl.BlockSpec((1, 1, LANE, d), lambda i, j, k: (i, j, 0, 0))],
        out_specs=[pl.BlockSpec((1, 1, r, QT, d), lambda i, j, k: (i, j, 0, k, 0)),
                   pl.BlockSpec((1, 1, QT, LANE), lambda i, j, k: (i, j, k, 0))],
        out_shape=[jax.ShapeDtypeStruct((b, g, r, t, d), F32),
                   jax.ShapeDtypeStruct((b, g, t, LANE), F32)],
        compiler_params=pltpu.CompilerParams(dimension_semantics=("arbitrary",) * 3, vmem_limit_bytes=VMEM_LIMIT),
        name="nsa_cmp_select",
    )(q, kcmp, vcmp)


def _slc_attend_body(q_ref, sel_ref, k_ref, v_ref, o_ref):
    qi = pl.program_id(2)
    r = q_ref.shape[2]
    rows = r * QT
    q = (q_ref[0, 0].reshape(rows, HEAD_DIM) * (HEAD_DIM ** -0.5)).astype(BF16)
    sel = sel_ref[0, 0]
    e_row = lax.broadcasted_iota(jnp.int32, (LANE, KT), 0)
    e_col = lax.broadcasted_iota(jnp.int32, (LANE, KT), 1)
    expand = jnp.where(e_row == 2 * (e_col >> 6), 1.0, 0.0).astype(BF16)
    t_pos = qi * QT + lax.broadcasted_iota(jnp.int32, (QT, KT), 0)
    k_off = lax.broadcasted_iota(jnp.int32, (QT, KT), 1)
    n_kt = (qi * QT + QT - 1) // KT + 1
    lanes_per_kt = 2 * (KT // SLC_BLOCK)

    def body(kt, carry):
        m, l, acc = carry
        start = pl.multiple_of(kt * KT, KT)
        k = k_ref[0, 0, pl.ds(start, KT), :].astype(BF16)
        v = v_ref[0, 0, pl.ds(start, KT), :].astype(BF16)
        s = lax.dot_general(q, k, (((1,), (1,)), ((), ())), preferred_element_type=F32)
        sel_kt = pltpu.roll(sel, (LANE - kt * lanes_per_kt) % LANE, 1).astype(BF16)
        picked = jnp.dot(sel_kt, expand, preferred_element_type=F32)
        bias = jnp.where((picked > 0.5) & (start + k_off <= t_pos), 0.0, NEG)
        s = s + jnp.concatenate([bias] * r, axis=0)
        m_new = jnp.maximum(m, jnp.max(s, axis=-1, keepdims=True))
        alpha = jnp.exp(m - m_new)
        p = jnp.exp(s - m_new)
        l = alpha * l + jnp.sum(p, axis=-1, keepdims=True)
        acc = alpha * acc + jnp.dot(p.astype(BF16), v, preferred_element_type=F32)
        return m_new, l, acc

    init = (jnp.full((rows, 1), NEG, F32), jnp.zeros((rows, 1), F32), jnp.zeros((rows, HEAD_DIM), F32))
    m, l, acc = lax.fori_loop(0, n_kt, body, init)
    o_ref[0, 0] = (acc / l).reshape(r, QT, HEAD_DIM)


def _slc_attend(q, sel, ks, vs):
    b, g, r, t, d = q.shape
    return pl.pallas_call(
        _slc_attend_body,
        grid=(b, g, t // QT),
        in_specs=[pl.BlockSpec((1, 1, r, QT, d), lambda i, j, k: (i, j, 0, k, 0)),
                  pl.BlockSpec((1, 1, QT, LANE), lambda i, j, k: (i, j, k, 0)),
                  pl.BlockSpec((1, 1, t, d), lambda i, j, k: (i, j, 0, 0)),
                  pl.BlockSpec((1, 1, t, d), lambda i, j, k: (i, j, 0, 0))],
        out_specs=pl.BlockSpec((1, 1, r, QT, d), lambda i, j, k: (i, j, 0, k, 0)),
        out_shape=jax.ShapeDtypeStruct((b, g, r, t, d), F32),
        compiler_params=pltpu.CompilerParams(dimension_semantics=("arbitrary",) * 3, vmem_limit_bytes=VMEM_LIMIT),
        name="nsa_slc_attend",
    )(q, sel, ks, vs)


CMP_HIDDEN = 128
CMP_W = 2 * KVH_D * HEAD_DIM
SLC_W = 2 * KVH_D * HEAD_DIM
BLK_PER_PAGE = PAGE_SIZE // CMP_BLOCK
PAGES_PER_STEP = 64


def _page_copy(pool_ref, buf_ref, sem_ref, layer, page, slot, j):
    return pltpu.make_async_copy(pool_ref.at[layer, page, pl.ds(0, CMP_W), :], buf_ref.at[slot, j], sem_ref.at[slot])


def _compress_pages_body(pt_ref, pool_ref, pos_ref, w1_ref, w2_ref, o_ref, buf_ref, xk_ref, xv_ref, sem_ref, *,
                         layer, n_steps):
    x_ref = (xk_ref, xv_ref)
    s = pl.program_id(0)
    slot = s % 2

    def start(step, slot_):
        for j in range(PAGES_PER_STEP):
            _page_copy(pool_ref, buf_ref, sem_ref, layer, pt_ref[step * PAGES_PER_STEP + j], slot_, j).start()

    @pl.when(s == 0)
    def _():
        start(0, 0)

    @pl.when(s + 1 < n_steps)
    def _():
        start(s + 1, 1 - slot)

    for j in range(PAGES_PER_STEP):
        _page_copy(pool_ref, buf_ref, sem_ref, layer, 0, slot, j).wait()

    for j in range(PAGES_PER_STEP):
        for br in range(2):
            x_ref[br][j * PAGE_SIZE:(j + 1) * PAGE_SIZE, :] = buf_ref[slot, j, br * LANE:(br + 1) * LANE, :].T

    m = PAGES_PER_STEP * BLK_PER_PAGE
    for br in range(2):
        acc = jnp.zeros((m, KVH_D * CMP_HIDDEN), F32)
        for r in range(CMP_BLOCK):
            x = x_ref[br][pl.ds(r, m, stride=CMP_BLOCK), :] + pos_ref[br, r:r + 1, :]
            acc = acc + jnp.dot(x.astype(BF16), w1_ref[br, r], preferred_element_type=F32)
        h = acc * jax.nn.sigmoid(acc)
        o_ref[:, br * LANE:(br + 1) * LANE] = jnp.dot(h.astype(BF16), w2_ref[br], preferred_element_type=F32)


def _block_diag2(w):
    z = jnp.zeros_like(w)
    return jnp.concatenate([jnp.concatenate([w, z], axis=-1), jnp.concatenate([z, w], axis=-1)], axis=-2)


def _compress_pages(pool, page_table, cmp_pos, cmp_w1, cmp_w2, layer):
    bsz, n_pages = page_table.shape
    total = bsz * n_pages
    n_steps = total // PAGES_PER_STEP
    assert total % PAGES_PER_STEP == 0 and KVH_D == 2
    w1 = cmp_w1.reshape(2, CMP_BLOCK, HEAD_DIM, CMP_HIDDEN)
    w1_bd = _block_diag2(w1).astype(BF16)
    w2_bd = _block_diag2(cmp_w2).astype(BF16)
    pos = jnp.concatenate([cmp_pos, cmp_pos], axis=-1)
    m = PAGES_PER_STEP * BLK_PER_PAGE
    grid_spec = pltpu.PrefetchScalarGridSpec(
        num_scalar_prefetch=1,
        grid=(n_steps,),
        in_specs=[pl.BlockSpec(memory_space=pl.ANY),
                  pl.BlockSpec((2, CMP_BLOCK, LANE), lambda s, pt: (0, 0, 0)),
                  pl.BlockSpec((2, CMP_BLOCK, LANE, KVH_D * CMP_HIDDEN), lambda s, pt: (0, 0, 0, 0)),
                  pl.BlockSpec((2, KVH_D * CMP_HIDDEN, LANE), lambda s, pt: (0, 0, 0))],
        out_specs=pl.BlockSpec((m, CMP_W), lambda s, pt: (s, 0)),
        scratch_shapes=[pltpu.VMEM((2, PAGES_PER_STEP, CMP_W, PAGE_SIZE), F32),
                        pltpu.VMEM((PAGES_PER_STEP * PAGE_SIZE, LANE), F32),
                        pltpu.VMEM((PAGES_PER_STEP * PAGE_SIZE, LANE), F32),
                        pltpu.SemaphoreType.DMA((2,))],
    )
    return pl.pallas_call(
        functools.partial(_compress_pages_body, layer=layer, n_steps=n_steps),
        grid_spec=grid_spec,
        out_shape=jax.ShapeDtypeStruct((total * BLK_PER_PAGE, CMP_W), F32),
        compiler_params=pltpu.CompilerParams(dimension_semantics=("arbitrary",), vmem_limit_bytes=VMEM_LIMIT),
        name="nsa_compress_pages",
    )(page_table.reshape(-1), pool, pos, w1_bd, w2_bd)


def _slc_copy(pool_ref, buf_ref, sem_ref, layer, page, feat0, slot, idx):
    return pltpu.make_async_copy(pool_ref.at[layer, page, pl.ds(feat0, HEAD_DIM), :], buf_ref.at[slot, idx], sem_ref.at[slot])


def _slc_step_body(pt_ref, sel_ref, pool_ref, q_ref, kn_ref, vn_ref, o_ref, kbuf_ref, vbuf_ref, sem_ref, *,
                   layer, n_seq, n_pages, new_blk):
    b = pl.program_id(0)
    slot = b % 2
    n_sel = KVH_D * TOP_N

    def copies(seq, slot_, from_table):
        out = []
        for g in range(KVH_D):
            for i in range(TOP_N):
                idx = g * TOP_N + i
                page = 0
                if from_table:
                    blk = jnp.minimum(sel_ref[seq * n_sel + idx], new_blk - 1)
                    page = pt_ref[seq * n_pages + (blk >> 1)]
                out.append(_slc_copy(pool_ref, kbuf_ref, sem_ref, layer, page, CMP_W + g * HEAD_DIM, slot_, idx))
                out.append(_slc_copy(pool_ref, vbuf_ref, sem_ref, layer, page, CMP_W + (KVH_D + g) * HEAD_DIM, slot_, idx))
        return out

    @pl.when(b == 0)
    def _():
        for c in copies(0, 0, True):
            c.start()

    @pl.when(b + 1 < n_seq)
    def _():
        for c in copies(b + 1, 1 - slot, True):
            c.start()

    for c in copies(0, slot, False):
        c.wait()

    half = lax.broadcasted_iota(jnp.int32, (8, LANE), 1) >> 6
    for g in range(KVH_D):
        q = q_ref[0, g] * (HEAD_DIM ** -0.5)
        qb = q.astype(BF16)
        s_parts = []
        for i in range(TOP_N):
            blk = sel_ref[b * n_sel + g * TOP_N + i]
            s_i = jnp.dot(qb, kbuf_ref[slot, g * TOP_N + i].astype(BF16), preferred_element_type=F32)
            ok = (half == (blk & 1)) & (blk < new_blk)
            s_parts.append(jnp.where(ok, s_i, NEG))
        s = jnp.concatenate(s_parts, axis=-1)
        s_new = jnp.sum(q * kn_ref[0, g], axis=-1, keepdims=True)
        m = jnp.maximum(jnp.max(s, axis=-1, keepdims=True), s_new)
        p = jnp.exp(s - m)
        p_new = jnp.exp(s_new - m)
        den = jnp.sum(p, axis=-1, keepdims=True) + p_new
        o = p_new * vn_ref[0, g]
        for i in range(TOP_N):
            p_i = p[:, i * LANE:(i + 1) * LANE].astype(BF16)
            o = o + lax.dot_general(p_i, vbuf_ref[slot, g * TOP_N + i].astype(BF16), (((1,), (1,)), ((), ())),
                                    preferred_element_type=F32)
        o_ref[0, g] = o / den


def _slc_step(pool, page_table, sel, q, k_new, v_new, layer):
    bsz, n_pages = page_table.shape
    n_sel = KVH_D * TOP_N
    grid_spec = pltpu.PrefetchScalarGridSpec(
        num_scalar_prefetch=2,
        grid=(bsz,),
        in_specs=[pl.BlockSpec(memory_space=pl.ANY),
                  pl.BlockSpec((1, KVH_D, 8, HEAD_DIM), lambda b, pt, sl: (b, 0, 0, 0)),
                  pl.BlockSpec((1, KVH_D, 1, HEAD_DIM), lambda b, pt, sl: (b, 0, 0, 0)),
                  pl.BlockSpec((1, KVH_D, 1, HEAD_DIM), lambda b, pt, sl: (b, 0, 0, 0))],
        out_specs=pl.BlockSpec((1, KVH_D, 8, HEAD_DIM), lambda b, pt, sl: (b, 0, 0, 0)),
        scratch_shapes=[pltpu.VMEM((2, n_sel, HEAD_DIM, PAGE_SIZE), F32),
                        pltpu.VMEM((2, n_sel, HEAD_DIM, PAGE_SIZE), F32),
                        pltpu.SemaphoreType.DMA((2,))],
    )
    return pl.pallas_call(
        functools.partial(_slc_step_body, layer=layer, n_seq=bsz, n_pages=n_pages, new_blk=2 * n_pages),
        grid_spec=grid_spec,
        out_shape=jax.ShapeDtypeStruct((bsz, KVH_D, 8, HEAD_DIM), F32),
        compiler_params=pltpu.CompilerParams(dimension_semantics=("arbitrary",), vmem_limit_bytes=VMEM_LIMIT),
        name="nsa_slc_step",
    )(page_table.reshape(-1), sel.reshape(-1), pool, q, k_new, v_new)


def _window_attend_body(sink_ref, q_ref, k_ref, v_ref, o_ref, *, window, use_sinks):
    g = pl.program_id(1)
    qi = pl.program_id(2)
    r = q_ref.shape[2]
    rows = r * QT
    span = window + QT
    q = (q_ref[0, 0].reshape(rows, HEAD_DIM) * (HEAD_DIM ** -0.5)).astype(BF16)
    start = pl.multiple_of(jnp.maximum(qi * QT - window, 0), QT)
    k = k_ref[0, 0, pl.ds(start, span), :].astype(BF16)
    v = v_ref[0, 0, pl.ds(start, span), :].astype(BF16)
    s = lax.dot_general(q, k, (((1,), (1,)), ((), ())), preferred_element_type=F32)
    row = lax.broadcasted_iota(jnp.int32, (rows, span), 0)
    diff = qi * QT + (row & (QT - 1)) - (start + lax.broadcasted_iota(jnp.int32, (rows, span), 1))
    s = jnp.where((diff >= 0) & (diff < window), s, -jnp.inf)
    m = jnp.max(s, axis=-1, keepdims=True)
    if use_sinks:
        head = lax.broadcasted_iota(jnp.int32, (rows, 1), 0) // QT
        sink = jnp.zeros((rows, 1), F32)
        for h in range(r):
            sink = jnp.where(head == h, sink_ref[g * r + h], sink)
        m = jnp.maximum(m, sink)
    p = jnp.exp(s - m)
    den = jnp.sum(p, axis=-1, keepdims=True)
    if use_sinks:
        den = den + jnp.exp(sink - m)
    o = jnp.dot(p.astype(BF16), v, preferred_element_type=F32) / den
    o_ref[0, 0] = o.reshape(r, QT, HEAD_DIM)


def _window_attend(q, k, v, window, sinks=None):
    b, g, r, t, d = q.shape
    use_sinks = sinks is not None
    if sinks is None:
        sinks = jnp.zeros((g * r,), F32)
    grid_spec = pltpu.PrefetchScalarGridSpec(
        num_scalar_prefetch=1,
        grid=(b, g, t // QT),
        in_specs=[pl.BlockSpec((1, 1, r, QT, d), lambda i, j, n, sk: (i, j, 0, n, 0)),
                  pl.BlockSpec((1, 1, t, d), lambda i, j, n, sk: (i, j, 0, 0)),
                  pl.BlockSpec((1, 1, t, d), lambda i, j, n, sk: (i, j, 0, 0))],
        out_specs=pl.BlockSpec((1, 1, r, QT, d), lambda i, j, n, sk: (i, j, 0, n, 0)),
    )
    return pl.pallas_call(
        functools.partial(_window_attend_body, window=window, use_sinks=use_sinks),
        grid_spec=grid_spec,
        out_shape=jax.ShapeDtypeStruct((b, g, r, t, d), F32),
        compiler_params=pltpu.CompilerParams(dimension_semantics=("arbitrary",) * 3, vmem_limit_bytes=VMEM_LIMIT),
        name="window_attend_%d" % window,
    )(sinks.astype(F32), q, k, v)


MLSTM_CHUNK = 128
HIGHEST = lax.Precision.HIGHEST


def _mlstm_body(q_ref, kt_ref, v_ref, g_ref, ob_ref, hn_ref, o_ref, c_out, n_out, m_out, c_ref, n_ref, m_ref):
    ci = pl.program_id(1)
    L = MLSTM_CHUNK

    @pl.when(ci == 0)
    def _():
        c_ref[...] = jnp.zeros_like(c_ref)
        n_ref[...] = jnp.zeros_like(n_ref)
        m_ref[...] = jnp.zeros_like(m_ref)

    row = lax.broadcasted_iota(jnp.int32, (L, L), 0)
    col = lax.broadcasted_iota(jnp.int32, (L, L), 1)
    causal = col <= row
    upper = jnp.where(row <= col, 1.0, 0.0)
    for h in range(H_B):
        q = q_ref[0, :, h * DK_B:(h + 1) * DK_B]
        kt = kt_ref[0, h * DK_B:(h + 1) * DK_B, :] * (DK_B ** -0.5)
        v = v_ref[0, :, h * DV_B:(h + 1) * DV_B]
        ig = g_ref[0, h:h + 1, :]
        lf = g_ref[0, H_B + h:H_B + h + 1, :]
        m_prev = m_ref[h, 0:1, 0:1]
        f_col = jnp.sum(jnp.where(causal, jnp.broadcast_to(lf, (L, L)), 0.0), axis=-1, keepdims=True)
        f_row = jnp.dot(jnp.broadcast_to(lf, (8, L)), upper, preferred_element_type=F32, precision=HIGHEST)[0:1]
        log_d = jnp.where(causal, f_col - f_row + ig, -jnp.inf)
        m_inter = f_col + m_prev
        m_t = jnp.maximum(m_inter, jnp.max(log_d, axis=-1, keepdims=True))
        d_mat = jnp.exp(log_d - m_t)
        w_inter = jnp.exp(m_inter - m_t)
        qb = q.astype(BF16)
        qk = jnp.dot(qb, kt.astype(BF16), preferred_element_type=F32) * d_mat
        num = jnp.dot(qk.astype(BF16), v.astype(BF16), preferred_element_type=F32)
        num = num + w_inter * jnp.dot(qb, c_ref[h].astype(BF16), preferred_element_type=F32)
        qn = jnp.dot(qb, n_ref[h].astype(BF16), preferred_element_type=F32)[:, 0:1]
        den = jnp.sum(qk, axis=-1, keepdims=True) + w_inter * qn
        hh = num / jnp.maximum(jnp.abs(den), jnp.exp(-m_t))
        hh = hh * lax.rsqrt(jnp.mean(hh * hh, axis=-1, keepdims=True) + RMS_EPS) * hn_ref[...]
        o_ref[0, :, h * DV_B:(h + 1) * DV_B] = hh * jax.nn.sigmoid(ob_ref[0, :, h * DV_B:(h + 1) * DV_B])
        m_new = m_t[L - 1:L, :]
        f_last = f_col[L - 1:L, :]
        w_end = jnp.exp(f_last - f_row + ig - m_new)
        decay = jnp.exp(f_last + m_prev - m_new)
        ktw = kt * w_end
        c_ref[h] = decay * c_ref[h] + jnp.dot(ktw.astype(BF16), v.astype(BF16), preferred_element_type=F32)
        n_ref[h] = decay * n_ref[h] + jnp.sum(ktw, axis=-1, keepdims=True)
        m_ref[h] = jnp.broadcast_to(m_new, m_ref.shape[1:])

    @pl.when(ci == pl.num_programs(1) - 1)
    def _():
        c_out[0] = c_ref[...]
        n_out[0] = n_ref[...]
        m_out[0] = m_ref[...]


def _mlstm_prompt(q, kt, v, gates, ob, h_norm):
    b, t, _ = q.shape
    L = MLSTM_CHUNK
    nc = t // L
    return pl.pallas_call(
        _mlstm_body,
        grid=(b, nc),
        in_specs=[pl.BlockSpec((1, L, H_B * DK_B), lambda i, c: (i, c, 0)),
                  pl.BlockSpec((1, H_B * DK_B, L), lambda i, c: (i, 0, c)),
                  pl.BlockSpec((1, L, H_B * DV_B), lambda i, c: (i, c, 0)),
                  pl.BlockSpec((1, 2 * H_B, L), lambda i, c: (i, 0, c)),
                  pl.BlockSpec((1, L, H_B * DV_B), lambda i, c: (i, c, 0)),
                  pl.BlockSpec((1, DV_B), lambda i, c: (0, 0))],
        out_specs=[pl.BlockSpec((1, L, H_B * DV_B), lambda i, c: (i, c, 0)),
                   pl.BlockSpec((1, H_B, DK_B, DV_B), lambda i, c: (i, 0, 0, 0)),
                   pl.BlockSpec((1, H_B, DK_B, LANE), lambda i, c: (i, 0, 0, 0)),
                   pl.BlockSpec((1, H_B, 8, LANE), lambda i, c: (i, 0, 0, 0))],
        out_shape=[jax.ShapeDtypeStruct((b, t, H_B * DV_B), F32),
                   jax.ShapeDtypeStruct((b, H_B, DK_B, DV_B), F32),
                   jax.ShapeDtypeStruct((b, H_B, DK_B, LANE), F32),
                   jax.ShapeDtypeStruct((b, H_B, 8, LANE), F32)],
        scratch_shapes=[pltpu.VMEM((H_B, DK_B, DV_B), F32), pltpu.VMEM((H_B, DK_B, LANE), F32),
                        pltpu.VMEM((H_B, 8, LANE), F32)],
        compiler_params=pltpu.CompilerParams(dimension_semantics=("arbitrary", "arbitrary"), vmem_limit_bytes=VMEM_LIMIT),
        name="mlstm_prompt",
    )(q, kt, v, gates, ob, h_norm.reshape(1, DV_B))


SSD_CHUNK = 128
CONV_PAD = 8


def _ssd_body(z_ref, xr_ref, bcr_ref, dtc_ref, dtr_ref, an_ref, cw_ref, cb_ref, ds_ref, yn_ref,
              o_ref, h_out, xp_ref, h_ref):
    ci = pl.program_id(1)
    L = SSD_CHUNK
    r = H_C // G_C

    @pl.when(ci == 0)
    def _():
        xp_ref[0:CONV_PAD, :] = jnp.zeros((CONV_PAD, CONV_DIM_C), F32)
        h_ref[...] = jnp.zeros_like(h_ref)

    xp_ref[CONV_PAD:CONV_PAD + L, 0:D_INNER_C] = xr_ref[0]
    xp_ref[CONV_PAD:CONV_PAD + L, D_INNER_C:CONV_DIM_C] = bcr_ref[0]
    conv = cb_ref[...]
    for j in range(CONV_W):
        off = CONV_PAD - (CONV_W - 1) + j
        conv = conv + xp_ref[off:off + L, :] * cw_ref[j:j + 1, :]
    tail = xp_ref[L:L + CONV_PAD, :]
    xp_ref[0:CONV_PAD, :] = tail
    xbc = conv * jax.nn.sigmoid(conv)
    x = xbc[:, 0:D_INNER_C]
    bm = xbc[:, D_INNER_C:D_INNER_C + G_C * N_C]
    cm = xbc[:, D_INNER_C + G_C * N_C:CONV_DIM_C]

    row = lax.broadcasted_iota(jnp.int32, (L, L), 0)
    col = lax.broadcasted_iota(jnp.int32, (L, L), 1)
    causal = col <= row
    lower = jnp.where(causal, 1.0, 0.0)
    upper = jnp.where(row <= col, 1.0, 0.0)
    dt_c = dtc_ref[0]
    dt_r = dtr_ref[0]
    a_c = dt_c * an_ref[0:1, 0:H_C]
    a_r = dt_r * an_ref[:, H_C:H_C + 1]
    cum_c = jnp.dot(lower, a_c, preferred_element_type=F32, precision=HIGHEST)
    cum_r = jnp.dot(a_r, upper, preferred_element_type=F32, precision=HIGHEST)
    ys = []
    for g in range(G_C):
        bg = bm[:, g * N_C:(g + 1) * N_C].astype(BF16)
        cg = cm[:, g * N_C:(g + 1) * N_C].astype(BF16)
        cbm = lax.dot_general(cg, bg, (((1,), (1,)), ((), ())), preferred_element_type=F32)
        dtxw = []
        for hh in range(r):
            h = g * r + hh
            cc = cum_c[:, h:h + 1]
            cr = cum_r[h:h + 1, :]
            decay = jnp.exp(jnp.where(causal, cc - cr, -jnp.inf))
            dtx = dt_c[:, h:h + 1] * x[:, h * P_C:(h + 1) * P_C]
            y = jnp.dot((cbm * decay).astype(BF16), dtx.astype(BF16), preferred_element_type=F32)
            hs = h_ref[h].astype(BF16)
            y = y + lax.dot_general(cg, hs, (((1,), (1,)), ((), ())), preferred_element_type=F32) * jnp.exp(cc)
            ys.append(y + ds_ref[0:1, h:h + 1] * x[:, h * P_C:(h + 1) * P_C])
            cl = cum_c[L - 1:L, h:h + 1]
            dtxw.append(dtx * jnp.exp(cl - cc))
        dtxw = jnp.concatenate(dtxw, axis=-1)
        upd = jnp.dot(dtxw.T.astype(BF16), bg, preferred_element_type=F32)
        for hh in range(r):
            h = g * r + hh
            cl = cum_c[L - 1:L, h:h + 1]
            h_ref[h] = h_ref[h] * jnp.exp(cl) + upd[hh * P_C:(hh + 1) * P_C, :]
    y = jnp.concatenate(ys, axis=-1)
    z = z_ref[0]
    y = y * (z * jax.nn.sigmoid(z))
    gw = D_INNER_C // G_C
    outs = []
    for g in range(G_C):
        yg = y[:, g * gw:(g + 1) * gw]
        outs.append(yg * lax.rsqrt(jnp.mean(yg * yg, axis=-1, keepdims=True) + RMS_EPS))
    o_ref[0] = jnp.concatenate(outs, axis=-1) * yn_ref[...]

    @pl.when(ci == pl.num_programs(1) - 1)
    def _():
        h_out[0] = h_ref[...]


def _ssd_prompt(proj, dt, a_neg, conv_w, conv_b, d_skip, y_norm):
    b, t, _ = proj.shape
    L = SSD_CHUNK
    nc = t // L
    an = jnp.concatenate([jnp.broadcast_to(a_neg[None, :], (H_C, H_C)), a_neg[:, None]], axis=1)
    an = jnp.pad(an, ((0, 0), (0, LANE - an.shape[1])))
    ds = jnp.pad(d_skip[None, :], ((0, 0), (0, LANE - H_C)))
    return pl.pallas_call(
        _ssd_body,
        grid=(b, nc),
        in_specs=[pl.BlockSpec((1, L, D_INNER_C), lambda i, c: (i, c, 0)),
                  pl.BlockSpec((1, L, D_INNER_C), lambda i, c: (i, c, 1)),
                  pl.BlockSpec((1, L, 2 * G_C * N_C), lambda i, c: (i, c, 2)),
                  pl.BlockSpec((1, L, H_C), lambda i, c: (i, c, 0)),
                  pl.BlockSpec((1, H_C, L), lambda i, c: (i, 0, c)),
                  pl.BlockSpec((H_C, LANE), lambda i, c: (0, 0)),
                  pl.BlockSpec((CONV_W, CONV_DIM_C), lambda i, c: (0, 0)),
                  pl.BlockSpec((1, CONV_DIM_C), lambda i, c: (0, 0)),
                  pl.BlockSpec((1, LANE), lambda i, c: (0, 0)),
                  pl.BlockSpec((1, D_INNER_C), lambda i, c: (0, 0))],
        out_specs=[pl.BlockSpec((1, L, D_INNER_C), lambda i, c: (i, c, 0)),
                   pl.BlockSpec((1, H_C, P_C, N_C), lambda i, c: (i, 0, 0, 0))],
        out_shape=[jax.ShapeDtypeStruct((b, t, D_INNER_C), F32),
                   jax.ShapeDtypeStruct((b, H_C, P_C, N_C), F32)],
        scratch_shapes=[pltpu.VMEM((CONV_PAD + SSD_CHUNK, CONV_DIM_C), F32), pltpu.VMEM((H_C, P_C, N_C), F32)],
        compiler_params=pltpu.CompilerParams(dimension_semantics=("arbitrary", "arbitrary"), vmem_limit_bytes=VMEM_LIMIT),
        name="ssd_prompt",
    )(proj, proj, proj, dt, dt.transpose(0, 2, 1), an, conv_w, conv_b.reshape(1, CONV_DIM_C), ds,
      y_norm.reshape(1, D_INNER_C))


def _rms_norm(x, g, eps=RMS_EPS):
    xf = x.astype(F32)
    y = xf * lax.rsqrt(jnp.mean(xf * xf, axis=-1, keepdims=True) + eps)
    return (y * g.astype(F32)).astype(x.dtype)


def _split_cols(x, sizes):
    return jnp.split(x, [int(s) for s in np.cumsum(sizes)[:-1]], axis=-1)


def _last_rows(a, n):
    t = a.shape[1]
    if t >= n:
        return a[:, t - n:]
    pad = [(0, 0)] * a.ndim
    pad[1] = (n - t, 0)
    return jnp.pad(a, pad)


def _softmax_attend(q, k, v, mask, sinks=None):
    s = jnp.einsum('...qgrd,...kgd->...grqk', q, k).astype(F32) * (HEAD_DIM ** -0.5)
    s = jnp.where(mask[..., None, None, :, :], s, -jnp.inf)
    m = jnp.max(s, axis=-1, keepdims=True)
    if sinks is not None:
        sk = sinks.astype(F32)[:, :, None, None]
        m = jnp.maximum(m, sk)
    m = jnp.where(jnp.isfinite(m), m, 0.0)
    p = jnp.exp(s - m)
    den = jnp.sum(p, axis=-1, keepdims=True)
    if sinks is not None:
        den = den + jnp.exp(sk - m)
    p = p / jnp.where(den > 0, den, 1.0)
    out = jnp.einsum('...grqk,...kgd->...qgrd', p.astype(v.dtype), v)
    return out, p


def _window_step_attention(q, k_new, v_new, buf, window, sinks):
    t = k_new.shape[1]
    k = jnp.concatenate([buf[:, :, 0], k_new], axis=1)
    v = jnp.concatenate([buf[:, :, 1], v_new], axis=1)
    q_pos = PAST_LEN + jnp.arange(t)
    k_pos = PAST_LEN - window + jnp.arange(window + t)
    diff = q_pos[:, None] - k_pos[None, :]
    mask = (k_pos[None, :] >= 0) & (diff >= 0) & (diff < window)
    out, _ = _softmax_attend(q, k, v, mask, sinks)
    new_buf = jnp.concatenate([buf, jnp.stack([k_new, v_new], axis=2)], axis=1)[:, t:]
    return out, new_buf


def _mlstm_chunk(state, inputs):
    c, n, m = state
    q, k, v, ig, lf = inputs
    L = q.shape[1]
    f_cum = jnp.cumsum(lf, axis=1)
    causal = jnp.tril(jnp.ones((L, L), dtype=bool))
    log_d = jnp.where(causal[None, :, :, None], f_cum[:, :, None, :] - f_cum[:, None, :, :] + ig[:, None, :, :], -jnp.inf)
    m_inter = f_cum + m[:, None, :]
    m_t = jnp.maximum(m_inter, jnp.max(log_d, axis=2))
    d_mat = jnp.exp(log_d - m_t[:, :, None, :])
    w_inter = jnp.exp(m_inter - m_t)
    qk = jnp.einsum('bthd,bshd->btsh', q, k) * d_mat
    num = jnp.einsum('btsh,bshv->bthv', qk, v) + w_inter[..., None] * jnp.einsum('bhvd,bthd->bthv', c, q)
    den = jnp.sum(qk, axis=2) + w_inter * jnp.einsum('bhd,bthd->bth', n, q)
    h = num / jnp.maximum(jnp.abs(den), jnp.exp(-m_t))[..., None]
    m_new = m_t[:, -1]
    w_end = jnp.exp(f_cum[:, -1:] - f_cum + ig - m_new[:, None, :])
    decay = jnp.exp(f_cum[:, -1] + m - m_new)
    c_new = decay[..., None, None] * c + jnp.einsum('bsh,bshv,bshd->bhvd', w_end, v, k)
    n_new = decay[..., None] * n + jnp.einsum('bsh,bshd->bhd', w_end, k)
    return (c_new, n_new, m_new), h


def _ssd_chunk(h, inputs, a_neg):
    x, dt, bm, cm = inputs
    bsz, L = x.shape[:2]
    r = H_C // G_C
    cum = jnp.cumsum(dt * a_neg, axis=1)
    causal = jnp.tril(jnp.ones((L, L), dtype=bool))
    seg = jnp.where(causal[None, :, :, None], cum[:, :, None, :] - cum[:, None, :, :], -jnp.inf)
    decay = jnp.exp(seg).reshape(bsz, L, L, G_C, r)
    dtx = (dt[..., None] * x).reshape(bsz, L, G_C, r, P_C)
    cb = jnp.einsum('btgn,bsgn->btsg', cm, bm)
    y = jnp.einsum('btsg,btsgr,bsgrp->btgrp', cb, decay, dtx)
    hg = h.reshape(bsz, G_C, r, P_C, N_C)
    y = y + jnp.einsum('btgn,bgrpn->btgrp', cm, hg) * jnp.exp(cum).reshape(bsz, L, G_C, r)[..., None]
    w_end = jnp.exp(cum[:, -1:] - cum).reshape(bsz, L, G_C, r)
    h_new = hg * jnp.exp(cum[:, -1]).reshape(bsz, G_C, r)[..., None, None] + jnp.einsum('bsgr,bsgrp,bsgn->bgrpn', w_end, dtx, bm)
    return h_new.reshape(bsz, H_C, P_C, N_C), y.reshape(bsz, L, H_C, P_C)


def _causal_conv(xbc, buf, w, b):
    t = xbc.shape[1]
    xp = jnp.concatenate([buf, xbc], axis=1)
    out = b + sum(xp[:, j:j + t] * w[j] for j in range(CONV_W))
    return jax.nn.silu(out), xp[:, t:]


def _compress(rows, pos, w1, w2):
    bsz, t, g, d = rows.shape
    nb = t // CMP_BLOCK
    blk = rows.reshape(bsz, nb, CMP_BLOCK, g, d) + pos[:, None, :]
    blk = blk.transpose(0, 1, 3, 2, 4).reshape(bsz, nb, g, CMP_BLOCK * d)
    return jax.nn.silu(blk @ w1) @ w2


def _select_blocks(p_cmp, q_pos, n_top):
    bsz, g, r, t, nbc = p_cmp.shape
    ratio = SLC_BLOCK // CMP_BLOCK
    imp = p_cmp.sum(axis=2).reshape(bsz, g, t, nbc // ratio, ratio).sum(axis=-1)
    blk = jnp.arange(nbc // ratio)[None, :]
    cur = (q_pos // SLC_BLOCK)[:, None]
    forced = (blk == 0) | (blk == cur) | (blk == cur - 1)
    score = jnp.where(blk > cur, -jnp.inf, jnp.where(forced, jnp.inf, imp))
    _, sel = lax.top_k(score, n_top)
    return sel.astype(jnp.int32)


def _pad_cols(w, n):
    return jnp.pad(w, ((0, 0), (0, n - w.shape[1])))


def _even_layer(x, w, st, prompt):
    bsz, t, _ = x.shape
    r_a = H_A // KVH_A
    n_in = sum(IN_E_SIZES)
    proj = _norm_matmul(x.reshape(bsz * t, D_MODEL), w['norm_mix'], w['w_in_p'])[:, :n_in].reshape(bsz, t, n_in)
    qa, ka, va, qb, kb, vb, ib, fb, ob = _split_cols(proj, IN_E_SIZES)
    qa = _rms_norm(qa.reshape(bsz, t, KVH_A, r_a, HEAD_DIM), w['q_norm'])
    ka = _rms_norm(ka.reshape(bsz, t, KVH_A, HEAD_DIM), w['k_norm'])
    va = va.reshape(bsz, t, KVH_A, HEAD_DIM)
    sinks = w['sinks'].reshape(KVH_A, r_a)
    if prompt:
        o_a = _window_attend(qa.transpose(0, 2, 3, 1, 4), ka.transpose(0, 2, 1, 3), va.transpose(0, 2, 1, 3),
                             WINDOW_A, w['sinks']).transpose(0, 3, 1, 2, 4)
        new_win = _last_rows(jnp.stack([ka, va], axis=2), WINDOW_A)
    else:
        o_a, new_win = _window_step_attention(qa, ka, va, st['a_win'], WINDOW_A, sinks)
    ig = ib + w['b_igate']
    lf = jax.nn.log_sigmoid(fb + w['b_fgate'])
    if prompt:
        gates = jnp.concatenate([ig.transpose(0, 2, 1), lf.transpose(0, 2, 1)], axis=1)
        hb, c_t, n_rep, m_rep = _mlstm_prompt(qb, kb.transpose(0, 2, 1), vb, gates, ob, w['h_norm'])
        c_new, n_new, m_new = c_t.transpose(0, 1, 3, 2), n_rep[..., 0], m_rep[:, :, 0, 0]
    else:
        qb = qb.reshape(bsz, t, H_B, DK_B)
        kb = kb.reshape(bsz, t, H_B, DK_B) * (DK_B ** -0.5)
        vb = vb.reshape(bsz, t, H_B, DV_B)
        init = (st['b_c'], st['b_n'], st['b_m'])
        (c_new, n_new, m_new), hb = _mlstm_chunk(init, (qb, kb, vb, ig, lf))
        hb = _rms_norm(hb, w['h_norm']) * jax.nn.sigmoid(ob.reshape(bsz, t, H_B, DV_B))
    mix = jnp.concatenate([o_a.reshape(bsz, t, H_A * HEAD_DIM), hb.reshape(bsz, t, H_B * DV_B)], axis=-1)
    xr = _out_proj(mix.reshape(bsz * t, D_MODEL), w['w_out'], x.reshape(bsz * t, D_MODEL))
    xr = _ffn(xr, w['norm_ffn'], w['w_gu'], w['w_down'])
    return xr.reshape(bsz, t, D_MODEL), (new_win, c_new, n_new, m_new)


def _odd_mixers(x, w, st, page_table, prompt):
    bsz, t, _ = x.shape
    r_d = H_D // KVH_D
    n_in = sum(IN_O_SIZES)
    proj_full = _norm_matmul(x.reshape(bsz * t, D_MODEL), w['norm_mix'], w['w_in_p'])
    proj = proj_full[:, :n_in].reshape(bsz, t, n_in)
    zc, xbc, dtc, qd, kvd, gd = _split_cols(proj, IN_O_SIZES)
    dt = jax.nn.softplus(dtc + w['dt_bias'])
    if prompt:
        yc, h_new = _ssd_prompt(proj_full.reshape(bsz, t, -1), dt, -jnp.exp(w['a_log']), w['conv_w'], w['conv_b'],
                                w['d_skip'], w['y_norm'])
        new_conv = xbc[:, t - (CONV_W - 1):]
    else:
        xbc, new_conv = _causal_conv(xbc, st['c_conv'], w['conv_w'], w['conv_b'])
        xc, bc, cc = _split_cols(xbc, (D_INNER_C, G_C * N_C, G_C * N_C))
        xc = xc.reshape(bsz, t, H_C, P_C)
        bc = bc.reshape(bsz, t, G_C, N_C)
        cc = cc.reshape(bsz, t, G_C, N_C)
        h_new, yc = _ssd_chunk(st['c_ssm'], (xc, dt, bc, cc), -jnp.exp(w['a_log']))
        yc = yc + w['d_skip'][:, None] * xc
        yc = yc.reshape(bsz, t, D_INNER_C) * jax.nn.silu(zc)
        yc = _rms_norm(yc.reshape(bsz, t, G_C, D_INNER_C // G_C), w['y_norm'].reshape(G_C, D_INNER_C // G_C))
    qd = _rms_norm(qd.reshape(bsz, t, KVH_D, r_d, HEAD_DIM), w['q_norm'])
    kc, vc, ks, vs, kw, vw = [a.reshape(bsz, t, KVH_D, HEAD_DIM) for a in _split_cols(kvd, (KVH_D * HEAD_DIM,) * 6)]
    ks = _rms_norm(ks, w['k_norm'])
    kw = _rms_norm(kw, w['k_norm'])
    new_kv = jnp.stack([kc, vc, ks, vs], axis=2)
    q_pos = jnp.arange(t) + (0 if prompt else PAST_LEN)
    ks_t, vs_t = ks.transpose(0, 2, 1, 3), vs.transpose(0, 2, 1, 3)

    def comp(rows, j):
        return _compress(rows, w['cmp_pos'][j], w['cmp_w1'][j], w['cmp_w2'][j])
    if prompt:
        kcmp, vcmp = comp(kc, 0), comp(vc, 1)
    else:
        assert t == 1
        pool = jnp.transpose(st['d_kv'], (0, 1, 3, 4, 5, 2)).reshape(
            st['d_kv'].shape[0], st['d_kv'].shape[1], CMP_W + SLC_W, PAGE_SIZE)
        n_pages = page_table.shape[1]
        past = _compress_pages(pool, page_table, w['cmp_pos'], w['cmp_w1'], w['cmp_w2'], st['layer'])
        past = past.reshape(bsz, n_pages * BLK_PER_PAGE, 2, KVH_D, HEAD_DIM)
        pad_t = -(-t // SLC_BLOCK) * SLC_BLOCK
        padw = ((0, 0), (0, pad_t - t), (0, 0), (0, 0))
        kcmp = jnp.concatenate([past[:, :, 0], comp(jnp.pad(kc, padw), 0)], axis=1)
        vcmp = jnp.concatenate([past[:, :, 1], comp(jnp.pad(vc, padw), 1)], axis=1)
    kcmp = _rms_norm(kcmp, w['k_norm'])
    n_top = min(TOP_N, kcmp.shape[1] // (SLC_BLOCK // CMP_BLOCK))
    if prompt:
        qh = qd.transpose(0, 2, 3, 1, 4)
        o_cmp, sel = _cmp_select(qh, kcmp.transpose(0, 2, 1, 3), vcmp.transpose(0, 2, 1, 3), n_top)
        o_slc = _slc_attend(qh, sel, ks_t, vs_t).transpose(0, 3, 1, 2, 4)
        o_cmp = o_cmp.transpose(0, 3, 1, 2, 4)
        o_win = _window_attend(qh, kw.transpose(0, 2, 1, 3), vw.transpose(0, 2, 1, 3), WINDOW_D).transpose(0, 3, 1, 2, 4)
        new_win = _last_rows(jnp.stack([kw, vw], axis=2), WINDOW_D)
    else:
        blk_end = jnp.arange(kcmp.shape[1]) * CMP_BLOCK + (CMP_BLOCK - 1)
        o_cmp, p_cmp = _softmax_attend(qd, kcmp, vcmp, blk_end[None, :] <= q_pos[:, None])
        sel = _select_blocks(p_cmp, q_pos, n_top)[:, :, 0]
        q_rows = jnp.pad(qd[:, 0], ((0, 0), (0, 0), (0, 8 - r_d), (0, 0)))
        o8 = _slc_step(pool, page_table, sel, q_rows, ks_t, vs_t, st['layer'])
        o_slc = o8[:, None, :, :r_d]
        o_win, new_win = _window_step_attention(qd, kw, vw, st['d_win'], WINDOW_D, None)
    g = jax.nn.sigmoid(gd).reshape(bsz, t, KVH_D, r_d, 3)
    o_d = g[..., 0:1] * o_cmp + g[..., 1:2] * o_slc + g[..., 2:3] * o_win
    mix = jnp.concatenate([yc.reshape(bsz, t, D_INNER_C), o_d.reshape(bsz, t, H_D * HEAD_DIM)], axis=-1)
    return mix.reshape(bsz * t, D_MODEL), (h_new, new_conv, new_kv, new_win)


def kernel(x_prompt, x_sample, cache_a_win, state_b_c, state_b_n, state_b_m, state_c_ssm, state_c_conv, cache_d_kv, cache_d_win, page_table, e_norm_mix, e_w_in, e_q_norm, e_k_norm, e_sinks, e_b_igate, e_b_fgate, e_h_norm, e_w_out, e_norm_ffn, e_w_gu, e_w_down, o_norm_mix, o_w_in, o_conv_w, o_conv_b, o_dt_bias, o_a_log, o_d_skip, o_y_norm, o_q_norm, o_k_norm, o_cmp_pos, o_cmp_w1, o_cmp_w2, o_w_out, o_norm_ffn, o_router, o_w_gu, o_w_down):
    i = 0
    we = {'norm_mix': e_norm_mix[i], 'w_in_p': _pad_cols(e_w_in[i], _round_up(sum(IN_E_SIZES), LANE)).astype(BF16),
          'q_norm': e_q_norm[i], 'k_norm': e_k_norm[i],
          'sinks': e_sinks[i], 'b_igate': e_b_igate[i], 'b_fgate': e_b_fgate[i], 'h_norm': e_h_norm[i],
          'w_out': e_w_out[i].astype(BF16), 'norm_ffn': e_norm_ffn[i], 'w_gu': e_w_gu[i].astype(BF16),
          'w_down': e_w_down[i].astype(BF16)}
    wo = {'norm_mix': o_norm_mix[i], 'w_in_p': _pad_cols(o_w_in[i], _round_up(sum(IN_O_SIZES), LANE)).astype(BF16),
          'conv_w': o_conv_w[i], 'conv_b': o_conv_b[i],
          'dt_bias': o_dt_bias[i], 'a_log': o_a_log[i], 'd_skip': o_d_skip[i], 'y_norm': o_y_norm[i],
          'q_norm': o_q_norm[i], 'k_norm': o_k_norm[i], 'cmp_pos': o_cmp_pos[i], 'cmp_w1': o_cmp_w1[i],
          'cmp_w2': o_cmp_w2[i], 'w_out': o_w_out[i].astype(BF16), 'norm_ffn': o_norm_ffn[i],
          'router': _pad_cols(o_router[i], LANE), 'w_gu': o_w_gu[i].astype(BF16), 'w_down': o_w_down[i].astype(BF16)}
    st_e = {'a_win': cache_a_win[i], 'b_c': state_b_c[i], 'b_n': state_b_n[i], 'b_m': state_b_m[i]}
    st_o = {'c_ssm': state_c_ssm[i], 'c_conv': state_c_conv[i], 'd_kv': cache_d_kv, 'layer': i,
            'd_win': cache_d_win[i]}

    xp, sp_e = _even_layer(x_prompt, we, None, True)
    xs, ss_e = _even_layer(x_sample, we, st_e, False)

    mix_p, sp_o = _odd_mixers(xp, wo, None, page_table, True)
    mix_s, ss_o = _odd_mixers(xs, wo, st_o, page_table, False)
    np_rows = mix_p.shape[0]
    xp2, xnp, lgp = _out_proj_route(mix_p, wo['w_out'], xp.reshape(np_rows, D_MODEL), wo['norm_ffn'], wo['router'])
    xs2, xns, lgs = _out_proj_route(mix_s, wo['w_out'], xs.reshape(-1, D_MODEL), wo['norm_ffn'], wo['router'])
    x_all = jnp.concatenate([xp2, xs2], axis=0)
    xn_all = jnp.concatenate([xnp, xns], axis=0)
    lg_all = jnp.concatenate([lgp, lgs], axis=0)[:, :N_EXP]
    y_all = _moe_layer(x_all, xn_all, lg_all, wo['w_gu'], wo['w_down'])
    y_prompt = y_all[:np_rows].reshape(x_prompt.shape)
    y_sample = y_all[np_rows:].reshape(x_sample.shape)

    def one(a):
        return a[None]
    return (y_prompt, y_sample, one(sp_e[0]), one(ss_e[0]), one(sp_e[1]), one(ss_e[1]), one(sp_e[2]), one(ss_e[2]),
            one(sp_e[3]), one(ss_e[3]), one(sp_o[0]), one(ss_o[0]), one(sp_o[1]), one(ss_o[1]),
            one(sp_o[2]), one(ss_o[2]), one(sp_o[3]), one(ss_o[3]))
```

```python
import functools
import math

import numpy as np
import jax
import jax.numpy as jnp
from jax import lax
from jax.experimental import pallas as pl
from jax.experimental.pallas import tpu as pltpu

F32 = jnp.float32
BF16 = jnp.bfloat16

D_MODEL = 1024
PAST_LEN = 16384
PAGE_SIZE = 128
HEAD_DIM = 64
QBLOCK = 128
H_A, KVH_A, WINDOW_A = 8, 2, 128
H_B, DK_B, DV_B, CHUNK_B = 4, 64, 128, 64
P_C, H_C, G_C, N_C, CONV_W, CHUNK_C = 64, 8, 2, 128, 4, 128
D_INNER_C = H_C * P_C
CONV_DIM_C = D_INNER_C + 2 * G_C * N_C
H_D, KVH_D = 8, 2
CMP_BLOCK, SLC_BLOCK, TOP_N, WINDOW_D, SLC_QBLOCK = 32, 64, 16, 512, 32
D_FF, N_EXP, TOP_K = 3584, 8, 2
IN_E_SIZES = (H_A * HEAD_DIM, KVH_A * HEAD_DIM, KVH_A * HEAD_DIM, H_B * DK_B, H_B * DK_B, H_B * DV_B, H_B, H_B, H_B * DV_B)
IN_O_SIZES = (D_INNER_C, CONV_DIM_C, H_C, H_D * HEAD_DIM, 6 * KVH_D * HEAD_DIM, 3 * H_D)

LANE = 128
VMEM_LIMIT = 48 * 1024 * 1024
RMS_EPS = 1e-6
FF_TILE = 1792
ROW_TILE = 512


def _round_up(n, m):
    return -(-n // m) * m


def _rms(x, g):
    return x * lax.rsqrt(jnp.mean(x * x, axis=-1, keepdims=True) + RMS_EPS) * g


def _norm_matmul_body(x_ref, g_ref, w_ref, o_ref):
    xn = _rms(x_ref[...], g_ref[...])
    o_ref[...] = jnp.dot(xn.astype(BF16), w_ref[...], preferred_element_type=F32)


def _norm_matmul(x, gain, w):
    m, k = x.shape
    n = w.shape[1]
    tm = min(ROW_TILE, m)
    return pl.pallas_call(
        _norm_matmul_body,
        grid=(m // tm,),
        in_specs=[pl.BlockSpec((tm, k), lambda i: (i, 0)),
                  pl.BlockSpec((1, k), lambda i: (0, 0)),
                  pl.BlockSpec((k, n), lambda i: (0, 0))],
        out_specs=pl.BlockSpec((tm, n), lambda i: (i, 0)),
        out_shape=jax.ShapeDtypeStruct((m, n), F32),
        compiler_params=pltpu.CompilerParams(dimension_semantics=("arbitrary",), vmem_limit_bytes=VMEM_LIMIT),
        name="norm_in_proj",
    )(x, gain.reshape(1, k), w)


def _out_proj_body(mix_ref, w_ref, res_ref, o_ref):
    o_ref[...] = res_ref[...] + jnp.dot(mix_ref[...].astype(BF16), w_ref[...], preferred_element_type=F32)


def _out_proj(mix, w, res):
    m, k = mix.shape
    n = w.shape[1]
    tm = min(ROW_TILE, m)
    return pl.pallas_call(
        _out_proj_body,
        grid=(m // tm,),
        in_specs=[pl.BlockSpec((tm, k), lambda i: (i, 0)),
                  pl.BlockSpec((k, n), lambda i: (0, 0)),
                  pl.BlockSpec((tm, n), lambda i: (i, 0))],
        out_specs=pl.BlockSpec((tm, n), lambda i: (i, 0)),
        out_shape=jax.ShapeDtypeStruct((m, n), F32),
        compiler_params=pltpu.CompilerParams(dimension_semantics=("arbitrary",), vmem_limit_bytes=VMEM_LIMIT),
        name="out_proj",
    )(mix, w, res)


def _out_proj_route_body(mix_ref, w_ref, res_ref, g_ref, r_ref, o_ref, xn_ref, lg_ref):
    x = res_ref[...] + jnp.dot(mix_ref[...].astype(BF16), w_ref[...], preferred_element_type=F32)
    o_ref[...] = x
    xn = _rms(x, g_ref[...])
    xn_ref[...] = xn.astype(BF16)
    lg_ref[...] = jnp.dot(xn, r_ref[...], preferred_element_type=F32, precision=lax.Precision.HIGHEST)


def _out_proj_route(mix, w, res, gain, router_pad):
    m, k = mix.shape
    n = w.shape[1]
    tm = min(ROW_TILE, m)
    return pl.pallas_call(
        _out_proj_route_body,
        grid=(m // tm,),
        in_specs=[pl.BlockSpec((tm, k), lambda i: (i, 0)),
                  pl.BlockSpec((k, n), lambda i: (0, 0)),
                  pl.BlockSpec((tm, n), lambda i: (i, 0)),
                  pl.BlockSpec((1, n), lambda i: (0, 0)),
                  pl.BlockSpec((n, LANE), lambda i: (0, 0))],
        out_specs=[pl.BlockSpec((tm, n), lambda i: (i, 0)),
                   pl.BlockSpec((tm, n), lambda i: (i, 0)),
                   pl.BlockSpec((tm, LANE), lambda i: (i, 0))],
        out_shape=[jax.ShapeDtypeStruct((m, n), F32),
                   jax.ShapeDtypeStruct((m, n), BF16),
                   jax.ShapeDtypeStruct((m, LANE), F32)],
        compiler_params=pltpu.CompilerParams(dimension_semantics=("arbitrary",), vmem_limit_bytes=VMEM_LIMIT),
        name="out_proj_route",
    )(mix, w, res, gain.reshape(1, n), router_pad)


def _ffn_body(x_ref, g_ref, wg_ref, wu_ref, wd_ref, o_ref, xn_ref, acc_ref):
    c = pl.program_id(1)

    @pl.when(c == 0)
    def _():
        xn_ref[...] = _rms(x_ref[...], g_ref[...]).astype(BF16)
        acc_ref[...] = jnp.zeros_like(acc_ref)

    xn = xn_ref[...]
    gate = jnp.dot(xn, wg_ref[...], preferred_element_type=F32)
    up = jnp.dot(xn, wu_ref[...], preferred_element_type=F32)
    h = (gate * jax.nn.sigmoid(gate) * up).astype(BF16)
    acc_ref[...] += jnp.dot(h, wd_ref[...], preferred_element_type=F32)

    @pl.when(c == pl.num_programs(1) - 1)
    def _():
        o_ref[...] = x_ref[...] + acc_ref[...]


def _ffn(x, gain, w_gu, w_down):
    m, d = x.shape
    tm = min(ROW_TILE, m)
    nc = D_FF // FF_TILE
    return pl.pallas_call(
        _ffn_body,
        grid=(m // tm, nc),
        in_specs=[pl.BlockSpec((tm, d), lambda i, c: (i, 0)),
                  pl.BlockSpec((1, d), lambda i, c: (0, 0)),
                  pl.BlockSpec((d, FF_TILE), lambda i, c: (0, c)),
                  pl.BlockSpec((d, FF_TILE), lambda i, c: (0, c + nc)),
                  pl.BlockSpec((FF_TILE, d), lambda i, c: (c, 0))],
        out_specs=pl.BlockSpec((tm, d), lambda i, c: (i, 0)),
        out_shape=jax.ShapeDtypeStruct((m, d), F32),
        scratch_shapes=[pltpu.VMEM((tm, d), BF16), pltpu.VMEM((tm, d), F32)],
        compiler_params=pltpu.CompilerParams(dimension_semantics=("arbitrary", "arbitrary"), vmem_limit_bytes=VMEM_LIMIT),
        name="ffn_dense",
    )(x, gain.reshape(1, d), w_gu, w_gu, w_down)


def _moe_body(te_ref, tv_ref, x_ref, s_ref, wg_ref, wu_ref, wd_ref, o_ref, acc_ref):
    i = pl.program_id(0)
    c = pl.program_id(1)
    valid = tv_ref[i] > 0

    @pl.when(c == 0)
    def _():
        acc_ref[...] = jnp.zeros_like(acc_ref)

    @pl.when(valid)
    def _():
        x = x_ref[...]
        gate = jnp.dot(x, wg_ref[0], preferred_element_type=F32)
        up = jnp.dot(x, wu_ref[0], preferred_element_type=F32)
        h = (gate * jax.nn.sigmoid(gate) * up).astype(BF16)
        acc_ref[...] += jnp.dot(h, wd_ref[0], preferred_element_type=F32)

    @pl.when(c == pl.num_programs(1) - 1)
    def _():
        o_ref[...] = s_ref[...] * acc_ref[...]


def _moe_grouped(xs, scale, tile_expert, tile_valid, w_gu, w_down, tm):
    p, d = xs.shape
    nc = D_FF // FF_TILE
    grid_spec = pltpu.PrefetchScalarGridSpec(
        num_scalar_prefetch=2,
        grid=(p // tm, nc),
        in_specs=[pl.BlockSpec((tm, d), lambda i, c, te, tv: (i, 0)),
                  pl.BlockSpec((tm, 1), lambda i, c, te, tv: (i, 0)),
                  pl.BlockSpec((1, d, FF_TILE), lambda i, c, te, tv: (te[i], 0, c)),
                  pl.BlockSpec((1, d, FF_TILE), lambda i, c, te, tv: (te[i], 0, c + nc)),
                  pl.BlockSpec((1, FF_TILE, d), lambda i, c, te, tv: (te[i], c, 0))],
        out_specs=pl.BlockSpec((tm, d), lambda i, c, te, tv: (i, 0)),
        scratch_shapes=[pltpu.VMEM((tm, d), F32)],
    )
    return pl.pallas_call(
        _moe_body,
        grid_spec=grid_spec,
        out_shape=jax.ShapeDtypeStruct((p, d), F32),
        compiler_params=pltpu.CompilerParams(dimension_semantics=("arbitrary", "arbitrary"), vmem_limit_bytes=VMEM_LIMIT),
        name="moe_grouped",
    )(tile_expert, tile_valid, xs, scale, w_gu, w_gu, w_down)


def _moe_layer(x_rows, xn_rows, logits, w_gu, w_down, tm=ROW_TILE):
    n = x_rows.shape[0]
    top_v, top_i = lax.top_k(logits, TOP_K)
    gate = jax.nn.softmax(top_v, axis=-1)
    flat_e = top_i.reshape(-1).astype(jnp.int32)
    na = flat_e.shape[0]
    order = jnp.argsort(flat_e, stable=True).astype(jnp.int32)
    inv = jnp.argsort(order).astype(jnp.int32)
    counts = jnp.sum(jax.nn.one_hot(flat_e, N_EXP, dtype=jnp.int32), axis=0)
    tiles_per_e = (counts + tm - 1) // tm
    tile_end = jnp.cumsum(tiles_per_e)
    tile_start = tile_end - tiles_per_e
    group_start = jnp.cumsum(counts) - counts
    se = flat_e[order]
    pos_sorted = tile_start[se] * tm + (jnp.arange(na, dtype=jnp.int32) - group_start[se])
    pos_of_a = pos_sorted[inv]
    n_tiles = na // tm + N_EXP
    p = n_tiles * tm
    tile_ids = jnp.arange(n_tiles, dtype=jnp.int32)
    total_tiles = tile_end[-1]
    tile_expert = jnp.searchsorted(tile_end, jnp.minimum(tile_ids, total_tiles - 1), side="right").astype(jnp.int32)
    tile_expert = jnp.clip(tile_expert, 0, N_EXP - 1)
    tile_valid = (tile_ids < total_tiles).astype(jnp.int32)
    p_idx = jnp.arange(p, dtype=jnp.int32)
    e_p = tile_expert[p_idx // tm]
    r_p = p_idx - tile_start[e_p] * tm
    row_valid = (r_p < counts[e_p]) & (tile_valid[p_idx // tm] > 0)
    a_p = order[jnp.clip(group_start[e_p] + r_p, 0, na - 1)]
    tok_of_pos = jnp.where(row_valid, a_p // TOP_K, 0)
    gate_of_pos = jnp.where(row_valid, gate.reshape(-1)[a_p], 0.0)
    xs = xn_rows[tok_of_pos]
    ys = _moe_grouped(xs, gate_of_pos.reshape(p, 1), tile_expert, tile_valid, w_gu, w_down, tm)
    pa = pos_of_a.reshape(n, TOP_K)
    return x_rows + (ys[pa[:, 0]] + ys[pa[:, 1]])


QT = 128
KT = 512
NEG = -1e30
RANK_GROUP = 16
LOG2E = 1.4426950408889634


def _cmp_select_body(q_ref, kc_ref, vc_ref, o_ref, sel_ref, rank_ref, *, n_top):
    qi = pl.program_id(2)
    r = q_ref.shape[2]
    q = q_ref[0, 0].reshape(r * QT, HEAD_DIM)
    s = lax.dot_general(q.astype(BF16), kc_ref[0, 0].astype(BF16), (((1,), (1,)), ((), ())),
                        preferred_element_type=F32) * (HEAD_DIM ** -0.5)
    row = lax.broadcasted_iota(jnp.int32, (r * QT, LANE), 0)
    lane = lax.broadcasted_iota(jnp.int32, (r * QT, LANE), 1)
    t_pos = qi * QT + (row & (QT - 1))
    s = jnp.where(lane * CMP_BLOCK + (CMP_BLOCK - 1) <= t_pos, s, -jnp.inf)
    m = jnp.max(s, axis=-1, keepdims=True)
    m = jnp.where(m > -jnp.inf, m, 0.0)
    p = jnp.exp(s - m)
    den = jnp.sum(p, axis=-1, keepdims=True)
    p = p / jnp.where(den > 0, den, 1.0)
    o = jnp.dot(p.astype(BF16), vc_ref[0, 0].astype(BF16), preferred_element_type=F32)
    o_ref[0, 0] = o.reshape(r, QT, HEAD_DIM)
    pt = p[0:QT]
    for h in range(1, r):
        pt = pt + p[h * QT:(h + 1) * QT]
    lane1 = lax.broadcasted_iota(jnp.int32, (QT, LANE), 1)
    row1 = lax.broadcasted_iota(jnp.int32, (QT, LANE), 0)
    even = (lane1 & 1) == 0
    lo = jnp.where(even, pt, pltpu.roll(pt, 1, 1))
    hi = jnp.where(even, pltpu.roll(pt, LANE - 1, 1), pt)
    imp = lo + hi
    blk = lane1 >> 1
    cur = (qi * QT + row1) >> 6
    forced = (blk == 0) | (blk == cur) | (blk == cur - 1)
    score = jnp.where(blk > cur, -jnp.inf, jnp.where(forced, jnp.inf, imp))
    lane_f = lane1.astype(F32)
    rank_ref[...] = jnp.zeros((QT, LANE), F32)
    for grp in range(0, LANE // 2, RANK_GROUP):
        @pl.when(2 * qi + 1 >= grp)
        def _():
            cnt = rank_ref[...]
            for j in range(grp, grp + RANK_GROUP):
                col = jnp.broadcast_to(score[:, 2 * j:2 * j + 1], (QT, LANE))
                later = jnp.where(lane_f >= 2.0 * j + 2.0, 1.0, 0.0)
                cnt = cnt + jnp.where(col > score, 1.0, 0.0) + jnp.where(col == score, later, 0.0)
            rank_ref[...] = cnt
    sel_ref[0, 0] = jnp.where((rank_ref[...] < n_top) & (blk <= cur), 1.0, 0.0)


def _cmp_select(q, kcmp, vcmp, n_top):
    b, g, r, t, d = q.shape
    assert kcmp.shape == (b, g, LANE, d) and t // CMP_BLOCK == LANE
    return pl.pallas_call(
        functools.partial(_cmp_select_body, n_top=n_top),
        grid=(b, g, t // QT),
        in_specs=[pl.BlockSpec((1, 1, r, QT, d), lambda i, j, k: (i, j, 0, k, 0)),
                  pl.BlockSpec((1, 1, LANE, d), lambda i, j, k: (i, j, 0, 0)),
                  pl.BlockSpec((1, 1, LANE, d), lambda i, j, k: (i, j, 0, 0))],
        out_specs=[pl.BlockSpec((1, 1, r, QT, d), lambda i, j, k: (i, j, 0, k, 0)),
                   pl.BlockSpec((1, 1, QT, LANE), lambda i, j, k: (i, j, k, 0))],
        out_shape=[jax.ShapeDtypeStruct((b, g, r, t, d), F32),
                   jax.ShapeDtypeStruct((b, g, t, LANE), F32)],
        scratch_shapes=[pltpu.VMEM((QT, LANE), F32)],
        compiler_params=pltpu.CompilerParams(dimension_semantics=("arbitrary",) * 3, vmem_limit_bytes=VMEM_LIMIT),
        name="nsa_cmp_select",
    )(q, kcmp, vcmp)


def _slc_attend_body(q_ref, sel_ref, k_ref, v_ref, o_ref):
    qi = pl.program_id(2)
    r = q_ref.shape[2]
    rows = r * QT
    q = (q_ref[0, 0].reshape(rows, HEAD_DIM) * (HEAD_DIM ** -0.5 * LOG2E)).astype(BF16)
    sel = sel_ref[0, 0]
    e_row = lax.broadcasted_iota(jnp.int32, (LANE, KT), 0)
    e_col = lax.broadcasted_iota(jnp.int32, (LANE, KT), 1)
    expand = jnp.where(e_row == 2 * (e_col >> 6), 1.0, 0.0).astype(BF16)
    t_pos = qi * QT + lax.broadcasted_iota(jnp.int32, (QT, KT), 0)
    k_off = lax.broadcasted_iota(jnp.int32, (QT, KT), 1)
    n_kt = (qi * QT + QT - 1) // KT + 1
    lanes_per_kt = 2 * (KT // SLC_BLOCK)
    ones = jnp.ones((KT, LANE - HEAD_DIM), BF16)

    def body(kt, carry):
        m, acc = carry
        start = pl.multiple_of(kt * KT, KT)
        k = k_ref[0, 0, pl.ds(start, KT), :].astype(BF16)
        v = v_ref[0, 0, pl.ds(start, KT), :].astype(BF16)
        v1 = jnp.concatenate([v, ones], axis=-1)
        s = lax.dot_general(q, k, (((1,), (1,)), ((), ())), preferred_element_type=F32)
        sel_kt = pltpu.roll(sel, (LANE - kt * lanes_per_kt) % LANE, 1).astype(BF16)
        picked = jnp.dot(sel_kt, expand, preferred_element_type=F32)
        bias = jnp.where((picked > 0.5) & (start + k_off <= t_pos), 0.0, NEG)
        s = s + jnp.concatenate([bias] * r, axis=0)
        m_new = jnp.maximum(m, jnp.max(s, axis=-1, keepdims=True))
        alpha = jnp.exp2(m - m_new)
        p = jnp.exp2(s - m_new).astype(BF16)
        acc = alpha * acc + jnp.dot(p, v1, preferred_element_type=F32)
        return m_new, acc

    init = (jnp.full((rows, 1), NEG, F32), jnp.zeros((rows, LANE), F32))
    m, acc = lax.fori_loop(0, n_kt, body, init)
    o_ref[0, 0] = (acc[:, :HEAD_DIM] / acc[:, HEAD_DIM:HEAD_DIM + 1]).reshape(r, QT, HEAD_DIM)


def _slc_attend(q, sel, ks, vs):
    b, g, r, t, d = q.shape
    return pl.pallas_call(
        _slc_attend_body,
        grid=(b, g, t // QT),
        in_specs=[pl.BlockSpec((1, 1, r, QT, d), lambda i, j, k: (i, j, 0, k, 0)),
                  pl.BlockSpec((1, 1, QT, LANE), lambda i, j, k: (i, j, k, 0)),
                  pl.BlockSpec((1, 1, t, d), lambda i, j, k: (i, j, 0, 0)),
                  pl.BlockSpec((1, 1, t, d), lambda i, j, k: (i, j, 0, 0))],
        out_specs=pl.BlockSpec((1, 1, r, QT, d), lambda i, j, k: (i, j, 0, k, 0)),
        out_shape=jax.ShapeDtypeStruct((b, g, r, t, d), F32),
        compiler_params=pltpu.CompilerParams(dimension_semantics=("arbitrary",) * 3, vmem_limit_bytes=VMEM_LIMIT),
        name="nsa_slc_attend",
    )(q, sel, ks, vs)


CMP_HIDDEN = 128
CMP_W = 2 * KVH_D * HEAD_DIM
SLC_W = 2 * KVH_D * HEAD_DIM
BLK_PER_PAGE = PAGE_SIZE // CMP_BLOCK
PAGES_PER_STEP = 64


def _page_copy(pool_ref, buf_ref, sem_ref, layer, page, slot, j):
    return pltpu.make_async_copy(pool_ref.at[layer, page, pl.ds(0, CMP_W), :], buf_ref.at[slot, j], sem_ref.at[slot])


def _compress_pages_body(pt_ref, pool_ref, post_ref, perm_ref, w1_ref, w2_ref, o_ref, buf_ref, xk_ref, xv_ref, sem_ref, *,
                         layer, n_steps):
    x_ref = (xk_ref, xv_ref)
    s = pl.program_id(0)
    slot = s % 2

    def start(step, slot_):
        for j in range(PAGES_PER_STEP):
            _page_copy(pool_ref, buf_ref, sem_ref, layer, pt_ref[step * PAGES_PER_STEP + j], slot_, j).start()

    @pl.when(s == 0)
    def _():
        start(0, 0)

    @pl.when(s + 1 < n_steps)
    def _():
        start(s + 1, 1 - slot)

    for j in range(PAGES_PER_STEP):
        _page_copy(pool_ref, buf_ref, sem_ref, layer, 0, slot, j).wait()

    for pp in range(PAGES_PER_STEP // 2):
        for br in range(2):
            t0 = buf_ref[slot, 2 * pp, br * LANE:(br + 1) * LANE, :] + post_ref[br]
            t1 = buf_ref[slot, 2 * pp + 1, br * LANE:(br + 1) * LANE, :] + post_ref[br]
            tc = jnp.concatenate([t0, t1], axis=-1).astype(BF16)
            tp = jnp.dot(tc, perm_ref[...], preferred_element_type=F32)
            x_ref[br][pp] = tp.T

    m = PAGES_PER_STEP * BLK_PER_PAGE
    for br in range(2):
        acc = jnp.zeros((m, KVH_D * CMP_HIDDEN), F32)
        for r in range(0, CMP_BLOCK, 2):
            x = jnp.concatenate([x_ref[br][:, rr * 8:(rr + 1) * 8, :].reshape(m, LANE) for rr in (r, r + 1)], axis=-1)
            acc = acc + jnp.dot(x.astype(BF16), w1_ref[br, r // 2], preferred_element_type=F32)
        h = acc * jax.nn.sigmoid(acc)
        o_ref[:, br * LANE:(br + 1) * LANE] = jnp.dot(h.astype(BF16), w2_ref[br], preferred_element_type=F32)


def _block_diag2(w):
    z = jnp.zeros_like(w)
    return jnp.concatenate([jnp.concatenate([w, z], axis=-1), jnp.concatenate([z, w], axis=-1)], axis=-2)


def _compress_pages(pool, page_table, cmp_pos, cmp_w1, cmp_w2, layer):
    bsz, n_pages = page_table.shape
    total = bsz * n_pages
    n_steps = total // PAGES_PER_STEP
    assert total % PAGES_PER_STEP == 0 and KVH_D == 2
    w1 = cmp_w1.reshape(2, CMP_BLOCK, HEAD_DIM, CMP_HIDDEN)
    w1_bd = _block_diag2(w1).astype(BF16).reshape(2, CMP_BLOCK // 2, 2 * LANE, KVH_D * CMP_HIDDEN)
    w2_bd = _block_diag2(cmp_w2).astype(BF16)
    pos_t = jnp.tile(jnp.transpose(cmp_pos, (0, 2, 1)), (1, KVH_D, BLK_PER_PAGE))
    lane_in = np.arange(2 * PAGE_SIZE)
    page2, blk, r = lane_in // PAGE_SIZE, (lane_in % PAGE_SIZE) // CMP_BLOCK, lane_in % CMP_BLOCK
    perm = np.zeros((2 * PAGE_SIZE, 2 * PAGE_SIZE), np.float32)
    perm[lane_in, r * (2 * BLK_PER_PAGE) + page2 * BLK_PER_PAGE + blk] = 1.0
    perm = jnp.asarray(perm, BF16)
    m = PAGES_PER_STEP * BLK_PER_PAGE
    grid_spec = pltpu.PrefetchScalarGridSpec(
        num_scalar_prefetch=1,
        grid=(n_steps,),
        in_specs=[pl.BlockSpec(memory_space=pl.ANY),
                  pl.BlockSpec((2, LANE, PAGE_SIZE), lambda s, pt: (0, 0, 0)),
                  pl.BlockSpec((2 * PAGE_SIZE, 2 * PAGE_SIZE), lambda s, pt: (0, 0)),
                  pl.BlockSpec((2, CMP_BLOCK // 2, 2 * LANE, KVH_D * CMP_HIDDEN), lambda s, pt: (0, 0, 0, 0)),
                  pl.BlockSpec((2, KVH_D * CMP_HIDDEN, LANE), lambda s, pt: (0, 0, 0))],
        out_specs=pl.BlockSpec((m, CMP_W), lambda s, pt: (s, 0)),
        scratch_shapes=[pltpu.VMEM((2, PAGES_PER_STEP, CMP_W, PAGE_SIZE), F32),
                        pltpu.VMEM((PAGES_PER_STEP // 2, 2 * PAGE_SIZE, LANE), F32),
                        pltpu.VMEM((PAGES_PER_STEP // 2, 2 * PAGE_SIZE, LANE), F32),
                        pltpu.SemaphoreType.DMA((2,))],
    )
    return pl.pallas_call(
        functools.partial(_compress_pages_body, layer=layer, n_steps=n_steps),
        grid_spec=grid_spec,
        out_shape=jax.ShapeDtypeStruct((total * BLK_PER_PAGE, CMP_W), F32),
        compiler_params=pltpu.CompilerParams(dimension_semantics=("arbitrary",), vmem_limit_bytes=VMEM_LIMIT),
        name="nsa_compress_pages",
    )(page_table.reshape(-1), pool, pos_t, perm, w1_bd, w2_bd)


def _slc_copy(pool_ref, buf_ref, sem_ref, layer, page, feat0, slot, idx):
    return pltpu.make_async_copy(pool_ref.at[layer, page, pl.ds(feat0, HEAD_DIM), :], buf_ref.at[slot, idx], sem_ref.at[slot])


def _slc_step_body(pt_ref, sel_ref, pool_ref, q_ref, kn_ref, vn_ref, o_ref, kbuf_ref, vbuf_ref, sem_ref, *,
                   layer, n_seq, n_pages, new_blk):
    b = pl.program_id(0)
    slot = b % 2
    n_sel = KVH_D * TOP_N

    def copies(seq, slot_, from_table):
        out = []
        for g in range(KVH_D):
            for i in range(TOP_N):
                idx = g * TOP_N + i
                page = 0
                if from_table:
                    blk = jnp.minimum(sel_ref[seq * n_sel + idx], new_blk - 1)
                    page = pt_ref[seq * n_pages + (blk >> 1)]
                out.append(_slc_copy(pool_ref, kbuf_ref, sem_ref, layer, page, CMP_W + g * HEAD_DIM, slot_, idx))
                out.append(_slc_copy(pool_ref, vbuf_ref, sem_ref, layer, page, CMP_W + (KVH_D + g) * HEAD_DIM, slot_, idx))
        return out

    @pl.when(b == 0)
    def _():
        for c in copies(0, 0, True):
            c.start()

    @pl.when(b + 1 < n_seq)
    def _():
        for c in copies(b + 1, 1 - slot, True):
            c.start()

    for c in copies(0, slot, False):
        c.wait()

    half = lax.broadcasted_iota(jnp.int32, (8, LANE), 1) >> 6
    for g in range(KVH_D):
        q = q_ref[0, g] * (HEAD_DIM ** -0.5)
        qb = q.astype(BF16)
        s_parts = []
        for i in range(TOP_N):
            blk = sel_ref[b * n_sel + g * TOP_N + i]
            s_i = jnp.dot(qb, kbuf_ref[slot, g * TOP_N + i].astype(BF16), preferred_element_type=F32)
            ok = (half == (blk & 1)) & (blk < new_blk)
            s_parts.append(jnp.where(ok, s_i, NEG))
        s = jnp.concatenate(s_parts, axis=-1)
        s_new = jnp.sum(q * kn_ref[0, g], axis=-1, keepdims=True)
        m = jnp.maximum(jnp.max(s, axis=-1, keepdims=True), s_new)
        p = jnp.exp(s - m)
        p_new = jnp.exp(s_new - m)
        den = jnp.sum(p, axis=-1, keepdims=True) + p_new
        o = p_new * vn_ref[0, g]
        for i in range(TOP_N):
            p_i = p[:, i * LANE:(i + 1) * LANE].astype(BF16)
            o = o + lax.dot_general(p_i, vbuf_ref[slot, g * TOP_N + i].astype(BF16), (((1,), (1,)), ((), ())),
                                    preferred_element_type=F32)
        o_ref[0, g] = o / den


def _slc_step(pool, page_table, sel, q, k_new, v_new, layer):
    bsz, n_pages = page_table.shape
    n_sel = KVH_D * TOP_N
    grid_spec = pltpu.PrefetchScalarGridSpec(
        num_scalar_prefetch=2,
        grid=(bsz,),
        in_specs=[pl.BlockSpec(memory_space=pl.ANY),
                  pl.BlockSpec((1, KVH_D, 8, HEAD_DIM), lambda b, pt, sl: (b, 0, 0, 0)),
                  pl.BlockSpec((1, KVH_D, 1, HEAD_DIM), lambda b, pt, sl: (b, 0, 0, 0)),
                  pl.BlockSpec((1, KVH_D, 1, HEAD_DIM), lambda b, pt, sl: (b, 0, 0, 0))],
        out_specs=pl.BlockSpec((1, KVH_D, 8, HEAD_DIM), lambda b, pt, sl: (b, 0, 0, 0)),
        scratch_shapes=[pltpu.VMEM((2, n_sel, HEAD_DIM, PAGE_SIZE), F32),
                        pltpu.VMEM((2, n_sel, HEAD_DIM, PAGE_SIZE), F32),
                        pltpu.SemaphoreType.DMA((2,))],
    )
    return pl.pallas_call(
        functools.partial(_slc_step_body, layer=layer, n_seq=bsz, n_pages=n_pages, new_blk=2 * n_pages),
        grid_spec=grid_spec,
        out_shape=jax.ShapeDtypeStruct((bsz, KVH_D, 8, HEAD_DIM), F32),
        compiler_params=pltpu.CompilerParams(dimension_semantics=("arbitrary",), vmem_limit_bytes=VMEM_LIMIT),
        name="nsa_slc_step",
    )(page_table.reshape(-1), sel.reshape(-1), pool, q, k_new, v_new)


def _window_attend_body(sink_ref, q_ref, k_ref, v_ref, o_ref, *, window, use_sinks):
    g = pl.program_id(1)
    qi = pl.program_id(2)
    r = q_ref.shape[2]
    rows = r * QT
    span = window + QT
    q = (q_ref[0, 0].reshape(rows, HEAD_DIM) * (HEAD_DIM ** -0.5)).astype(BF16)
    start = pl.multiple_of(jnp.maximum(qi * QT - window, 0), QT)
    k = k_ref[0, 0, pl.ds(start, span), :].astype(BF16)
    v = v_ref[0, 0, pl.ds(start, span), :].astype(BF16)
    s = lax.dot_general(q, k, (((1,), (1,)), ((), ())), preferred_element_type=F32)
    row = lax.broadcasted_iota(jnp.int32, (rows, span), 0)
    diff = qi * QT + (row & (QT - 1)) - (start + lax.broadcasted_iota(jnp.int32, (rows, span), 1))
    s = jnp.where((diff >= 0) & (diff < window), s, -jnp.inf)
    m = jnp.max(s, axis=-1, keepdims=True)
    if use_sinks:
        head = lax.broadcasted_iota(jnp.int32, (rows, 1), 0) // QT
        sink = jnp.zeros((rows, 1), F32)
        for h in range(r):
            sink = jnp.where(head == h, sink_ref[g * r + h], sink)
        m = jnp.maximum(m, sink)
    p = jnp.exp(s - m)
    den = jnp.sum(p, axis=-1, keepdims=True)
    if use_sinks:
        den = den + jnp.exp(sink - m)
    o = jnp.dot(p.astype(BF16), v, preferred_element_type=F32) / den
    o_ref[0, 0] = o.reshape(r, QT, HEAD_DIM)


def _window_attend(q, k, v, window, sinks=None):
    b, g, r, t, d = q.shape
    use_sinks = sinks is not None
    if sinks is None:
        sinks = jnp.zeros((g * r,), F32)
    grid_spec = pltpu.PrefetchScalarGridSpec(
        num_scalar_prefetch=1,
        grid=(b, g, t // QT),
        in_specs=[pl.BlockSpec((1, 1, r, QT, d), lambda i, j, n, sk: (i, j, 0, n, 0)),
                  pl.BlockSpec((1, 1, t, d), lambda i, j, n, sk: (i, j, 0, 0)),
                  pl.BlockSpec((1, 1, t, d), lambda i, j, n, sk: (i, j, 0, 0))],
        out_specs=pl.BlockSpec((1, 1, r, QT, d), lambda i, j, n, sk: (i, j, 0, n, 0)),
    )
    return pl.pallas_call(
        functools.partial(_window_attend_body, window=window, use_sinks=use_sinks),
        grid_spec=grid_spec,
        out_shape=jax.ShapeDtypeStruct((b, g, r, t, d), F32),
        compiler_params=pltpu.CompilerParams(dimension_semantics=("arbitrary",) * 3, vmem_limit_bytes=VMEM_LIMIT),
        name="window_attend_%d" % window,
    )(sinks.astype(F32), q, k, v)


MLSTM_CHUNK = 128
HIGHEST = lax.Precision.HIGHEST


def _mlstm_body(q_ref, kt_ref, v_ref, g_ref, ob_ref, hn_ref, o_ref, c_out, n_out, m_out, c_ref, n_ref, m_ref):
    ci = pl.program_id(1)
    L = MLSTM_CHUNK

    @pl.when(ci == 0)
    def _():
        c_ref[...] = jnp.zeros_like(c_ref)
        n_ref[...] = jnp.zeros_like(n_ref)
        m_ref[...] = jnp.zeros_like(m_ref)

    row = lax.broadcasted_iota(jnp.int32, (L, L), 0)
    col = lax.broadcasted_iota(jnp.int32, (L, L), 1)
    causal = col <= row
    upper = jnp.where(row <= col, 1.0, 0.0)
    for h in range(H_B):
        q = q_ref[0, :, h * DK_B:(h + 1) * DK_B]
        kt = kt_ref[0, h * DK_B:(h + 1) * DK_B, :] * (DK_B ** -0.5)
        v = v_ref[0, :, h * DV_B:(h + 1) * DV_B]
        ig = g_ref[0, h:h + 1, :]
        lf = g_ref[0, H_B + h:H_B + h + 1, :]
        m_prev = m_ref[h, 0:1, 0:1]
        f_col = jnp.sum(jnp.where(causal, jnp.broadcast_to(lf, (L, L)), 0.0), axis=-1, keepdims=True)
        f_row = jnp.dot(jnp.broadcast_to(lf, (8, L)), upper, preferred_element_type=F32, precision=HIGHEST)[0:1]
        log_d = jnp.where(causal, f_col - f_row + ig, -jnp.inf)
        m_inter = f_col + m_prev
        m_t = jnp.maximum(m_inter, jnp.max(log_d, axis=-1, keepdims=True))
        d_mat = jnp.exp(log_d - m_t)
        w_inter = jnp.exp(m_inter - m_t)
        qb = q.astype(BF16)
        qk = jnp.dot(qb, kt.astype(BF16), preferred_element_type=F32) * d_mat
        num = jnp.dot(qk.astype(BF16), v.astype(BF16), preferred_element_type=F32)
        num = num + w_inter * jnp.dot(qb, c_ref[h].astype(BF16), preferred_element_type=F32)
        qn = jnp.dot(qb, n_ref[h].astype(BF16), preferred_element_type=F32)[:, 0:1]
        den = jnp.sum(qk, axis=-1, keepdims=True) + w_inter * qn
        hh = num / jnp.maximum(jnp.abs(den), jnp.exp(-m_t))
        hh = hh * lax.rsqrt(jnp.mean(hh * hh, axis=-1, keepdims=True) + RMS_EPS) * hn_ref[...]
        o_ref[0, :, h * DV_B:(h + 1) * DV_B] = hh * jax.nn.sigmoid(ob_ref[0, :, h * DV_B:(h + 1) * DV_B])
        m_new = m_t[L - 1:L, :]
        f_last = f_col[L - 1:L, :]
        w_end = jnp.exp(f_last - f_row + ig - m_new)
        decay = jnp.exp(f_last + m_prev - m_new)
        ktw = kt * w_end
        c_ref[h] = decay * c_ref[h] + jnp.dot(ktw.astype(BF16), v.astype(BF16), preferred_element_type=F32)
        n_ref[h] = decay * n_ref[h] + jnp.sum(ktw, axis=-1, keepdims=True)
        m_ref[h] = jnp.broadcast_to(m_new, m_ref.shape[1:])

    @pl.when(ci == pl.num_programs(1) - 1)
    def _():
        c_out[0] = c_ref[...]
        n_out[0] = n_ref[...]
        m_out[0] = m_ref[...]


def _mlstm_prompt(q, kt, v, gates, ob, h_norm):
    b, t, _ = q.shape
    L = MLSTM_CHUNK
    nc = t // L
    return pl.pallas_call(
        _mlstm_body,
        grid=(b, nc),
        in_specs=[pl.BlockSpec((1, L, H_B * DK_B), lambda i, c: (i, c, 0)),
                  pl.BlockSpec((1, H_B * DK_B, L), lambda i, c: (i, 0, c)),
                  pl.BlockSpec((1, L, H_B * DV_B), lambda i, c: (i, c, 0)),
                  pl.BlockSpec((1, 2 * H_B, L), lambda i, c: (i, 0, c)),
                  pl.BlockSpec((1, L, H_B * DV_B), lambda i, c: (i, c, 0)),
                  pl.BlockSpec((1, DV_B), lambda i, c: (0, 0))],
        out_specs=[pl.BlockSpec((1, L, H_B * DV_B), lambda i, c: (i, c, 0)),
                   pl.BlockSpec((1, H_B, DK_B, DV_B), lambda i, c: (i, 0, 0, 0)),
                   pl.BlockSpec((1, H_B, DK_B, LANE), lambda i, c: (i, 0, 0, 0)),
                   pl.BlockSpec((1, H_B, 8, LANE), lambda i, c: (i, 0, 0, 0))],
        out_shape=[jax.ShapeDtypeStruct((b, t, H_B * DV_B), F32),
                   jax.ShapeDtypeStruct((b, H_B, DK_B, DV_B), F32),
                   jax.ShapeDtypeStruct((b, H_B, DK_B, LANE), F32),
                   jax.ShapeDtypeStruct((b, H_B, 8, LANE), F32)],
        scratch_shapes=[pltpu.VMEM((H_B, DK_B, DV_B), F32), pltpu.VMEM((H_B, DK_B, LANE), F32),
                        pltpu.VMEM((H_B, 8, LANE), F32)],
        compiler_params=pltpu.CompilerParams(dimension_semantics=("arbitrary", "arbitrary"), vmem_limit_bytes=VMEM_LIMIT),
        name="mlstm_prompt",
    )(q, kt, v, gates, ob, h_norm.reshape(1, DV_B))


SSD_CHUNK = 128
CONV_PAD = 8


def _ssd_body(z_ref, xr_ref, bcr_ref, dtc_ref, dtr_ref, an_ref, cw_ref, cb_ref, ds_ref, yn_ref,
              o_ref, h_out, xp_ref, h_ref):
    ci = pl.program_id(1)
    L = SSD_CHUNK
    r = H_C // G_C

    @pl.when(ci == 0)
    def _():
        xp_ref[0:CONV_PAD, :] = jnp.zeros((CONV_PAD, CONV_DIM_C), F32)
        h_ref[...] = jnp.zeros_like(h_ref)

    xp_ref[CONV_PAD:CONV_PAD + L, 0:D_INNER_C] = xr_ref[0]
    xp_ref[CONV_PAD:CONV_PAD + L, D_INNER_C:CONV_DIM_C] = bcr_ref[0]
    conv = cb_ref[...]
    for j in range(CONV_W):
        off = CONV_PAD - (CONV_W - 1) + j
        conv = conv + xp_ref[off:off + L, :] * cw_ref[j:j + 1, :]
    tail = xp_ref[L:L + CONV_PAD, :]
    xp_ref[0:CONV_PAD, :] = tail
    xbc = conv * jax.nn.sigmoid(conv)
    x = xbc[:, 0:D_INNER_C]
    bm = xbc[:, D_INNER_C:D_INNER_C + G_C * N_C]
    cm = xbc[:, D_INNER_C + G_C * N_C:CONV_DIM_C]

    row = lax.broadcasted_iota(jnp.int32, (L, L), 0)
    col = lax.broadcasted_iota(jnp.int32, (L, L), 1)
    causal = col <= row
    lower = jnp.where(causal, 1.0, 0.0)
    upper = jnp.where(row <= col, 1.0, 0.0)
    dt_c = dtc_ref[0]
    dt_r = dtr_ref[0]
    a_c = dt_c * an_ref[0:1, 0:H_C]
    a_r = dt_r * an_ref[:, H_C:H_C + 1]
    cum_c = jnp.dot(lower, a_c, preferred_element_type=F32, precision=HIGHEST)
    cum_r = jnp.dot(a_r, upper, preferred_element_type=F32, precision=HIGHEST)
    ys = []
    for g in range(G_C):
        bg = bm[:, g * N_C:(g + 1) * N_C].astype(BF16)
        cg = cm[:, g * N_C:(g + 1) * N_C].astype(BF16)
        cbm = lax.dot_general(cg, bg, (((1,), (1,)), ((), ())), preferred_element_type=F32)
        dtxw = []
        for hh in range(r):
            h = g * r + hh
            cc = cum_c[:, h:h + 1]
            cr = cum_r[h:h + 1, :]
            decay = jnp.exp(jnp.where(causal, cc - cr, -jnp.inf))
            dtx = dt_c[:, h:h + 1] * x[:, h * P_C:(h + 1) * P_C]
            y = jnp.dot((cbm * decay).astype(BF16), dtx.astype(BF16), preferred_element_type=F32)
            hs = h_ref[h].astype(BF16)
            y = y + lax.dot_general(cg, hs, (((1,), (1,)), ((), ())), preferred_element_type=F32) * jnp.exp(cc)
            ys.append(y + ds_ref[0:1, h:h + 1] * x[:, h * P_C:(h + 1) * P_C])
            cl = cum_c[L - 1:L, h:h + 1]
            dtxw.append(dtx * jnp.exp(cl - cc))
        dtxw = jnp.concatenate(dtxw, axis=-1)
        upd = jnp.dot(dtxw.T.astype(BF16), bg, preferred_element_type=F32)
        for hh in range(r):
            h = g * r + hh
            cl = cum_c[L - 1:L, h:h + 1]
            h_ref[h] = h_ref[h] * jnp.exp(cl) + upd[hh * P_C:(hh + 1) * P_C, :]
    y = jnp.concatenate(ys, axis=-1)
    z = z_ref[0]
    y = y * (z * jax.nn.sigmoid(z))
    gw = D_INNER_C // G_C
    outs = []
    for g in range(G_C):
        yg = y[:, g * gw:(g + 1) * gw]
        outs.append(yg * lax.rsqrt(jnp.mean(yg * yg, axis=-1, keepdims=True) + RMS_EPS))
    o_ref[0] = jnp.concatenate(outs, axis=-1) * yn_ref[...]

    @pl.when(ci == pl.num_programs(1) - 1)
    def _():
        h_out[0] = h_ref[...]


def _ssd_prompt(proj, dt, a_neg, conv_w, conv_b, d_skip, y_norm):
    b, t, _ = proj.shape
    L = SSD_CHUNK
    nc = t // L
    an = jnp.concatenate([jnp.broadcast_to(a_neg[None, :], (H_C, H_C)), a_neg[:, None]], axis=1)
    an = jnp.pad(an, ((0, 0), (0, LANE - an.shape[1])))
    ds = jnp.pad(d_skip[None, :], ((0, 0), (0, LANE - H_C)))
    return pl.pallas_call(
        _ssd_body,
        grid=(b, nc),
        in_specs=[pl.BlockSpec((1, L, D_INNER_C), lambda i, c: (i, c, 0)),
                  pl.BlockSpec((1, L, D_INNER_C), lambda i, c: (i, c, 1)),
                  pl.BlockSpec((1, L, 2 * G_C * N_C), lambda i, c: (i, c, 2)),
                  pl.BlockSpec((1, L, H_C), lambda i, c: (i, c, 0)),
                  pl.BlockSpec((1, H_C, L), lambda i, c: (i, 0, c)),
                  pl.BlockSpec((H_C, LANE), lambda i, c: (0, 0)),
                  pl.BlockSpec((CONV_W, CONV_DIM_C), lambda i, c: (0, 0)),
                  pl.BlockSpec((1, CONV_DIM_C), lambda i, c: (0, 0)),
                  pl.BlockSpec((1, LANE), lambda i, c: (0, 0)),
                  pl.BlockSpec((1, D_INNER_C), lambda i, c: (0, 0))],
        out_specs=[pl.BlockSpec((1, L, D_INNER_C), lambda i, c: (i, c, 0)),
                   pl.BlockSpec((1, H_C, P_C, N_C), lambda i, c: (i, 0, 0, 0))],
        out_shape=[jax.ShapeDtypeStruct((b, t, D_INNER_C), F32),
                   jax.ShapeDtypeStruct((b, H_C, P_C, N_C), F32)],
        scratch_shapes=[pltpu.VMEM((CONV_PAD + SSD_CHUNK, CONV_DIM_C), F32), pltpu.VMEM((H_C, P_C, N_C), F32)],
        compiler_params=pltpu.CompilerParams(dimension_semantics=("arbitrary", "arbitrary"), vmem_limit_bytes=VMEM_LIMIT),
        name="ssd_prompt",
    )(proj, proj, proj, dt, dt.transpose(0, 2, 1), an, conv_w, conv_b.reshape(1, CONV_DIM_C), ds,
      y_norm.reshape(1, D_INNER_C))


def _rms_norm(x, g, eps=RMS_EPS):
    xf = x.astype(F32)
    y = xf * lax.rsqrt(jnp.mean(xf * xf, axis=-1, keepdims=True) + eps)
    return (y * g.astype(F32)).astype(x.dtype)


def _split_cols(x, sizes):
    return jnp.split(x, [int(s) for s in np.cumsum(sizes)[:-1]], axis=-1)


def _last_rows(a, n):
    t = a.shape[1]
    if t >= n:
        return a[:, t - n:]
    pad = [(0, 0)] * a.ndim
    pad[1] = (n - t, 0)
    return jnp.pad(a, pad)


def _softmax_attend(q, k, v, mask, sinks=None):
    s = jnp.einsum('...qgrd,...kgd->...grqk', q, k).astype(F32) * (HEAD_DIM ** -0.5)
    s = jnp.where(mask[..., None, None, :, :], s, -jnp.inf)
    m = jnp.max(s, axis=-1, keepdims=True)
    if sinks is not None:
        sk = sinks.astype(F32)[:, :, None, None]
        m = jnp.maximum(m, sk)
    m = jnp.where(jnp.isfinite(m), m, 0.0)
    p = jnp.exp(s - m)
    den = jnp.sum(p, axis=-1, keepdims=True)
    if sinks is not None:
        den = den + jnp.exp(sk - m)
    p = p / jnp.where(den > 0, den, 1.0)
    out = jnp.einsum('...grqk,...kgd->...qgrd', p.astype(v.dtype), v)
    return out, p


def _window_step_attention(q, k_new, v_new, buf, window, sinks):
    t = k_new.shape[1]
    k = jnp.concatenate([buf[:, :, 0], k_new], axis=1)
    v = jnp.concatenate([buf[:, :, 1], v_new], axis=1)
    q_pos = PAST_LEN + jnp.arange(t)
    k_pos = PAST_LEN - window + jnp.arange(window + t)
    diff = q_pos[:, None] - k_pos[None, :]
    mask = (k_pos[None, :] >= 0) & (diff >= 0) & (diff < window)
    out, _ = _softmax_attend(q, k, v, mask, sinks)
    new_buf = jnp.concatenate([buf, jnp.stack([k_new, v_new], axis=2)], axis=1)[:, t:]
    return out, new_buf


def _mlstm_chunk(state, inputs):
    c, n, m = state
    q, k, v, ig, lf = inputs
    L = q.shape[1]
    f_cum = jnp.cumsum(lf, axis=1)
    causal = jnp.tril(jnp.ones((L, L), dtype=bool))
    log_d = jnp.where(causal[None, :, :, None], f_cum[:, :, None, :] - f_cum[:, None, :, :] + ig[:, None, :, :], -jnp.inf)
    m_inter = f_cum + m[:, None, :]
    m_t = jnp.maximum(m_inter, jnp.max(log_d, axis=2))
    d_mat = jnp.exp(log_d - m_t[:, :, None, :])
    w_inter = jnp.exp(m_inter - m_t)
    qk = jnp.einsum('bthd,bshd->btsh', q, k) * d_mat
    num = jnp.einsum('btsh,bshv->bthv', qk, v) + w_inter[..., None] * jnp.einsum('bhvd,bthd->bthv', c, q)
    den = jnp.sum(qk, axis=2) + w_inter * jnp.einsum('bhd,bthd->bth', n, q)
    h = num / jnp.maximum(jnp.abs(den), jnp.exp(-m_t))[..., None]
    m_new = m_t[:, -1]
    w_end = jnp.exp(f_cum[:, -1:] - f_cum + ig - m_new[:, None, :])
    decay = jnp.exp(f_cum[:, -1] + m - m_new)
    c_new = decay[..., None, None] * c + jnp.einsum('bsh,bshv,bshd->bhvd', w_end, v, k)
    n_new = decay[..., None] * n + jnp.einsum('bsh,bshd->bhd', w_end, k)
    return (c_new, n_new, m_new), h


def _ssd_chunk(h, inputs, a_neg):
    x, dt, bm, cm = inputs
    bsz, L = x.shape[:2]
    r = H_C // G_C
    cum = jnp.cumsum(dt * a_neg, axis=1)
    causal = jnp.tril(jnp.ones((L, L), dtype=bool))
    seg = jnp.where(causal[None, :, :, None], cum[:, :, None, :] - cum[:, None, :, :], -jnp.inf)
    decay = jnp.exp(seg).reshape(bsz, L, L, G_C, r)
    dtx = (dt[..., None] * x).reshape(bsz, L, G_C, r, P_C)
    cb = jnp.einsum('btgn,bsgn->btsg', cm, bm)
    y = jnp.einsum('btsg,btsgr,bsgrp->btgrp', cb, decay, dtx)
    hg = h.reshape(bsz, G_C, r, P_C, N_C)
    y = y + jnp.einsum('btgn,bgrpn->btgrp', cm, hg) * jnp.exp(cum).reshape(bsz, L, G_C, r)[..., None]
    w_end = jnp.exp(cum[:, -1:] - cum).reshape(bsz, L, G_C, r)
    h_new = hg * jnp.exp(cum[:, -1]).reshape(bsz, G_C, r)[..., None, None] + jnp.einsum('bsgr,bsgrp,bsgn->bgrpn', w_end, dtx, bm)
    return h_new.reshape(bsz, H_C, P_C, N_C), y.reshape(bsz, L, H_C, P_C)


def _causal_conv(xbc, buf, w, b):
    t = xbc.shape[1]
    xp = jnp.concatenate([buf, xbc], axis=1)
    out = b + sum(xp[:, j:j + t] * w[j] for j in range(CONV_W))
    return jax.nn.silu(out), xp[:, t:]


def _compress(rows, pos, w1, w2):
    bsz, t, g, d = rows.shape
    nb = t // CMP_BLOCK
    blk = rows.reshape(bsz, nb, CMP_BLOCK, g, d) + pos[:, None, :]
    blk = blk.transpose(0, 1, 3, 2, 4).reshape(bsz, nb, g, CMP_BLOCK * d)
    return jax.nn.silu(blk @ w1) @ w2


def _select_blocks(p_cmp, q_pos, n_top):
    bsz, g, r, t, nbc = p_cmp.shape
    ratio = SLC_BLOCK // CMP_BLOCK
    imp = p_cmp.sum(axis=2).reshape(bsz, g, t, nbc // ratio, ratio).sum(axis=-1)
    blk = jnp.arange(nbc // ratio)[None, :]
    cur = (q_pos // SLC_BLOCK)[:, None]
    forced = (blk == 0) | (blk == cur) | (blk == cur - 1)
    score = jnp.where(blk > cur, -jnp.inf, jnp.where(forced, jnp.inf, imp))
    _, sel = lax.top_k(score, n_top)
    return sel.astype(jnp.int32)


def _pad_cols(w, n):
    return jnp.pad(w, ((0, 0), (0, n - w.shape[1])))


def _even_layer(x, w, st, prompt):
    bsz, t, _ = x.shape
    r_a = H_A // KVH_A
    n_in = sum(IN_E_SIZES)
    proj = _norm_matmul(x.reshape(bsz * t, D_MODEL), w['norm_mix'], w['w_in_p'])[:, :n_in].reshape(bsz, t, n_in)
    qa, ka, va, qb, kb, vb, ib, fb, ob = _split_cols(proj, IN_E_SIZES)
    qa = _rms_norm(qa.reshape(bsz, t, KVH_A, r_a, HEAD_DIM), w['q_norm'])
    ka = _rms_norm(ka.reshape(bsz, t, KVH_A, HEAD_DIM), w['k_norm'])
    va = va.reshape(bsz, t, KVH_A, HEAD_DIM)
    sinks = w['sinks'].reshape(KVH_A, r_a)
    if prompt:
        o_a = _window_attend(qa.transpose(0, 2, 3, 1, 4), ka.transpose(0, 2, 1, 3), va.transpose(0, 2, 1, 3),
                             WINDOW_A, w['sinks']).transpose(0, 3, 1, 2, 4)
        new_win = _last_rows(jnp.stack([ka, va], axis=2), WINDOW_A)
    else:
        o_a, new_win = _window_step_attention(qa, ka, va, st['a_win'], WINDOW_A, sinks)
    ig = ib + w['b_igate']
    lf = jax.nn.log_sigmoid(fb + w['b_fgate'])
    if prompt:
        gates = jnp.concatenate([ig.transpose(0, 2, 1), lf.transpose(0, 2, 1)], axis=1)
        hb, c_t, n_rep, m_rep = _mlstm_prompt(qb, kb.transpose(0, 2, 1), vb, gates, ob, w['h_norm'])
        c_new, n_new, m_new = c_t.transpose(0, 1, 3, 2), n_rep[..., 0], m_rep[:, :, 0, 0]
    else:
        qb = qb.reshape(bsz, t, H_B, DK_B)
        kb = kb.reshape(bsz, t, H_B, DK_B) * (DK_B ** -0.5)
        vb = vb.reshape(bsz, t, H_B, DV_B)
        init = (st['b_c'], st['b_n'], st['b_m'])
        (c_new, n_new, m_new), hb = _mlstm_chunk(init, (qb, kb, vb, ig, lf))
        hb = _rms_norm(hb, w['h_norm']) * jax.nn.sigmoid(ob.reshape(bsz, t, H_B, DV_B))
    mix = jnp.concatenate([o_a.reshape(bsz, t, H_A * HEAD_DIM), hb.reshape(bsz, t, H_B * DV_B)], axis=-1)
    xr = _out_proj(mix.reshape(bsz * t, D_MODEL), w['w_out'], x.reshape(bsz * t, D_MODEL))
    xr = _ffn(xr, w['norm_ffn'], w['w_gu'], w['w_down'])
    return xr.reshape(bsz, t, D_MODEL), (new_win, c_new, n_new, m_new)


def _odd_mixers(x, w, st, page_table, prompt):
    bsz, t, _ = x.shape
    r_d = H_D // KVH_D
    n_in = sum(IN_O_SIZES)
    proj_full = _norm_matmul(x.reshape(bsz * t, D_MODEL), w['norm_mix'], w['w_in_p'])
    proj = proj_full[:, :n_in].reshape(bsz, t, n_in)
    zc, xbc, dtc, qd, kvd, gd = _split_cols(proj, IN_O_SIZES)
    dt = jax.nn.softplus(dtc + w['dt_bias'])
    if prompt:
        yc, h_new = _ssd_prompt(proj_full.reshape(bsz, t, -1), dt, -jnp.exp(w['a_log']), w['conv_w'], w['conv_b'],
                                w['d_skip'], w['y_norm'])
        new_conv = xbc[:, t - (CONV_W - 1):]
    else:
        xbc, new_conv = _causal_conv(xbc, st['c_conv'], w['conv_w'], w['conv_b'])
        xc, bc, cc = _split_cols(xbc, (D_INNER_C, G_C * N_C, G_C * N_C))
        xc = xc.reshape(bsz, t, H_C, P_C)
        bc = bc.reshape(bsz, t, G_C, N_C)
        cc = cc.reshape(bsz, t, G_C, N_C)
        h_new, yc = _ssd_chunk(st['c_ssm'], (xc, dt, bc, cc), -jnp.exp(w['a_log']))
        yc = yc + w['d_skip'][:, None] * xc
        yc = yc.reshape(bsz, t, D_INNER_C) * jax.nn.silu(zc)
        yc = _rms_norm(yc.reshape(bsz, t, G_C, D_INNER_C // G_C), w['y_norm'].reshape(G_C, D_INNER_C // G_C))
    qd = _rms_norm(qd.reshape(bsz, t, KVH_D, r_d, HEAD_DIM), w['q_norm'])
    kc, vc, ks, vs, kw, vw = [a.reshape(bsz, t, KVH_D, HEAD_DIM) for a in _split_cols(kvd, (KVH_D * HEAD_DIM,) * 6)]
    ks = _rms_norm(ks, w['k_norm'])
    kw = _rms_norm(kw, w['k_norm'])
    new_kv = jnp.stack([kc, vc, ks, vs], axis=2)
    q_pos = jnp.arange(t) + (0 if prompt else PAST_LEN)
    ks_t, vs_t = ks.transpose(0, 2, 1, 3), vs.transpose(0, 2, 1, 3)

    def comp(rows, j):
        return _compress(rows, w['cmp_pos'][j], w['cmp_w1'][j], w['cmp_w2'][j])
    if prompt:
        kcmp, vcmp = comp(kc, 0), comp(vc, 1)
    else:
        assert t == 1
        pool = jnp.transpose(st['d_kv'], (0, 1, 3, 4, 5, 2)).reshape(
            st['d_kv'].shape[0], st['d_kv'].shape[1], CMP_W + SLC_W, PAGE_SIZE)
        n_pages = page_table.shape[1]
        past = _compress_pages(pool, page_table, w['cmp_pos'], w['cmp_w1'], w['cmp_w2'], st['layer'])
        past = past.reshape(bsz, n_pages * BLK_PER_PAGE, 2, KVH_D, HEAD_DIM)
        pad_t = -(-t // SLC_BLOCK) * SLC_BLOCK
        padw = ((0, 0), (0, pad_t - t), (0, 0), (0, 0))
        kcmp = jnp.concatenate([past[:, :, 0], comp(jnp.pad(kc, padw), 0)], axis=1)
        vcmp = jnp.concatenate([past[:, :, 1], comp(jnp.pad(vc, padw), 1)], axis=1)
    kcmp = _rms_norm(kcmp, w['k_norm'])
    n_top = min(TOP_N, kcmp.shape[1] // (SLC_BLOCK // CMP_BLOCK))
    if prompt:
        qh = qd.transpose(0, 2, 3, 1, 4)
        o_cmp, sel = _cmp_select(qh, kcmp.transpose(0, 2, 1, 3), vcmp.transpose(0, 2, 1, 3), n_top)
        o_slc = _slc_attend(qh, sel, ks_t, vs_t).transpose(0, 3, 1, 2, 4)
        o_cmp = o_cmp.transpose(0, 3, 1, 2, 4)
        o_win = _window_attend(qh, kw.transpose(0, 2, 1, 3), vw.transpose(0, 2, 1, 3), WINDOW_D).transpose(0, 3, 1, 2, 4)
        new_win = _last_rows(jnp.stack([kw, vw], axis=2), WINDOW_D)
    else:
        blk_end = jnp.arange(kcmp.shape[1]) * CMP_BLOCK + (CMP_BLOCK - 1)
        o_cmp, p_cmp = _softmax_attend(qd, kcmp, vcmp, blk_end[None, :] <= q_pos[:, None])
        sel = _select_blocks(p_cmp, q_pos, n_top)[:, :, 0]
        q_rows = jnp.pad(qd[:, 0], ((0, 0), (0, 0), (0, 8 - r_d), (0, 0)))
        o8 = _slc_step(pool, page_table, sel, q_rows, ks_t, vs_t, st['layer'])
        o_slc = o8[:, None, :, :r_d]
        o_win, new_win = _window_step_attention(qd, kw, vw, st['d_win'], WINDOW_D, None)
    g = jax.nn.sigmoid(gd).reshape(bsz, t, KVH_D, r_d, 3)
    o_d = g[..., 0:1] * o_cmp + g[..., 1:2] * o_slc + g[..., 2:3] * o_win
    mix = jnp.concatenate([yc.reshape(bsz, t, D_INNER_C), o_d.reshape(bsz, t, H_D * HEAD_DIM)], axis=-1)
    return mix.reshape(bsz * t, D_MODEL), (h_new, new_conv, new_kv, new_win)


def kernel(x_prompt, x_sample, cache_a_win, state_b_c, state_b_n, state_b_m, state_c_ssm, state_c_conv, cache_d_kv, cache_d_win, page_table, e_norm_mix, e_w_in, e_q_norm, e_k_norm, e_sinks, e_b_igate, e_b_fgate, e_h_norm, e_w_out, e_norm_ffn, e_w_gu, e_w_down, o_norm_mix, o_w_in, o_conv_w, o_conv_b, o_dt_bias, o_a_log, o_d_skip, o_y_norm, o_q_norm, o_k_norm, o_cmp_pos, o_cmp_w1, o_cmp_w2, o_w_out, o_norm_ffn, o_router, o_w_gu, o_w_down):
    i = 0
    we = {'norm_mix': e_norm_mix[i], 'w_in_p': _pad_cols(e_w_in[i], _round_up(sum(IN_E_SIZES), LANE)).astype(BF16),
          'q_norm': e_q_norm[i], 'k_norm': e_k_norm[i],
          'sinks': e_sinks[i], 'b_igate': e_b_igate[i], 'b_fgate': e_b_fgate[i], 'h_norm': e_h_norm[i],
          'w_out': e_w_out[i].astype(BF16), 'norm_ffn': e_norm_ffn[i], 'w_gu': e_w_gu[i].astype(BF16),
          'w_down': e_w_down[i].astype(BF16)}
    wo = {'norm_mix': o_norm_mix[i], 'w_in_p': _pad_cols(o_w_in[i], _round_up(sum(IN_O_SIZES), LANE)).astype(BF16),
          'conv_w': o_conv_w[i], 'conv_b': o_conv_b[i],
          'dt_bias': o_dt_bias[i], 'a_log': o_a_log[i], 'd_skip': o_d_skip[i], 'y_norm': o_y_norm[i],
          'q_norm': o_q_norm[i], 'k_norm': o_k_norm[i], 'cmp_pos': o_cmp_pos[i], 'cmp_w1': o_cmp_w1[i],
          'cmp_w2': o_cmp_w2[i], 'w_out': o_w_out[i].astype(BF16), 'norm_ffn': o_norm_ffn[i],
          'router': _pad_cols(o_router[i], LANE), 'w_gu': o_w_gu[i].astype(BF16), 'w_down': o_w_down[i].astype(BF16)}
    st_e = {'a_win': cache_a_win[i], 'b_c': state_b_c[i], 'b_n': state_b_n[i], 'b_m': state_b_m[i]}
    st_o = {'c_ssm': state_c_ssm[i], 'c_conv': state_c_conv[i], 'd_kv': cache_d_kv, 'layer': i,
            'd_win': cache_d_win[i]}

    xp, sp_e = _even_layer(x_prompt, we, None, True)
    xs, ss_e = _even_layer(x_sample, we, st_e, False)

    mix_p, sp_o = _odd_mixers(xp, wo, None, page_table, True)
    mix_s, ss_o = _odd_mixers(xs, wo, st_o, page_table, False)
    np_rows = mix_p.shape[0]
    xp2, xnp, lgp = _out_proj_route(mix_p, wo['w_out'], xp.reshape(np_rows, D_MODEL), wo['norm_ffn'], wo['router'])
    xs2, xns, lgs = _out_proj_route(mix_s, wo['w_out'], xs.reshape(-1, D_MODEL), wo['norm_ffn'], wo['router'])
    x_all = jnp.concatenate([xp2, xs2], axis=0)
    xn_all = jnp.concatenate([xnp, xns], axis=0)
    lg_all = jnp.concatenate([lgp, lgs], axis=0)[:, :N_EXP]
    y_all = _moe_layer(x_all, xn_all, lg_all, wo['w_gu'], wo['w_down'])
    y_prompt = y_all[:np_rows].reshape(x_prompt.shape)
    y_sample = y_all[np_rows:].reshape(x_sample.shape)

    def one(a):
        return a[None]
    return (y_prompt, y_sample, one(sp_e[0]), one(ss_e[0]), one(sp_e[1]), one(ss_e[1]), one(sp_e[2]), one(ss_e[2]),
            one(sp_e[3]), one(ss_e[3]), one(sp_o[0]), one(ss_o[0]), one(sp_o[1]), one(ss_o[1]),
            one(sp_o[2]), one(ss_o[2]), one(sp_o[3]), one(ss_o[3]))
```

```python
import functools
import math

import numpy as np
import jax
import jax.numpy as jnp
from jax import lax
from jax.experimental import pallas as pl
from jax.experimental.pallas import tpu as pltpu

F32 = jnp.float32
BF16 = jnp.bfloat16

D_MODEL = 1024
PAST_LEN = 16384
PAGE_SIZE = 128
HEAD_DIM = 64
QBLOCK = 128
H_A, KVH_A, WINDOW_A = 8, 2, 128
H_B, DK_B, DV_B, CHUNK_B = 4, 64, 128, 64
P_C, H_C, G_C, N_C, CONV_W, CHUNK_C = 64, 8, 2, 128, 4, 128
D_INNER_C = H_C * P_C
CONV_DIM_C = D_INNER_C + 2 * G_C * N_C
H_D, KVH_D = 8, 2
CMP_BLOCK, SLC_BLOCK, TOP_N, WINDOW_D, SLC_QBLOCK = 32, 64, 16, 512, 32
D_FF, N_EXP, TOP_K = 3584, 8, 2
IN_E_SIZES = (H_A * HEAD_DIM, KVH_A * HEAD_DIM, KVH_A * HEAD_DIM, H_B * DK_B, H_B * DK_B, H_B * DV_B, H_B, H_B, H_B * DV_B)
IN_O_SIZES = (D_INNER_C, CONV_DIM_C, H_C, H_D * HEAD_DIM, 6 * KVH_D * HEAD_DIM, 3 * H_D)

LANE = 128
VMEM_LIMIT = 48 * 1024 * 1024
RMS_EPS = 1e-6
FF_TILE = 1792
ROW_TILE = 512


def _round_up(n, m):
    return -(-n // m) * m


def _rms(x, g):
    return x * lax.rsqrt(jnp.mean(x * x, axis=-1, keepdims=True) + RMS_EPS) * g


def _norm_matmul_body(x_ref, g_ref, w_ref, o_ref):
    xn = _rms(x_ref[...], g_ref[...])
    o_ref[...] = jnp.dot(xn.astype(BF16), w_ref[...], preferred_element_type=F32)


def _norm_matmul(x, gain, w):
    m, k = x.shape
    n = w.shape[1]
    tm = min(ROW_TILE, m)
    return pl.pallas_call(
        _norm_matmul_body,
        grid=(m // tm,),
        in_specs=[pl.BlockSpec((tm, k), lambda i: (i, 0)),
                  pl.BlockSpec((1, k), lambda i: (0, 0)),
                  pl.BlockSpec((k, n), lambda i: (0, 0))],
        out_specs=pl.BlockSpec((tm, n), lambda i: (i, 0)),
        out_shape=jax.ShapeDtypeStruct((m, n), F32),
        compiler_params=pltpu.CompilerParams(dimension_semantics=("arbitrary",), vmem_limit_bytes=VMEM_LIMIT),
        name="norm_in_proj",
    )(x, gain.reshape(1, k), w)


def _mix_matmul(a_ref, b_ref, w_ref):
    ka = a_ref.shape[1]
    return (jnp.dot(a_ref[...].astype(BF16), w_ref[0:ka, :], preferred_element_type=F32)
            + jnp.dot(b_ref[...].astype(BF16), w_ref[ka:, :], preferred_element_type=F32))


def _out_proj_body(a_ref, b_ref, w_ref, res_ref, o_ref):
    o_ref[...] = res_ref[...] + _mix_matmul(a_ref, b_ref, w_ref)


def _out_proj(a, b, w, res):
    m, ka = a.shape
    kb = b.shape[1]
    n = w.shape[1]
    tm = min(ROW_TILE, m)
    return pl.pallas_call(
        _out_proj_body,
        grid=(m // tm,),
        in_specs=[pl.BlockSpec((tm, ka), lambda i: (i, 0)),
                  pl.BlockSpec((tm, kb), lambda i: (i, 0)),
                  pl.BlockSpec((ka + kb, n), lambda i: (0, 0)),
                  pl.BlockSpec((tm, n), lambda i: (i, 0))],
        out_specs=pl.BlockSpec((tm, n), lambda i: (i, 0)),
        out_shape=jax.ShapeDtypeStruct((m, n), F32),
        compiler_params=pltpu.CompilerParams(dimension_semantics=("arbitrary",), vmem_limit_bytes=VMEM_LIMIT),
        name="out_proj",
    )(a, b, w, res)


def _out_proj_route_body(a_ref, b_ref, w_ref, res_ref, g_ref, r_ref, o_ref, xn_ref, lg_ref):
    x = res_ref[...] + _mix_matmul(a_ref, b_ref, w_ref)
    o_ref[...] = x
    xn = _rms(x, g_ref[...])
    xn_ref[...] = xn.astype(BF16)
    lg_ref[...] = jnp.dot(xn, r_ref[...], preferred_element_type=F32, precision=lax.Precision.HIGHEST)


def _out_proj_route(a, b, w, res, gain, router_pad):
    m, ka = a.shape
    kb = b.shape[1]
    n = w.shape[1]
    tm = min(ROW_TILE, m)
    return pl.pallas_call(
        _out_proj_route_body,
        grid=(m // tm,),
        in_specs=[pl.BlockSpec((tm, ka), lambda i: (i, 0)),
                  pl.BlockSpec((tm, kb), lambda i: (i, 0)),
                  pl.BlockSpec((ka + kb, n), lambda i: (0, 0)),
                  pl.BlockSpec((tm, n), lambda i: (i, 0)),
                  pl.BlockSpec((1, n), lambda i: (0, 0)),
                  pl.BlockSpec((n, LANE), lambda i: (0, 0))],
        out_specs=[pl.BlockSpec((tm, n), lambda i: (i, 0)),
                   pl.BlockSpec((tm, n), lambda i: (i, 0)),
                   pl.BlockSpec((tm, LANE), lambda i: (i, 0))],
        out_shape=[jax.ShapeDtypeStruct((m, n), F32),
                   jax.ShapeDtypeStruct((m, n), BF16),
                   jax.ShapeDtypeStruct((m, LANE), F32)],
        compiler_params=pltpu.CompilerParams(dimension_semantics=("arbitrary",), vmem_limit_bytes=VMEM_LIMIT),
        name="out_proj_route",
    )(a, b, w, res, gain.reshape(1, n), router_pad)


def _ffn_body(x_ref, g_ref, wg_ref, wu_ref, wd_ref, o_ref, xn_ref, acc_ref):
    c = pl.program_id(1)

    @pl.when(c == 0)
    def _():
        xn_ref[...] = _rms(x_ref[...], g_ref[...]).astype(BF16)
        acc_ref[...] = jnp.zeros_like(acc_ref)

    xn = xn_ref[...]
    gate = jnp.dot(xn, wg_ref[...], preferred_element_type=F32)
    up = jnp.dot(xn, wu_ref[...], preferred_element_type=F32)
    h = (gate * jax.nn.sigmoid(gate) * up).astype(BF16)
    acc_ref[...] += jnp.dot(h, wd_ref[...], preferred_element_type=F32)

    @pl.when(c == pl.num_programs(1) - 1)
    def _():
        o_ref[...] = x_ref[...] + acc_ref[...]


def _ffn(x, gain, w_gu, w_down):
    m, d = x.shape
    tm = min(ROW_TILE, m)
    nc = D_FF // FF_TILE
    return pl.pallas_call(
        _ffn_body,
        grid=(m // tm, nc),
        in_specs=[pl.BlockSpec((tm, d), lambda i, c: (i, 0)),
                  pl.BlockSpec((1, d), lambda i, c: (0, 0)),
                  pl.BlockSpec((d, FF_TILE), lambda i, c: (0, c)),
                  pl.BlockSpec((d, FF_TILE), lambda i, c: (0, c + nc)),
                  pl.BlockSpec((FF_TILE, d), lambda i, c: (c, 0))],
        out_specs=pl.BlockSpec((tm, d), lambda i, c: (i, 0)),
        out_shape=jax.ShapeDtypeStruct((m, d), F32),
        scratch_shapes=[pltpu.VMEM((tm, d), BF16), pltpu.VMEM((tm, d), F32)],
        compiler_params=pltpu.CompilerParams(dimension_semantics=("arbitrary", "arbitrary"), vmem_limit_bytes=VMEM_LIMIT),
        name="ffn_dense",
    )(x, gain.reshape(1, d), w_gu, w_gu, w_down)


def _moe_body(te_ref, tv_ref, x_ref, s_ref, wg_ref, wu_ref, wd_ref, o_ref, acc_ref):
    i = pl.program_id(0)
    c = pl.program_id(1)
    valid = tv_ref[i] > 0

    @pl.when(c == 0)
    def _():
        acc_ref[...] = jnp.zeros_like(acc_ref)

    @pl.when(valid)
    def _():
        x = x_ref[...]
        gate = jnp.dot(x, wg_ref[0], preferred_element_type=F32)
        up = jnp.dot(x, wu_ref[0], preferred_element_type=F32)
        h = (gate * jax.nn.sigmoid(gate) * up).astype(BF16)
        acc_ref[...] += jnp.dot(h, wd_ref[0], preferred_element_type=F32)

    @pl.when(c == pl.num_programs(1) - 1)
    def _():
        o_ref[...] = s_ref[...] * acc_ref[...]


def _moe_grouped(xs, scale, tile_expert, tile_valid, w_gu, w_down, tm):
    p, d = xs.shape
    nc = D_FF // FF_TILE
    grid_spec = pltpu.PrefetchScalarGridSpec(
        num_scalar_prefetch=2,
        grid=(p // tm, nc),
        in_specs=[pl.BlockSpec((tm, d), lambda i, c, te, tv: (i, 0)),
                  pl.BlockSpec((tm, 1), lambda i, c, te, tv: (i, 0)),
                  pl.BlockSpec((1, d, FF_TILE), lambda i, c, te, tv: (te[i], 0, c)),
                  pl.BlockSpec((1, d, FF_TILE), lambda i, c, te, tv: (te[i], 0, c + nc)),
                  pl.BlockSpec((1, FF_TILE, d), lambda i, c, te, tv: (te[i], c, 0))],
        out_specs=pl.BlockSpec((tm, d), lambda i, c, te, tv: (i, 0)),
        scratch_shapes=[pltpu.VMEM((tm, d), F32)],
    )
    return pl.pallas_call(
        _moe_body,
        grid_spec=grid_spec,
        out_shape=jax.ShapeDtypeStruct((p, d), F32),
        compiler_params=pltpu.CompilerParams(dimension_semantics=("arbitrary", "arbitrary"), vmem_limit_bytes=VMEM_LIMIT),
        name="moe_grouped",
    )(tile_expert, tile_valid, xs, scale, w_gu, w_gu, w_down)


def _moe_route(logits, tm):
    n = logits.shape[0]
    top_v, top_i = lax.top_k(logits, TOP_K)
    gate = jax.nn.softmax(top_v, axis=-1)
    flat_e = top_i.reshape(-1).astype(jnp.int32)
    na = n * TOP_K
    n_tiles = na // tm + N_EXP
    p = n_tiles * tm
    n_pad = p - na
    counts = jnp.sum(jax.nn.one_hot(flat_e, N_EXP, dtype=jnp.int32), axis=0)
    tiles_per_e = (counts + tm - 1) // tm
    cum_pad = jnp.cumsum(tiles_per_e * tm - counts)
    d = jnp.arange(n_pad, dtype=jnp.int32)
    pad_e = jnp.sum((d[:, None] >= cum_pad[None, :]).astype(jnp.int32), axis=1)
    keys = jnp.concatenate([2 * flat_e, 2 * pad_e + 1])
    ids = jnp.arange(p, dtype=jnp.int32)
    gates_all = jnp.concatenate([gate.reshape(-1), jnp.zeros((n_pad,), F32)])
    _, id_sorted, gate_of_pos = lax.sort((keys, ids, gates_all), num_keys=1, is_stable=True)
    tok_of_pos = jnp.where(id_sorted < na, id_sorted // TOP_K, 0)
    _, pos_by_id = lax.sort((id_sorted, ids), num_keys=1)
    pos_of_a = pos_by_id[:na].reshape(n, TOP_K)
    tile_end = jnp.cumsum(tiles_per_e)
    total_tiles = tile_end[-1]
    tile_ids = jnp.arange(n_tiles, dtype=jnp.int32)
    tile_expert = jnp.searchsorted(tile_end, jnp.minimum(tile_ids, total_tiles - 1), side="right").astype(jnp.int32)
    tile_expert = jnp.clip(tile_expert, 0, N_EXP - 1)
    tile_valid = (tile_ids < total_tiles).astype(jnp.int32)
    return tok_of_pos, gate_of_pos, pos_of_a, tile_expert, tile_valid


def _moe_layer(x_parts, xn_rows, logits, w_gu, w_down, tm=ROW_TILE):
    tok_of_pos, gate_of_pos, pos_of_a, tile_expert, tile_valid = _moe_route(logits, tm)
    xs = xn_rows[tok_of_pos]
    ys = _moe_grouped(xs, gate_of_pos.reshape(-1, 1), tile_expert, tile_valid, w_gu, w_down, tm)
    outs, r0 = [], 0
    for xp in x_parts:
        pa = pos_of_a[r0:r0 + xp.shape[0]]
        outs.append(xp + (ys[pa[:, 0]] + ys[pa[:, 1]]))
        r0 += xp.shape[0]
    return outs


QT = 128
KT = 512
NEG = -1e30
RANK_GROUP = 16
LOG2E = 1.4426950408889634


def _heads_to_rows(qt):
    r = qt.shape[1] // HEAD_DIM
    return jnp.concatenate([qt[:, h * HEAD_DIM:(h + 1) * HEAD_DIM] for h in range(r)], axis=0)


def _rows_to_heads(o, r):
    return jnp.concatenate([o[h * QT:(h + 1) * QT] for h in range(r)], axis=-1)


def _cmp_select_body(q_ref, kc_ref, vc_ref, o_ref, sel_ref, rank_ref, *, n_top):
    qi = pl.program_id(2)
    r = q_ref.shape[2] // HEAD_DIM
    q = _heads_to_rows(q_ref[0])
    s = lax.dot_general(q.astype(BF16), kc_ref[0, 0].astype(BF16), (((1,), (1,)), ((), ())),
                        preferred_element_type=F32) * (HEAD_DIM ** -0.5)
    row = lax.broadcasted_iota(jnp.int32, (r * QT, LANE), 0)
    lane = lax.broadcasted_iota(jnp.int32, (r * QT, LANE), 1)
    t_pos = qi * QT + (row & (QT - 1))
    s = jnp.where(lane * CMP_BLOCK + (CMP_BLOCK - 1) <= t_pos, s, -jnp.inf)
    m = jnp.max(s, axis=-1, keepdims=True)
    m = jnp.where(m > -jnp.inf, m, 0.0)
    p = jnp.exp(s - m)
    den = jnp.sum(p, axis=-1, keepdims=True)
    p = p / jnp.where(den > 0, den, 1.0)
    o = jnp.dot(p.astype(BF16), vc_ref[0, 0].astype(BF16), preferred_element_type=F32)
    o_ref[0] = _rows_to_heads(o, r)
    pt = p[0:QT]
    for h in range(1, r):
        pt = pt + p[h * QT:(h + 1) * QT]
    lane1 = lax.broadcasted_iota(jnp.int32, (QT, LANE), 1)
    row1 = lax.broadcasted_iota(jnp.int32, (QT, LANE), 0)
    even = (lane1 & 1) == 0
    lo = jnp.where(even, pt, pltpu.roll(pt, 1, 1))
    hi = jnp.where(even, pltpu.roll(pt, LANE - 1, 1), pt)
    imp = lo + hi
    blk = lane1 >> 1
    cur = (qi * QT + row1) >> 6
    forced = (blk == 0) | (blk == cur) | (blk == cur - 1)
    score = jnp.where(blk > cur, -jnp.inf, jnp.where(forced, jnp.inf, imp))
    lane_f = lane1.astype(F32)
    rank_ref[...] = jnp.zeros((QT, LANE), F32)
    for grp in range(0, LANE // 2, RANK_GROUP):
        @pl.when(2 * qi + 1 >= grp)
        def _():
            cnt = rank_ref[...]
            for j in range(grp, grp + RANK_GROUP):
                col = jnp.broadcast_to(score[:, 2 * j:2 * j + 1], (QT, LANE))
                later = jnp.where(lane_f >= 2.0 * j + 2.0, 1.0, 0.0)
                cnt = cnt + jnp.where(col > score, 1.0, 0.0) + jnp.where(col == score, later, 0.0)
            rank_ref[...] = cnt
    sel_ref[0, 0] = jnp.where((rank_ref[...] < n_top) & (blk <= cur), 1.0, 0.0)


def _cmp_select(q, kcmp, vcmp, n_top):
    b, t, w = q.shape
    g, d = kcmp.shape[1], kcmp.shape[3]
    gw = w // g
    assert kcmp.shape == (b, g, LANE, d) and t // CMP_BLOCK == LANE
    return pl.pallas_call(
        functools.partial(_cmp_select_body, n_top=n_top),
        grid=(b, g, t // QT),
        in_specs=[pl.BlockSpec((1, QT, gw), lambda i, j, k: (i, k, j)),
                  pl.BlockSpec((1, 1, LANE, d), lambda i, j, k: (i, j, 0, 0)),
                  pl.BlockSpec((1, 1, LANE, d), lambda i, j, k: (i, j, 0, 0))],
        out_specs=[pl.BlockSpec((1, QT, gw), lambda i, j, k: (i, k, j)),
                   pl.BlockSpec((1, 1, QT, LANE), lambda i, j, k: (i, j, k, 0))],
        out_shape=[jax.ShapeDtypeStruct((b, t, w), F32),
                   jax.ShapeDtypeStruct((b, g, t, LANE), F32)],
        scratch_shapes=[pltpu.VMEM((QT, LANE), F32)],
        compiler_params=pltpu.CompilerParams(dimension_semantics=("arbitrary",) * 3, vmem_limit_bytes=VMEM_LIMIT),
        name="nsa_cmp_select",
    )(q, kcmp, vcmp)


def _slc_attend_body(q_ref, sel_ref, k_ref, v_ref, o_ref):
    qi = pl.program_id(2)
    r = q_ref.shape[2] // HEAD_DIM
    rows = r * QT
    q = (_heads_to_rows(q_ref[0]) * (HEAD_DIM ** -0.5 * LOG2E)).astype(BF16)
    sel = sel_ref[0, 0]
    e_row = lax.broadcasted_iota(jnp.int32, (LANE, KT), 0)
    e_col = lax.broadcasted_iota(jnp.int32, (LANE, KT), 1)
    expand = jnp.where(e_row == 2 * (e_col >> 6), 1.0, 0.0).astype(BF16)
    t_pos = qi * QT + lax.broadcasted_iota(jnp.int32, (QT, KT), 0)
    k_off = lax.broadcasted_iota(jnp.int32, (QT, KT), 1)
    n_kt = (qi * QT + QT - 1) // KT + 1
    lanes_per_kt = 2 * (KT // SLC_BLOCK)
    ones = jnp.ones((KT, LANE - HEAD_DIM), BF16)

    def body(kt, carry):
        m, acc = carry
        start = pl.multiple_of(kt * KT, KT)
        k = k_ref[0, 0, pl.ds(start, KT), :].astype(BF16)
        v = v_ref[0, 0, pl.ds(start, KT), :].astype(BF16)
        v1 = jnp.concatenate([v, ones], axis=-1)
        s = lax.dot_general(q, k, (((1,), (1,)), ((), ())), preferred_element_type=F32)
        sel_kt = pltpu.roll(sel, (LANE - kt * lanes_per_kt) % LANE, 1).astype(BF16)
        picked = jnp.dot(sel_kt, expand, preferred_element_type=F32)
        bias = jnp.where((picked > 0.5) & (start + k_off <= t_pos), 0.0, NEG)
        s = s + jnp.concatenate([bias] * r, axis=0)
        m_new = jnp.maximum(m, jnp.max(s, axis=-1, keepdims=True))
        alpha = jnp.exp2(m - m_new)
        p = jnp.exp2(s - m_new).astype(BF16)
        acc = alpha * acc + jnp.dot(p, v1, preferred_element_type=F32)
        return m_new, acc

    init = (jnp.full((rows, 1), NEG, F32), jnp.zeros((rows, LANE), F32))
    m, acc = lax.fori_loop(0, n_kt, body, init)
    o_ref[0] = _rows_to_heads(acc[:, :HEAD_DIM] / acc[:, HEAD_DIM:HEAD_DIM + 1], r)


def _slc_attend(q, sel, ks, vs):
    b, t, w = q.shape
    g, d = ks.shape[1], ks.shape[3]
    gw = w // g
    return pl.pallas_call(
        _slc_attend_body,
        grid=(b, g, t // QT),
        in_specs=[pl.BlockSpec((1, QT, gw), lambda i, j, k: (i, k, j)),
                  pl.BlockSpec((1, 1, QT, LANE), lambda i, j, k: (i, j, k, 0)),
                  pl.BlockSpec((1, 1, t, d), lambda i, j, k: (i, j, 0, 0)),
                  pl.BlockSpec((1, 1, t, d), lambda i, j, k: (i, j, 0, 0))],
        out_specs=pl.BlockSpec((1, QT, gw), lambda i, j, k: (i, k, j)),
        out_shape=jax.ShapeDtypeStruct((b, t, w), F32),
        compiler_params=pltpu.CompilerParams(dimension_semantics=("arbitrary",) * 3, vmem_limit_bytes=VMEM_LIMIT),
        name="nsa_slc_attend",
    )(q, sel, ks, vs)


CMP_HIDDEN = 128
CMP_W = 2 * KVH_D * HEAD_DIM
SLC_W = 2 * KVH_D * HEAD_DIM
BLK_PER_PAGE = PAGE_SIZE // CMP_BLOCK
PAGES_PER_STEP = 64


def _page_copy(pool_ref, buf_ref, sem_ref, layer, page, slot, j):
    return pltpu.make_async_copy(pool_ref.at[layer, page, pl.ds(0, CMP_W), :], buf_ref.at[slot, j], sem_ref.at[slot])


def _compress_pages_body(pt_ref, pool_ref, post_ref, perm_ref, w1_ref, w2_ref, o_ref, buf_ref, xk_ref, xv_ref, sem_ref, *,
                         layer, n_steps):
    x_ref = (xk_ref, xv_ref)
    s = pl.program_id(0)
    slot = s % 2

    def start(step, slot_):
        for j in range(PAGES_PER_STEP):
            _page_copy(pool_ref, buf_ref, sem_ref, layer, pt_ref[step * PAGES_PER_STEP + j], slot_, j).start()

    @pl.when(s == 0)
    def _():
        start(0, 0)

    @pl.when(s + 1 < n_steps)
    def _():
        start(s + 1, 1 - slot)

    for j in range(PAGES_PER_STEP):
        _page_copy(pool_ref, buf_ref, sem_ref, layer, 0, slot, j).wait()

    for pp in range(PAGES_PER_STEP // 2):
        for br in range(2):
            t0 = buf_ref[slot, 2 * pp, br * LANE:(br + 1) * LANE, :] + post_ref[br]
            t1 = buf_ref[slot, 2 * pp + 1, br * LANE:(br + 1) * LANE, :] + post_ref[br]
            tc = jnp.concatenate([t0, t1], axis=-1).astype(BF16)
            tp = jnp.dot(tc, perm_ref[...], preferred_element_type=F32)
            x_ref[br][pp] = tp.T

    m = PAGES_PER_STEP * BLK_PER_PAGE
    for br in range(2):
        acc = jnp.zeros((m, KVH_D * CMP_HIDDEN), F32)
        for r in range(0, CMP_BLOCK, 2):
            x = jnp.concatenate([x_ref[br][:, rr * 8:(rr + 1) * 8, :].reshape(m, LANE) for rr in (r, r + 1)], axis=-1)
            acc = acc + jnp.dot(x.astype(BF16), w1_ref[br, r // 2], preferred_element_type=F32)
        h = acc * jax.nn.sigmoid(acc)
        o_ref[:, br * LANE:(br + 1) * LANE] = jnp.dot(h.astype(BF16), w2_ref[br], preferred_element_type=F32)


def _block_diag2(w):
    z = jnp.zeros_like(w)
    return jnp.concatenate([jnp.concatenate([w, z], axis=-1), jnp.concatenate([z, w], axis=-1)], axis=-2)


def _compress_pages(pool, page_table, cmp_pos, cmp_w1, cmp_w2, layer):
    bsz, n_pages = page_table.shape
    total = bsz * n_pages
    n_steps = total // PAGES_PER_STEP
    assert total % PAGES_PER_STEP == 0 and KVH_D == 2
    w1 = cmp_w1.reshape(2, CMP_BLOCK, HEAD_DIM, CMP_HIDDEN)
    w1_bd = _block_diag2(w1).astype(BF16).reshape(2, CMP_BLOCK // 2, 2 * LANE, KVH_D * CMP_HIDDEN)
    w2_bd = _block_diag2(cmp_w2).astype(BF16)
    pos_t = jnp.tile(jnp.transpose(cmp_pos, (0, 2, 1)), (1, KVH_D, BLK_PER_PAGE))
    lane_in = np.arange(2 * PAGE_SIZE)
    page2, blk, r = lane_in // PAGE_SIZE, (lane_in % PAGE_SIZE) // CMP_BLOCK, lane_in % CMP_BLOCK
    perm = np.zeros((2 * PAGE_SIZE, 2 * PAGE_SIZE), np.float32)
    perm[lane_in, r * (2 * BLK_PER_PAGE) + page2 * BLK_PER_PAGE + blk] = 1.0
    perm = jnp.asarray(perm, BF16)
    m = PAGES_PER_STEP * BLK_PER_PAGE
    grid_spec = pltpu.PrefetchScalarGridSpec(
        num_scalar_prefetch=1,
        grid=(n_steps,),
        in_specs=[pl.BlockSpec(memory_space=pl.ANY),
                  pl.BlockSpec((2, LANE, PAGE_SIZE), lambda s, pt: (0, 0, 0)),
                  pl.BlockSpec((2 * PAGE_SIZE, 2 * PAGE_SIZE), lambda s, pt: (0, 0)),
                  pl.BlockSpec((2, CMP_BLOCK // 2, 2 * LANE, KVH_D * CMP_HIDDEN), lambda s, pt: (0, 0, 0, 0)),
                  pl.BlockSpec((2, KVH_D * CMP_HIDDEN, LANE), lambda s, pt: (0, 0, 0))],
        out_specs=pl.BlockSpec((m, CMP_W), lambda s, pt: (s, 0)),
        scratch_shapes=[pltpu.VMEM((2, PAGES_PER_STEP, CMP_W, PAGE_SIZE), F32),
                        pltpu.VMEM((PAGES_PER_STEP // 2, 2 * PAGE_SIZE, LANE), F32),
                        pltpu.VMEM((PAGES_PER_STEP // 2, 2 * PAGE_SIZE, LANE), F32),
                        pltpu.SemaphoreType.DMA((2,))],
    )
    return pl.pallas_call(
        functools.partial(_compress_pages_body, layer=layer, n_steps=n_steps),
        grid_spec=grid_spec,
        out_shape=jax.ShapeDtypeStruct((total * BLK_PER_PAGE, CMP_W), F32),
        compiler_params=pltpu.CompilerParams(dimension_semantics=("arbitrary",), vmem_limit_bytes=VMEM_LIMIT),
        name="nsa_compress_pages",
    )(page_table.reshape(-1), pool, pos_t, perm, w1_bd, w2_bd)


def _slc_copy(pool_ref, buf_ref, sem_ref, layer, page, feat0, slot, idx):
    return pltpu.make_async_copy(pool_ref.at[layer, page, pl.ds(feat0, HEAD_DIM), :], buf_ref.at[slot, idx], sem_ref.at[slot])


def _slc_step_body(pt_ref, sel_ref, pool_ref, q_ref, kn_ref, vn_ref, o_ref, kbuf_ref, vbuf_ref, sem_ref, *,
                   layer, n_seq, n_pages, new_blk):
    b = pl.program_id(0)
    slot = b % 2
    n_sel = KVH_D * TOP_N

    def copies(seq, slot_, from_table):
        out = []
        for g in range(KVH_D):
            for i in range(TOP_N):
                idx = g * TOP_N + i
                page = 0
                if from_table:
                    blk = jnp.minimum(sel_ref[seq * n_sel + idx], new_blk - 1)
                    page = pt_ref[seq * n_pages + (blk >> 1)]
                out.append(_slc_copy(pool_ref, kbuf_ref, sem_ref, layer, page, CMP_W + g * HEAD_DIM, slot_, idx))
                out.append(_slc_copy(pool_ref, vbuf_ref, sem_ref, layer, page, CMP_W + (KVH_D + g) * HEAD_DIM, slot_, idx))
        return out

    @pl.when(b == 0)
    def _():
        for c in copies(0, 0, True):
            c.start()

    @pl.when(b + 1 < n_seq)
    def _():
        for c in copies(b + 1, 1 - slot, True):
            c.start()

    for c in copies(0, slot, False):
        c.wait()

    half = lax.broadcasted_iota(jnp.int32, (8, LANE), 1) >> 6
    for g in range(KVH_D):
        q = q_ref[0, g] * (HEAD_DIM ** -0.5)
        qb = q.astype(BF16)
        s_parts = []
        for i in range(TOP_N):
            blk = sel_ref[b * n_sel + g * TOP_N + i]
            s_i = jnp.dot(qb, kbuf_ref[slot, g * TOP_N + i].astype(BF16), preferred_element_type=F32)
            ok = (half == (blk & 1)) & (blk < new_blk)
            s_parts.append(jnp.where(ok, s_i, NEG))
        s = jnp.concatenate(s_parts, axis=-1)
        s_new = jnp.sum(q * kn_ref[0, g], axis=-1, keepdims=True)
        m = jnp.maximum(jnp.max(s, axis=-1, keepdims=True), s_new)
        p = jnp.exp(s - m)
        p_new = jnp.exp(s_new - m)
        den = jnp.sum(p, axis=-1, keepdims=True) + p_new
        o = p_new * vn_ref[0, g]
        for i in range(TOP_N):
            p_i = p[:, i * LANE:(i + 1) * LANE].astype(BF16)
            o = o + lax.dot_general(p_i, vbuf_ref[slot, g * TOP_N + i].astype(BF16), (((1,), (1,)), ((), ())),
                                    preferred_element_type=F32)
        o_ref[0, g] = o / den


def _slc_step(pool, page_table, sel, q, k_new, v_new, layer):
    bsz, n_pages = page_table.shape
    n_sel = KVH_D * TOP_N
    grid_spec = pltpu.PrefetchScalarGridSpec(
        num_scalar_prefetch=2,
        grid=(bsz,),
        in_specs=[pl.BlockSpec(memory_space=pl.ANY),
                  pl.BlockSpec((1, KVH_D, 8, HEAD_DIM), lambda b, pt, sl: (b, 0, 0, 0)),
                  pl.BlockSpec((1, KVH_D, 1, HEAD_DIM), lambda b, pt, sl: (b, 0, 0, 0)),
                  pl.BlockSpec((1, KVH_D, 1, HEAD_DIM), lambda b, pt, sl: (b, 0, 0, 0))],
        out_specs=pl.BlockSpec((1, KVH_D, 8, HEAD_DIM), lambda b, pt, sl: (b, 0, 0, 0)),
        scratch_shapes=[pltpu.VMEM((2, n_sel, HEAD_DIM, PAGE_SIZE), F32),
                        pltpu.VMEM((2, n_sel, HEAD_DIM, PAGE_SIZE), F32),
                        pltpu.SemaphoreType.DMA((2,))],
    )
    return pl.pallas_call(
        functools.partial(_slc_step_body, layer=layer, n_seq=bsz, n_pages=n_pages, new_blk=2 * n_pages),
        grid_spec=grid_spec,
        out_shape=jax.ShapeDtypeStruct((bsz, KVH_D, 8, HEAD_DIM), F32),
        compiler_params=pltpu.CompilerParams(dimension_semantics=("arbitrary",), vmem_limit_bytes=VMEM_LIMIT),
        name="nsa_slc_step",
    )(page_table.reshape(-1), sel.reshape(-1), pool, q, k_new, v_new)


def _window_attend_body(sink_ref, q_ref, k_ref, v_ref, o_ref, *, window, use_sinks):
    g = pl.program_id(1)
    qi = pl.program_id(2)
    r = q_ref.shape[2] // HEAD_DIM
    rows = r * QT
    span = window + QT
    q = (_heads_to_rows(q_ref[0]) * (HEAD_DIM ** -0.5)).astype(BF16)
    start = pl.multiple_of(jnp.maximum(qi * QT - window, 0), QT)
    k = k_ref[0, 0, pl.ds(start, span), :].astype(BF16)
    v = v_ref[0, 0, pl.ds(start, span), :].astype(BF16)
    s = lax.dot_general(q, k, (((1,), (1,)), ((), ())), preferred_element_type=F32)
    row = lax.broadcasted_iota(jnp.int32, (rows, span), 0)
    diff = qi * QT + (row & (QT - 1)) - (start + lax.broadcasted_iota(jnp.int32, (rows, span), 1))
    s = jnp.where((diff >= 0) & (diff < window), s, -jnp.inf)
    m = jnp.max(s, axis=-1, keepdims=True)
    if use_sinks:
        head = lax.broadcasted_iota(jnp.int32, (rows, 1), 0) // QT
        sink = jnp.zeros((rows, 1), F32)
        for h in range(r):
            sink = jnp.where(head == h, sink_ref[g * r + h], sink)
        m = jnp.maximum(m, sink)
    p = jnp.exp(s - m)
    den = jnp.sum(p, axis=-1, keepdims=True)
    if use_sinks:
        den = den + jnp.exp(sink - m)
    o = jnp.dot(p.astype(BF16), v, preferred_element_type=F32) / den
    o_ref[0] = _rows_to_heads(o, r)


def _window_attend(q, k, v, window, sinks=None):
    b, t, w = q.shape
    g, d = k.shape[1], k.shape[3]
    gw = w // g
    use_sinks = sinks is not None
    if sinks is None:
        sinks = jnp.zeros((w // d,), F32)
    grid_spec = pltpu.PrefetchScalarGridSpec(
        num_scalar_prefetch=1,
        grid=(b, g, t // QT),
        in_specs=[pl.BlockSpec((1, QT, gw), lambda i, j, n, sk: (i, n, j)),
                  pl.BlockSpec((1, 1, t, d), lambda i, j, n, sk: (i, j, 0, 0)),
                  pl.BlockSpec((1, 1, t, d), lambda i, j, n, sk: (i, j, 0, 0))],
        out_specs=pl.BlockSpec((1, QT, gw), lambda i, j, n, sk: (i, n, j)),
    )
    return pl.pallas_call(
        functools.partial(_window_attend_body, window=window, use_sinks=use_sinks),
        grid_spec=grid_spec,
        out_shape=jax.ShapeDtypeStruct((b, t, w), F32),
        compiler_params=pltpu.CompilerParams(dimension_semantics=("arbitrary",) * 3, vmem_limit_bytes=VMEM_LIMIT),
        name="window_attend_%d" % window,
    )(sinks.astype(F32), q, k, v)


MLSTM_CHUNK = 128
HIGHEST = lax.Precision.HIGHEST


def _mlstm_body(q_ref, kt_ref, v_ref, g_ref, ob_ref, hn_ref, o_ref, c_out, n_out, m_out, c_ref, n_ref, m_ref):
    ci = pl.program_id(1)
    L = MLSTM_CHUNK

    @pl.when(ci == 0)
    def _():
        c_ref[...] = jnp.zeros_like(c_ref)
        n_ref[...] = jnp.zeros_like(n_ref)
        m_ref[...] = jnp.zeros_like(m_ref)

    row = lax.broadcasted_iota(jnp.int32, (L, L), 0)
    col = lax.broadcasted_iota(jnp.int32, (L, L), 1)
    causal = col <= row
    upper = jnp.where(row <= col, 1.0, 0.0)
    for h in range(H_B):
        q = q_ref[0, :, h * DK_B:(h + 1) * DK_B]
        kt = kt_ref[0, h * DK_B:(h + 1) * DK_B, :] * (DK_B ** -0.5)
        v = v_ref[0, :, h * DV_B:(h + 1) * DV_B]
        ig = g_ref[0, h:h + 1, :]
        lf = g_ref[0, H_B + h:H_B + h + 1, :]
        m_prev = m_ref[h, 0:1, 0:1]
        f_col = jnp.sum(jnp.where(causal, jnp.broadcast_to(lf, (L, L)), 0.0), axis=-1, keepdims=True)
        f_row = jnp.dot(jnp.broadcast_to(lf, (8, L)), upper, preferred_element_type=F32, precision=HIGHEST)[0:1]
        log_d = jnp.where(causal, f_col - f_row + ig, -jnp.inf)
        m_inter = f_col + m_prev
        m_t = jnp.maximum(m_inter, jnp.max(log_d, axis=-1, keepdims=True))
        d_mat = jnp.exp(log_d - m_t)
        w_inter = jnp.exp(m_inter - m_t)
        qb = q.astype(BF16)
        qk = jnp.dot(qb, kt.astype(BF16), preferred_element_type=F32) * d_mat
        num = jnp.dot(qk.astype(BF16), v.astype(BF16), preferred_element_type=F32)
        num = num + w_inter * jnp.dot(qb, c_ref[h].astype(BF16), preferred_element_type=F32)
        qn = jnp.dot(qb, n_ref[h].astype(BF16), preferred_element_type=F32)[:, 0:1]
        den = jnp.sum(qk, axis=-1, keepdims=True) + w_inter * qn
        hh = num / jnp.maximum(jnp.abs(den), jnp.exp(-m_t))
        hh = hh * lax.rsqrt(jnp.mean(hh * hh, axis=-1, keepdims=True) + RMS_EPS) * hn_ref[...]
        o_ref[0, :, h * DV_B:(h + 1) * DV_B] = hh * jax.nn.sigmoid(ob_ref[0, :, h * DV_B:(h + 1) * DV_B])
        m_new = m_t[L - 1:L, :]
        f_last = f_col[L - 1:L, :]
        w_end = jnp.exp(f_last - f_row + ig - m_new)
        decay = jnp.exp(f_last + m_prev - m_new)
        ktw = kt * w_end
        c_ref[h] = decay * c_ref[h] + jnp.dot(ktw.astype(BF16), v.astype(BF16), preferred_element_type=F32)
        n_ref[h] = decay * n_ref[h] + jnp.sum(ktw, axis=-1, keepdims=True)
        m_ref[h] = jnp.broadcast_to(m_new, m_ref.shape[1:])

    @pl.when(ci == pl.num_programs(1) - 1)
    def _():
        c_out[0] = c_ref[...]
        n_out[0] = n_ref[...]
        m_out[0] = m_ref[...]


def _mlstm_prompt(q, kt, v, gates, ob, h_norm):
    b, t, _ = q.shape
    L = MLSTM_CHUNK
    nc = t // L
    return pl.pallas_call(
        _mlstm_body,
        grid=(b, nc),
        in_specs=[pl.BlockSpec((1, L, H_B * DK_B), lambda i, c: (i, c, 0)),
                  pl.BlockSpec((1, H_B * DK_B, L), lambda i, c: (i, 0, c)),
                  pl.BlockSpec((1, L, H_B * DV_B), lambda i, c: (i, c, 0)),
                  pl.BlockSpec((1, 2 * H_B, L), lambda i, c: (i, 0, c)),
                  pl.BlockSpec((1, L, H_B * DV_B), lambda i, c: (i, c, 0)),
                  pl.BlockSpec((1, DV_B), lambda i, c: (0, 0))],
        out_specs=[pl.BlockSpec((1, L, H_B * DV_B), lambda i, c: (i, c, 0)),
                   pl.BlockSpec((1, H_B, DK_B, DV_B), lambda i, c: (i, 0, 0, 0)),
                   pl.BlockSpec((1, H_B, DK_B, LANE), lambda i, c: (i, 0, 0, 0)),
                   pl.BlockSpec((1, H_B, 8, LANE), lambda i, c: (i, 0, 0, 0))],
        out_shape=[jax.ShapeDtypeStruct((b, t, H_B * DV_B), F32),
                   jax.ShapeDtypeStruct((b, H_B, DK_B, DV_B), F32),
                   jax.ShapeDtypeStruct((b, H_B, DK_B, LANE), F32),
                   jax.ShapeDtypeStruct((b, H_B, 8, LANE), F32)],
        scratch_shapes=[pltpu.VMEM((H_B, DK_B, DV_B), F32), pltpu.VMEM((H_B, DK_B, LANE), F32),
                        pltpu.VMEM((H_B, 8, LANE), F32)],
        compiler_params=pltpu.CompilerParams(dimension_semantics=("arbitrary", "arbitrary"), vmem_limit_bytes=VMEM_LIMIT),
        name="mlstm_prompt",
    )(q, kt, v, gates, ob, h_norm.reshape(1, DV_B))


SSD_CHUNK = 128
CONV_PAD = 8


def _ssd_body(z_ref, xr_ref, bcr_ref, dtc_ref, dtr_ref, an_ref, cw_ref, cb_ref, ds_ref, yn_ref,
              o_ref, h_out, xp_ref, h_ref):
    ci = pl.program_id(1)
    L = SSD_CHUNK
    r = H_C // G_C

    @pl.when(ci == 0)
    def _():
        xp_ref[0:CONV_PAD, :] = jnp.zeros((CONV_PAD, CONV_DIM_C), F32)
        h_ref[...] = jnp.zeros_like(h_ref)

    xp_ref[CONV_PAD:CONV_PAD + L, 0:D_INNER_C] = xr_ref[0]
    xp_ref[CONV_PAD:CONV_PAD + L, D_INNER_C:CONV_DIM_C] = bcr_ref[0]
    conv = cb_ref[...]
    for j in range(CONV_W):
        off = CONV_PAD - (CONV_W - 1) + j
        conv = conv + xp_ref[off:off + L, :] * cw_ref[j:j + 1, :]
    tail = xp_ref[L:L + CONV_PAD, :]
    xp_ref[0:CONV_PAD, :] = tail
    xbc = conv * jax.nn.sigmoid(conv)
    x = xbc[:, 0:D_INNER_C]
    bm = xbc[:, D_INNER_C:D_INNER_C + G_C * N_C]
    cm = xbc[:, D_INNER_C + G_C * N_C:CONV_DIM_C]

    row = lax.broadcasted_iota(jnp.int32, (L, L), 0)
    col = lax.broadcasted_iota(jnp.int32, (L, L), 1)
    causal = col <= row
    lower = jnp.where(causal, 1.0, 0.0)
    upper = jnp.where(row <= col, 1.0, 0.0)
    dt_c = dtc_ref[0]
    dt_r = dtr_ref[0]
    a_c = dt_c * an_ref[0:1, 0:H_C]
    a_r = dt_r * an_ref[:, H_C:H_C + 1]
    cum_c = jnp.dot(lower, a_c, preferred_element_type=F32, precision=HIGHEST)
    cum_r = jnp.dot(a_r, upper, preferred_element_type=F32, precision=HIGHEST)
    ys = []
    for g in range(G_C):
        bg = bm[:, g * N_C:(g + 1) * N_C].astype(BF16)
        cg = cm[:, g * N_C:(g + 1) * N_C].astype(BF16)
        cbm = lax.dot_general(cg, bg, (((1,), (1,)), ((), ())), preferred_element_type=F32)
        dtxw = []
        for hh in range(r):
            h = g * r + hh
            cc = cum_c[:, h:h + 1]
            cr = cum_r[h:h + 1, :]
            decay = jnp.exp(jnp.where(causal, cc - cr, -jnp.inf))
            dtx = dt_c[:, h:h + 1] * x[:, h * P_C:(h + 1) * P_C]
            y = jnp.dot((cbm * decay).astype(BF16), dtx.astype(BF16), preferred_element_type=F32)
            hs = h_ref[h].astype(BF16)
            y = y + lax.dot_general(cg, hs, (((1,), (1,)), ((), ())), preferred_element_type=F32) * jnp.exp(cc)
            ys.append(y + ds_ref[0:1, h:h + 1] * x[:, h * P_C:(h + 1) * P_C])
            cl = cum_c[L - 1:L, h:h + 1]
            dtxw.append(dtx * jnp.exp(cl - cc))
        dtxw = jnp.concatenate(dtxw, axis=-1)
        upd = jnp.dot(dtxw.T.astype(BF16), bg, preferred_element_type=F32)
        for hh in range(r):
            h = g * r + hh
            cl = cum_c[L - 1:L, h:h + 1]
            h_ref[h] = h_ref[h] * jnp.exp(cl) + upd[hh * P_C:(hh + 1) * P_C, :]
    y = jnp.concatenate(ys, axis=-1)
    z = z_ref[0]
    y = y * (z * jax.nn.sigmoid(z))
    gw = D_INNER_C // G_C
    outs = []
    for g in range(G_C):
        yg = y[:, g * gw:(g + 1) * gw]
        outs.append(yg * lax.rsqrt(jnp.mean(yg * yg, axis=-1, keepdims=True) + RMS_EPS))
    o_ref[0] = jnp.concatenate(outs, axis=-1) * yn_ref[...]

    @pl.when(ci == pl.num_programs(1) - 1)
    def _():
        h_out[0] = h_ref[...]


def _ssd_prompt(proj, dt, a_neg, conv_w, conv_b, d_skip, y_norm):
    b, t, _ = proj.shape
    L = SSD_CHUNK
    nc = t // L
    an = jnp.concatenate([jnp.broadcast_to(a_neg[None, :], (H_C, H_C)), a_neg[:, None]], axis=1)
    an = jnp.pad(an, ((0, 0), (0, LANE - an.shape[1])))
    ds = jnp.pad(d_skip[None, :], ((0, 0), (0, LANE - H_C)))
    return pl.pallas_call(
        _ssd_body,
        grid=(b, nc),
        in_specs=[pl.BlockSpec((1, L, D_INNER_C), lambda i, c: (i, c, 0)),
                  pl.BlockSpec((1, L, D_INNER_C), lambda i, c: (i, c, 1)),
                  pl.BlockSpec((1, L, 2 * G_C * N_C), lambda i, c: (i, c, 2)),
                  pl.BlockSpec((1, L, H_C), lambda i, c: (i, c, 0)),
                  pl.BlockSpec((1, H_C, L), lambda i, c: (i, 0, c)),
                  pl.BlockSpec((H_C, LANE), lambda i, c: (0, 0)),
                  pl.BlockSpec((CONV_W, CONV_DIM_C), lambda i, c: (0, 0)),
                  pl.BlockSpec((1, CONV_DIM_C), lambda i, c: (0, 0)),
                  pl.BlockSpec((1, LANE), lambda i, c: (0, 0)),
                  pl.BlockSpec((1, D_INNER_C), lambda i, c: (0, 0))],
        out_specs=[pl.BlockSpec((1, L, D_INNER_C), lambda i, c: (i, c, 0)),
                   pl.BlockSpec((1, H_C, P_C, N_C), lambda i, c: (i, 0, 0, 0))],
        out_shape=[jax.ShapeDtypeStruct((b, t, D_INNER_C), F32),
                   jax.ShapeDtypeStruct((b, H_C, P_C, N_C), F32)],
        scratch_shapes=[pltpu.VMEM((CONV_PAD + SSD_CHUNK, CONV_DIM_C), F32), pltpu.VMEM((H_C, P_C, N_C), F32)],
        compiler_params=pltpu.CompilerParams(dimension_semantics=("arbitrary", "arbitrary"), vmem_limit_bytes=VMEM_LIMIT),
        name="ssd_prompt",
    )(proj, proj, proj, dt, dt.transpose(0, 2, 1), an, conv_w, conv_b.reshape(1, CONV_DIM_C), ds,
      y_norm.reshape(1, D_INNER_C))


def _rms_norm(x, g, eps=RMS_EPS):
    xf = x.astype(F32)
    y = xf * lax.rsqrt(jnp.mean(xf * xf, axis=-1, keepdims=True) + eps)
    return (y * g.astype(F32)).astype(x.dtype)


def _split_cols(x, sizes):
    return jnp.split(x, [int(s) for s in np.cumsum(sizes)[:-1]], axis=-1)


def _last_rows(a, n):
    t = a.shape[1]
    if t >= n:
        return a[:, t - n:]
    pad = [(0, 0)] * a.ndim
    pad[1] = (n - t, 0)
    return jnp.pad(a, pad)


def _softmax_attend(q, k, v, mask, sinks=None):
    s = jnp.einsum('...qgrd,...kgd->...grqk', q, k).astype(F32) * (HEAD_DIM ** -0.5)
    s = jnp.where(mask[..., None, None, :, :], s, -jnp.inf)
    m = jnp.max(s, axis=-1, keepdims=True)
    if sinks is not None:
        sk = sinks.astype(F32)[:, :, None, None]
        m = jnp.maximum(m, sk)
    m = jnp.where(jnp.isfinite(m), m, 0.0)
    p = jnp.exp(s - m)
    den = jnp.sum(p, axis=-1, keepdims=True)
    if sinks is not None:
        den = den + jnp.exp(sk - m)
    p = p / jnp.where(den > 0, den, 1.0)
    out = jnp.einsum('...grqk,...kgd->...qgrd', p.astype(v.dtype), v)
    return out, p


def _window_step_attention(q, k_new, v_new, buf, window, sinks):
    t = k_new.shape[1]
    k = jnp.concatenate([buf[:, :, 0], k_new], axis=1)
    v = jnp.concatenate([buf[:, :, 1], v_new], axis=1)
    q_pos = PAST_LEN + jnp.arange(t)
    k_pos = PAST_LEN - window + jnp.arange(window + t)
    diff = q_pos[:, None] - k_pos[None, :]
    mask = (k_pos[None, :] >= 0) & (diff >= 0) & (diff < window)
    out, _ = _softmax_attend(q, k, v, mask, sinks)
    new_buf = jnp.concatenate([buf, jnp.stack([k_new, v_new], axis=2)], axis=1)[:, t:]
    return out, new_buf


def _mlstm_chunk(state, inputs):
    c, n, m = state
    q, k, v, ig, lf = inputs
    L = q.shape[1]
    f_cum = jnp.cumsum(lf, axis=1)
    causal = jnp.tril(jnp.ones((L, L), dtype=bool))
    log_d = jnp.where(causal[None, :, :, None], f_cum[:, :, None, :] - f_cum[:, None, :, :] + ig[:, None, :, :], -jnp.inf)
    m_inter = f_cum + m[:, None, :]
    m_t = jnp.maximum(m_inter, jnp.max(log_d, axis=2))
    d_mat = jnp.exp(log_d - m_t[:, :, None, :])
    w_inter = jnp.exp(m_inter - m_t)
    qk = jnp.einsum('bthd,bshd->btsh', q, k) * d_mat
    num = jnp.einsum('btsh,bshv->bthv', qk, v) + w_inter[..., None] * jnp.einsum('bhvd,bthd->bthv', c, q)
    den = jnp.sum(qk, axis=2) + w_inter * jnp.einsum('bhd,bthd->bth', n, q)
    h = num / jnp.maximum(jnp.abs(den), jnp.exp(-m_t))[..., None]
    m_new = m_t[:, -1]
    w_end = jnp.exp(f_cum[:, -1:] - f_cum + ig - m_new[:, None, :])
    decay = jnp.exp(f_cum[:, -1] + m - m_new)
    c_new = decay[..., None, None] * c + jnp.einsum('bsh,bshv,bshd->bhvd', w_end, v, k)
    n_new = decay[..., None] * n + jnp.einsum('bsh,bshd->bhd', w_end, k)
    return (c_new, n_new, m_new), h


def _ssd_chunk(h, inputs, a_neg):
    x, dt, bm, cm = inputs
    bsz, L = x.shape[:2]
    r = H_C // G_C
    cum = jnp.cumsum(dt * a_neg, axis=1)
    causal = jnp.tril(jnp.ones((L, L), dtype=bool))
    seg = jnp.where(causal[None, :, :, None], cum[:, :, None, :] - cum[:, None, :, :], -jnp.inf)
    decay = jnp.exp(seg).reshape(bsz, L, L, G_C, r)
    dtx = (dt[..., None] * x).reshape(bsz, L, G_C, r, P_C)
    cb = jnp.einsum('btgn,bsgn->btsg', cm, bm)
    y = jnp.einsum('btsg,btsgr,bsgrp->btgrp', cb, decay, dtx)
    hg = h.reshape(bsz, G_C, r, P_C, N_C)
    y = y + jnp.einsum('btgn,bgrpn->btgrp', cm, hg) * jnp.exp(cum).reshape(bsz, L, G_C, r)[..., None]
    w_end = jnp.exp(cum[:, -1:] - cum).reshape(bsz, L, G_C, r)
    h_new = hg * jnp.exp(cum[:, -1]).reshape(bsz, G_C, r)[..., None, None] + jnp.einsum('bsgr,bsgrp,bsgn->bgrpn', w_end, dtx, bm)
    return h_new.reshape(bsz, H_C, P_C, N_C), y.reshape(bsz, L, H_C, P_C)


def _causal_conv(xbc, buf, w, b):
    t = xbc.shape[1]
    xp = jnp.concatenate([buf, xbc], axis=1)
    out = b + sum(xp[:, j:j + t] * w[j] for j in range(CONV_W))
    return jax.nn.silu(out), xp[:, t:]


def _compress(rows, pos, w1, w2):
    bsz, t, g, d = rows.shape
    nb = t // CMP_BLOCK
    blk = rows.reshape(bsz, nb, CMP_BLOCK, g, d) + pos[:, None, :]
    blk = blk.transpose(0, 1, 3, 2, 4).reshape(bsz, nb, g, CMP_BLOCK * d)
    return jax.nn.silu(blk @ w1) @ w2


def _select_blocks(p_cmp, q_pos, n_top):
    bsz, g, r, t, nbc = p_cmp.shape
    ratio = SLC_BLOCK // CMP_BLOCK
    imp = p_cmp.sum(axis=2).reshape(bsz, g, t, nbc // ratio, ratio).sum(axis=-1)
    blk = jnp.arange(nbc // ratio)[None, :]
    cur = (q_pos // SLC_BLOCK)[:, None]
    forced = (blk == 0) | (blk == cur) | (blk == cur - 1)
    score = jnp.where(blk > cur, -jnp.inf, jnp.where(forced, jnp.inf, imp))
    _, sel = lax.top_k(score, n_top)
    return sel.astype(jnp.int32)


def _pad_cols(w, n):
    return jnp.pad(w, ((0, 0), (0, n - w.shape[1])))


def _even_layer(x, w, st, prompt):
    bsz, t, _ = x.shape
    r_a = H_A // KVH_A
    n_in = sum(IN_E_SIZES)
    proj = _norm_matmul(x.reshape(bsz * t, D_MODEL), w['norm_mix'], w['w_in_p'])[:, :n_in].reshape(bsz, t, n_in)
    qa, ka, va, qb, kb, vb, ib, fb, ob = _split_cols(proj, IN_E_SIZES)
    qa = _rms_norm(qa.reshape(bsz, t, KVH_A, r_a, HEAD_DIM), w['q_norm'])
    ka = _rms_norm(ka.reshape(bsz, t, KVH_A, HEAD_DIM), w['k_norm'])
    va = va.reshape(bsz, t, KVH_A, HEAD_DIM)
    sinks = w['sinks'].reshape(KVH_A, r_a)
    if prompt:
        o_a = _window_attend(qa.reshape(bsz, t, H_A * HEAD_DIM), ka.transpose(0, 2, 1, 3), va.transpose(0, 2, 1, 3),
                             WINDOW_A, w['sinks'])
        new_win = _last_rows(jnp.stack([ka, va], axis=2), WINDOW_A)
    else:
        o_a, new_win = _window_step_attention(qa, ka, va, st['a_win'], WINDOW_A, sinks)
    ig = ib + w['b_igate']
    lf = jax.nn.log_sigmoid(fb + w['b_fgate'])
    if prompt:
        gates = jnp.concatenate([ig.transpose(0, 2, 1), lf.transpose(0, 2, 1)], axis=1)
        hb, c_t, n_rep, m_rep = _mlstm_prompt(qb, kb.transpose(0, 2, 1), vb, gates, ob, w['h_norm'])
        c_new, n_new, m_new = c_t.transpose(0, 1, 3, 2), n_rep[..., 0], m_rep[:, :, 0, 0]
    else:
        qb = qb.reshape(bsz, t, H_B, DK_B)
        kb = kb.reshape(bsz, t, H_B, DK_B) * (DK_B ** -0.5)
        vb = vb.reshape(bsz, t, H_B, DV_B)
        init = (st['b_c'], st['b_n'], st['b_m'])
        (c_new, n_new, m_new), hb = _mlstm_chunk(init, (qb, kb, vb, ig, lf))
        hb = _rms_norm(hb, w['h_norm']) * jax.nn.sigmoid(ob.reshape(bsz, t, H_B, DV_B))
    xr = _out_proj(o_a.reshape(bsz * t, H_A * HEAD_DIM), hb.reshape(bsz * t, H_B * DV_B), w['w_out'],
                   x.reshape(bsz * t, D_MODEL))
    xr = _ffn(xr, w['norm_ffn'], w['w_gu'], w['w_down'])
    return xr.reshape(bsz, t, D_MODEL), (new_win, c_new, n_new, m_new)


def _odd_mixers(x, w, st, page_table, prompt):
    bsz, t, _ = x.shape
    r_d = H_D // KVH_D
    n_in = sum(IN_O_SIZES)
    proj_full = _norm_matmul(x.reshape(bsz * t, D_MODEL), w['norm_mix'], w['w_in_p'])
    proj = proj_full[:, :n_in].reshape(bsz, t, n_in)
    zc, xbc, dtc, qd, kvd, gd = _split_cols(proj, IN_O_SIZES)
    dt = jax.nn.softplus(dtc + w['dt_bias'])
    if prompt:
        yc, h_new = _ssd_prompt(proj_full.reshape(bsz, t, -1), dt, -jnp.exp(w['a_log']), w['conv_w'], w['conv_b'],
                                w['d_skip'], w['y_norm'])
        new_conv = xbc[:, t - (CONV_W - 1):]
    else:
        xbc, new_conv = _causal_conv(xbc, st['c_conv'], w['conv_w'], w['conv_b'])
        xc, bc, cc = _split_cols(xbc, (D_INNER_C, G_C * N_C, G_C * N_C))
        xc = xc.reshape(bsz, t, H_C, P_C)
        bc = bc.reshape(bsz, t, G_C, N_C)
        cc = cc.reshape(bsz, t, G_C, N_C)
        h_new, yc = _ssd_chunk(st['c_ssm'], (xc, dt, bc, cc), -jnp.exp(w['a_log']))
        yc = yc + w['d_skip'][:, None] * xc
        yc = yc.reshape(bsz, t, D_INNER_C) * jax.nn.silu(zc)
        yc = _rms_norm(yc.reshape(bsz, t, G_C, D_INNER_C // G_C), w['y_norm'].reshape(G_C, D_INNER_C // G_C))
    qd = _rms_norm(qd.reshape(bsz, t, KVH_D, r_d, HEAD_DIM), w['q_norm'])
    kc, vc, ks, vs, kw, vw = [a.reshape(bsz, t, KVH_D, HEAD_DIM) for a in _split_cols(kvd, (KVH_D * HEAD_DIM,) * 6)]
    ks = _rms_norm(ks, w['k_norm'])
    kw = _rms_norm(kw, w['k_norm'])
    new_kv = jnp.stack([kc, vc, ks, vs], axis=2)
    q_pos = jnp.arange(t) + (0 if prompt else PAST_LEN)
    ks_t, vs_t = ks.transpose(0, 2, 1, 3), vs.transpose(0, 2, 1, 3)

    def comp(rows, j):
        return _compress(rows, w['cmp_pos'][j], w['cmp_w1'][j], w['cmp_w2'][j])
    if prompt:
        kcmp, vcmp = comp(kc, 0), comp(vc, 1)
    else:
        assert t == 1
        pool = jnp.transpose(st['d_kv'], (0, 1, 3, 4, 5, 2)).reshape(
            st['d_kv'].shape[0], st['d_kv'].shape[1], CMP_W + SLC_W, PAGE_SIZE)
        n_pages = page_table.shape[1]
        past = _compress_pages(pool, page_table, w['cmp_pos'], w['cmp_w1'], w['cmp_w2'], st['layer'])
        past = past.reshape(bsz, n_pages * BLK_PER_PAGE, 2, KVH_D, HEAD_DIM)
        pad_t = -(-t // SLC_BLOCK) * SLC_BLOCK
        padw = ((0, 0), (0, pad_t - t), (0, 0), (0, 0))
        kcmp = jnp.concatenate([past[:, :, 0], comp(jnp.pad(kc, padw), 0)], axis=1)
        vcmp = jnp.concatenate([past[:, :, 1], comp(jnp.pad(vc, padw), 1)], axis=1)
    kcmp = _rms_norm(kcmp, w['k_norm'])
    n_top = min(TOP_N, kcmp.shape[1] // (SLC_BLOCK // CMP_BLOCK))
    if prompt:
        q_tok = qd.reshape(bsz, t, H_D * HEAD_DIM)
        hshape = (bsz, t, KVH_D, r_d, HEAD_DIM)
        o_cmp, sel = _cmp_select(q_tok, kcmp.transpose(0, 2, 1, 3), vcmp.transpose(0, 2, 1, 3), n_top)
        o_slc = _slc_attend(q_tok, sel, ks_t, vs_t).reshape(hshape)
        o_cmp = o_cmp.reshape(hshape)
        o_win = _window_attend(q_tok, kw.transpose(0, 2, 1, 3), vw.transpose(0, 2, 1, 3), WINDOW_D).reshape(hshape)
        new_win = _last_rows(jnp.stack([kw, vw], axis=2), WINDOW_D)
    else:
        blk_end = jnp.arange(kcmp.shape[1]) * CMP_BLOCK + (CMP_BLOCK - 1)
        o_cmp, p_cmp = _softmax_attend(qd, kcmp, vcmp, blk_end[None, :] <= q_pos[:, None])
        sel = _select_blocks(p_cmp, q_pos, n_top)[:, :, 0]
        q_rows = jnp.pad(qd[:, 0], ((0, 0), (0, 0), (0, 8 - r_d), (0, 0)))
        o8 = _slc_step(pool, page_table, sel, q_rows, ks_t, vs_t, st['layer'])
        o_slc = o8[:, None, :, :r_d]
        o_win, new_win = _window_step_attention(qd, kw, vw, st['d_win'], WINDOW_D, None)
    g = jax.nn.sigmoid(gd).reshape(bsz, t, KVH_D, r_d, 3)
    o_d = g[..., 0:1] * o_cmp + g[..., 1:2] * o_slc + g[..., 2:3] * o_win
    return (yc.reshape(bsz * t, D_INNER_C), o_d.reshape(bsz * t, H_D * HEAD_DIM)), (h_new, new_conv, new_kv, new_win)


def kernel(x_prompt, x_sample, cache_a_win, state_b_c, state_b_n, state_b_m, state_c_ssm, state_c_conv, cache_d_kv, cache_d_win, page_table, e_norm_mix, e_w_in, e_q_norm, e_k_norm, e_sinks, e_b_igate, e_b_fgate, e_h_norm, e_w_out, e_norm_ffn, e_w_gu, e_w_down, o_norm_mix, o_w_in, o_conv_w, o_conv_b, o_dt_bias, o_a_log, o_d_skip, o_y_norm, o_q_norm, o_k_norm, o_cmp_pos, o_cmp_w1, o_cmp_w2, o_w_out, o_norm_ffn, o_router, o_w_gu, o_w_down):
    i = 0
    we = {'norm_mix': e_norm_mix[i], 'w_in_p': _pad_cols(e_w_in[i], _round_up(sum(IN_E_SIZES), LANE)).astype(BF16),
          'q_norm': e_q_norm[i], 'k_norm': e_k_norm[i],
          'sinks': e_sinks[i], 'b_igate': e_b_igate[i], 'b_fgate': e_b_fgate[i], 'h_norm': e_h_norm[i],
          'w_out': e_w_out[i].astype(BF16), 'norm_ffn': e_norm_ffn[i], 'w_gu': e_w_gu[i].astype(BF16),
          'w_down': e_w_down[i].astype(BF16)}
    wo = {'norm_mix': o_norm_mix[i], 'w_in_p': _pad_cols(o_w_in[i], _round_up(sum(IN_O_SIZES), LANE)).astype(BF16),
          'conv_w': o_conv_w[i], 'conv_b': o_conv_b[i],
          'dt_bias': o_dt_bias[i], 'a_log': o_a_log[i], 'd_skip': o_d_skip[i], 'y_norm': o_y_norm[i],
          'q_norm': o_q_norm[i], 'k_norm': o_k_norm[i], 'cmp_pos': o_cmp_pos[i], 'cmp_w1': o_cmp_w1[i],
          'cmp_w2': o_cmp_w2[i], 'w_out': o_w_out[i].astype(BF16), 'norm_ffn': o_norm_ffn[i],
          'router': _pad_cols(o_router[i], LANE), 'w_gu': o_w_gu[i].astype(BF16), 'w_down': o_w_down[i].astype(BF16)}
    st_e = {'a_win': cache_a_win[i], 'b_c': state_b_c[i], 'b_n': state_b_n[i], 'b_m': state_b_m[i]}
    st_o = {'c_ssm': state_c_ssm[i], 'c_conv': state_c_conv[i], 'd_kv': cache_d_kv, 'layer': i,
            'd_win': cache_d_win[i]}

    xp, sp_e = _even_layer(x_prompt, we, None, True)
    xs, ss_e = _even_layer(x_sample, we, st_e, False)

    mix_p, sp_o = _odd_mixers(xp, wo, None, page_table, True)
    mix_s, ss_o = _odd_mixers(xs, wo, st_o, page_table, False)
    xp2, xnp, lgp = _out_proj_route(mix_p[0], mix_p[1], wo['w_out'], xp.reshape(-1, D_MODEL), wo['norm_ffn'], wo['router'])
    xs2, xns, lgs = _out_proj_route(mix_s[0], mix_s[1], wo['w_out'], xs.reshape(-1, D_MODEL), wo['norm_ffn'], wo['router'])
    xn_all = jnp.concatenate([xnp, xns], axis=0)
    lg_all = jnp.concatenate([lgp, lgs], axis=0)[:, :N_EXP]
    y_p, y_s = _moe_layer([xp2, xs2], xn_all, lg_all, wo['w_gu'], wo['w_down'])
    y_prompt = y_p.reshape(x_prompt.shape)
    y_sample = y_s.reshape(x_sample.shape)

    def one(a):
        return a[None]
    return (y_prompt, y_sample, one(sp_e[0]), one(ss_e[0]), one(sp_e[1]), one(ss_e[1]), one(sp_e[2]), one(ss_e[2]),
            one(sp_e[3]), one(ss_e[3]), one(sp_o[0]), one(ss_o[0]), one(sp_o[1]), one(ss_o[1]),
            one(sp_o[2]), one(ss_o[2]), one(sp_o[3]), one(ss_o[3]))
```

```python
import functools
import math

import numpy as np
import jax
import jax.numpy as jnp
from jax import lax
from jax.experimental import pallas as pl
from jax.experimental.pallas import tpu as pltpu

F32 = jnp.float32
BF16 = jnp.bfloat16

D_MODEL = 1024
PAST_LEN = 16384
PAGE_SIZE = 128
HEAD_DIM = 64
QBLOCK = 128
H_A, KVH_A, WINDOW_A = 8, 2, 128
H_B, DK_B, DV_B, CHUNK_B = 4, 64, 128, 64
P_C, H_C, G_C, N_C, CONV_W, CHUNK_C = 64, 8, 2, 128, 4, 128
D_INNER_C = H_C * P_C
CONV_DIM_C = D_INNER_C + 2 * G_C * N_C
H_D, KVH_D = 8, 2
CMP_BLOCK, SLC_BLOCK, TOP_N, WINDOW_D, SLC_QBLOCK = 32, 64, 16, 512, 32
D_FF, N_EXP, TOP_K = 3584, 8, 2
IN_E_SIZES = (H_A * HEAD_DIM, KVH_A * HEAD_DIM, KVH_A * HEAD_DIM, H_B * DK_B, H_B * DK_B, H_B * DV_B, H_B, H_B, H_B * DV_B)
IN_O_SIZES = (D_INNER_C, CONV_DIM_C, H_C, H_D * HEAD_DIM, 6 * KVH_D * HEAD_DIM, 3 * H_D)

LANE = 128
VMEM_LIMIT = 48 * 1024 * 1024
RMS_EPS = 1e-6
FF_TILE = 1792
ROW_TILE = 512


def _round_up(n, m):
    return -(-n // m) * m


def _rms(x, g):
    return x * lax.rsqrt(jnp.mean(x * x, axis=-1, keepdims=True) + RMS_EPS) * g


def _norm_matmul_body(x_ref, g_ref, w_ref, o_ref):
    xn = _rms(x_ref[...], g_ref[...])
    o_ref[...] = jnp.dot(xn.astype(BF16), w_ref[...], preferred_element_type=F32)


def _norm_matmul(x, gain, w):
    m, k = x.shape
    n = w.shape[1]
    tm = min(ROW_TILE, m)
    return pl.pallas_call(
        _norm_matmul_body,
        grid=(m // tm,),
        in_specs=[pl.BlockSpec((tm, k), lambda i: (i, 0)),
                  pl.BlockSpec((1, k), lambda i: (0, 0)),
                  pl.BlockSpec((k, n), lambda i: (0, 0))],
        out_specs=pl.BlockSpec((tm, n), lambda i: (i, 0)),
        out_shape=jax.ShapeDtypeStruct((m, n), F32),
        compiler_params=pltpu.CompilerParams(dimension_semantics=("arbitrary",), vmem_limit_bytes=VMEM_LIMIT),
        name="norm_in_proj",
    )(x, gain.reshape(1, k), w)


def _mix_matmul(a_ref, b_ref, w_ref):
    ka = a_ref.shape[1]
    return (jnp.dot(a_ref[...].astype(BF16), w_ref[0:ka, :], preferred_element_type=F32)
            + jnp.dot(b_ref[...].astype(BF16), w_ref[ka:, :], preferred_element_type=F32))


def _out_proj_body(a_ref, b_ref, w_ref, res_ref, o_ref):
    o_ref[...] = res_ref[...] + _mix_matmul(a_ref, b_ref, w_ref)


def _out_proj(a, b, w, res):
    m, ka = a.shape
    kb = b.shape[1]
    n = w.shape[1]
    tm = min(ROW_TILE, m)
    return pl.pallas_call(
        _out_proj_body,
        grid=(m // tm,),
        in_specs=[pl.BlockSpec((tm, ka), lambda i: (i, 0)),
                  pl.BlockSpec((tm, kb), lambda i: (i, 0)),
                  pl.BlockSpec((ka + kb, n), lambda i: (0, 0)),
                  pl.BlockSpec((tm, n), lambda i: (i, 0))],
        out_specs=pl.BlockSpec((tm, n), lambda i: (i, 0)),
        out_shape=jax.ShapeDtypeStruct((m, n), F32),
        compiler_params=pltpu.CompilerParams(dimension_semantics=("arbitrary",), vmem_limit_bytes=VMEM_LIMIT),
        name="out_proj",
    )(a, b, w, res)


def _out_proj_route_body(a_ref, b_ref, w_ref, res_ref, g_ref, r_ref, o_ref, xn_ref, lg_ref):
    x = res_ref[...] + _mix_matmul(a_ref, b_ref, w_ref)
    o_ref[...] = x
    xn = _rms(x, g_ref[...])
    xn_ref[...] = xn.astype(BF16)
    lg_ref[...] = jnp.dot(xn, r_ref[...], preferred_element_type=F32, precision=lax.Precision.HIGHEST)


def _out_proj_route(a, b, w, res, gain, router_pad):
    m, ka = a.shape
    kb = b.shape[1]
    n = w.shape[1]
    tm = min(ROW_TILE, m)
    return pl.pallas_call(
        _out_proj_route_body,
        grid=(m // tm,),
        in_specs=[pl.BlockSpec((tm, ka), lambda i: (i, 0)),
                  pl.BlockSpec((tm, kb), lambda i: (i, 0)),
                  pl.BlockSpec((ka + kb, n), lambda i: (0, 0)),
                  pl.BlockSpec((tm, n), lambda i: (i, 0)),
                  pl.BlockSpec((1, n), lambda i: (0, 0)),
                  pl.BlockSpec((n, LANE), lambda i: (0, 0))],
        out_specs=[pl.BlockSpec((tm, n), lambda i: (i, 0)),
                   pl.BlockSpec((tm, n), lambda i: (i, 0)),
                   pl.BlockSpec((tm, LANE), lambda i: (i, 0))],
        out_shape=[jax.ShapeDtypeStruct((m, n), F32),
                   jax.ShapeDtypeStruct((m, n), BF16),
                   jax.ShapeDtypeStruct((m, LANE), F32)],
        compiler_params=pltpu.CompilerParams(dimension_semantics=("arbitrary",), vmem_limit_bytes=VMEM_LIMIT),
        name="out_proj_route",
    )(a, b, w, res, gain.reshape(1, n), router_pad)


def _ffn_body(x_ref, g_ref, wg_ref, wu_ref, wd_ref, o_ref, xn_ref, acc_ref):
    c = pl.program_id(1)

    @pl.when(c == 0)
    def _():
        xn_ref[...] = _rms(x_ref[...], g_ref[...]).astype(BF16)
        acc_ref[...] = jnp.zeros_like(acc_ref)

    xn = xn_ref[...]
    gate = jnp.dot(xn, wg_ref[...], preferred_element_type=F32)
    up = jnp.dot(xn, wu_ref[...], preferred_element_type=F32)
    h = (gate * jax.nn.sigmoid(gate) * up).astype(BF16)
    acc_ref[...] += jnp.dot(h, wd_ref[...], preferred_element_type=F32)

    @pl.when(c == pl.num_programs(1) - 1)
    def _():
        o_ref[...] = x_ref[...] + acc_ref[...]


def _ffn(x, gain, w_gu, w_down):
    m, d = x.shape
    tm = min(ROW_TILE, m)
    nc = D_FF // FF_TILE
    return pl.pallas_call(
        _ffn_body,
        grid=(m // tm, nc),
        in_specs=[pl.BlockSpec((tm, d), lambda i, c: (i, 0)),
                  pl.BlockSpec((1, d), lambda i, c: (0, 0)),
                  pl.BlockSpec((d, FF_TILE), lambda i, c: (0, c)),
                  pl.BlockSpec((d, FF_TILE), lambda i, c: (0, c + nc)),
                  pl.BlockSpec((FF_TILE, d), lambda i, c: (c, 0))],
        out_specs=pl.BlockSpec((tm, d), lambda i, c: (i, 0)),
        out_shape=jax.ShapeDtypeStruct((m, d), F32),
        scratch_shapes=[pltpu.VMEM((tm, d), BF16), pltpu.VMEM((tm, d), F32)],
        compiler_params=pltpu.CompilerParams(dimension_semantics=("arbitrary", "arbitrary"), vmem_limit_bytes=VMEM_LIMIT),
        name="ffn_dense",
    )(x, gain.reshape(1, d), w_gu, w_gu, w_down)


def _moe_body(te_ref, tv_ref, x_ref, s_ref, wg_ref, wu_ref, wd_ref, o_ref, acc_ref):
    i = pl.program_id(0)
    c = pl.program_id(1)
    valid = tv_ref[i] > 0

    @pl.when(c == 0)
    def _():
        acc_ref[...] = jnp.zeros_like(acc_ref)

    @pl.when(valid)
    def _():
        x = x_ref[...]
        gate = jnp.dot(x, wg_ref[0], preferred_element_type=F32)
        up = jnp.dot(x, wu_ref[0], preferred_element_type=F32)
        h = (gate * jax.nn.sigmoid(gate) * up).astype(BF16)
        acc_ref[...] += jnp.dot(h, wd_ref[0], preferred_element_type=F32)

    @pl.when(c == pl.num_programs(1) - 1)
    def _():
        o_ref[...] = s_ref[...] * acc_ref[...]


def _moe_grouped(xs, scale, tile_expert, tile_valid, w_gu, w_down, tm):
    p, d = xs.shape
    nc = D_FF // FF_TILE
    grid_spec = pltpu.PrefetchScalarGridSpec(
        num_scalar_prefetch=2,
        grid=(p // tm, nc),
        in_specs=[pl.BlockSpec((tm, d), lambda i, c, te, tv: (i, 0)),
                  pl.BlockSpec((tm, 1), lambda i, c, te, tv: (i, 0)),
                  pl.BlockSpec((1, d, FF_TILE), lambda i, c, te, tv: (te[i], 0, c)),
                  pl.BlockSpec((1, d, FF_TILE), lambda i, c, te, tv: (te[i], 0, c + nc)),
                  pl.BlockSpec((1, FF_TILE, d), lambda i, c, te, tv: (te[i], c, 0))],
        out_specs=pl.BlockSpec((tm, d), lambda i, c, te, tv: (i, 0)),
        scratch_shapes=[pltpu.VMEM((tm, d), F32)],
    )
    return pl.pallas_call(
        _moe_body,
        grid_spec=grid_spec,
        out_shape=jax.ShapeDtypeStruct((p, d), F32),
        compiler_params=pltpu.CompilerParams(dimension_semantics=("arbitrary", "arbitrary"), vmem_limit_bytes=VMEM_LIMIT),
        name="moe_grouped",
    )(tile_expert, tile_valid, xs, scale, w_gu, w_gu, w_down)


def _moe_route(logits, tm):
    n = logits.shape[0]
    top_v, top_i = lax.top_k(logits, TOP_K)
    gate = jax.nn.softmax(top_v, axis=-1)
    flat_e = top_i.reshape(-1).astype(jnp.int32)
    na = n * TOP_K
    n_tiles = na // tm + N_EXP
    p = n_tiles * tm
    n_pad = p - na
    counts = jnp.sum(jax.nn.one_hot(flat_e, N_EXP, dtype=jnp.int32), axis=0)
    tiles_per_e = (counts + tm - 1) // tm
    cum_pad = jnp.cumsum(tiles_per_e * tm - counts)
    d = jnp.arange(n_pad, dtype=jnp.int32)
    pad_e = jnp.sum((d[:, None] >= cum_pad[None, :]).astype(jnp.int32), axis=1)
    keys = jnp.concatenate([2 * flat_e, 2 * pad_e + 1])
    ids = jnp.arange(p, dtype=jnp.int32)
    gates_all = jnp.concatenate([gate.reshape(-1), jnp.zeros((n_pad,), F32)])
    _, id_sorted, gate_of_pos = lax.sort((keys, ids, gates_all), num_keys=1, is_stable=True)
    tok_of_pos = jnp.where(id_sorted < na, id_sorted // TOP_K, 0)
    _, pos_by_id = lax.sort((id_sorted, ids), num_keys=1)
    pos_of_a = pos_by_id[:na].reshape(n, TOP_K)
    tile_end = jnp.cumsum(tiles_per_e)
    total_tiles = tile_end[-1]
    tile_ids = jnp.arange(n_tiles, dtype=jnp.int32)
    tile_expert = jnp.searchsorted(tile_end, jnp.minimum(tile_ids, total_tiles - 1), side="right").astype(jnp.int32)
    tile_expert = jnp.clip(tile_expert, 0, N_EXP - 1)
    tile_valid = (tile_ids < total_tiles).astype(jnp.int32)
    return tok_of_pos, gate_of_pos, pos_of_a, tile_expert, tile_valid


def _moe_layer(x_parts, xn_rows, logits, w_gu, w_down, tm=ROW_TILE):
    tok_of_pos, gate_of_pos, pos_of_a, tile_expert, tile_valid = _moe_route(logits, tm)
    xs = xn_rows[tok_of_pos]
    ys = _moe_grouped(xs, gate_of_pos.reshape(-1, 1), tile_expert, tile_valid, w_gu, w_down, tm)
    outs, r0 = [], 0
    for xp in x_parts:
        pa = pos_of_a[r0:r0 + xp.shape[0]]
        outs.append(xp + (ys[pa[:, 0]] + ys[pa[:, 1]]))
        r0 += xp.shape[0]
    return outs


QT = 128
KT = 512
NEG = -1e30
RANK_GROUP = 16
LOG2E = 1.4426950408889634


def _heads_to_rows(qt):
    r = qt.shape[1] // HEAD_DIM
    return jnp.concatenate([qt[:, h * HEAD_DIM:(h + 1) * HEAD_DIM] for h in range(r)], axis=0)


def _rows_to_heads(o, r):
    return jnp.concatenate([o[h * QT:(h + 1) * QT] for h in range(r)], axis=-1)


def _cmp_select_body(q_ref, kc_ref, vc_ref, o_ref, sel_ref, rank_ref, *, n_top):
    qi = pl.program_id(2)
    r = q_ref.shape[2] // HEAD_DIM
    q = _heads_to_rows(q_ref[0])
    s = lax.dot_general(q.astype(BF16), kc_ref[0, 0].astype(BF16), (((1,), (1,)), ((), ())),
                        preferred_element_type=F32) * (HEAD_DIM ** -0.5)
    row = lax.broadcasted_iota(jnp.int32, (r * QT, LANE), 0)
    lane = lax.broadcasted_iota(jnp.int32, (r * QT, LANE), 1)
    t_pos = qi * QT + (row & (QT - 1))
    s = jnp.where(lane * CMP_BLOCK + (CMP_BLOCK - 1) <= t_pos, s, -jnp.inf)
    m = jnp.max(s, axis=-1, keepdims=True)
    m = jnp.where(m > -jnp.inf, m, 0.0)
    p = jnp.exp(s - m)
    den = jnp.sum(p, axis=-1, keepdims=True)
    p = p / jnp.where(den > 0, den, 1.0)
    o = jnp.dot(p.astype(BF16), vc_ref[0, 0].astype(BF16), preferred_element_type=F32)
    o_ref[0] = _rows_to_heads(o, r)
    pt = p[0:QT]
    for h in range(1, r):
        pt = pt + p[h * QT:(h + 1) * QT]
    lane1 = lax.broadcasted_iota(jnp.int32, (QT, LANE), 1)
    row1 = lax.broadcasted_iota(jnp.int32, (QT, LANE), 0)
    even = (lane1 & 1) == 0
    lo = jnp.where(even, pt, pltpu.roll(pt, 1, 1))
    hi = jnp.where(even, pltpu.roll(pt, LANE - 1, 1), pt)
    imp = lo + hi
    blk = lane1 >> 1
    cur = (qi * QT + row1) >> 6
    forced = (blk == 0) | (blk == cur) | (blk == cur - 1)
    score = jnp.where(blk > cur, -jnp.inf, jnp.where(forced, jnp.inf, imp))
    lane_f = lane1.astype(F32)
    rank_ref[...] = jnp.zeros((QT, LANE), F32)
    for grp in range(0, LANE // 2, RANK_GROUP):
        @pl.when(2 * qi + 1 >= grp)
        def _():
            cnt = rank_ref[...]
            for j in range(grp, grp + RANK_GROUP):
                col = jnp.broadcast_to(score[:, 2 * j:2 * j + 1], (QT, LANE))
                later = jnp.where(lane_f >= 2.0 * j + 2.0, 1.0, 0.0)
                cnt = cnt + jnp.where(col > score, 1.0, 0.0) + jnp.where(col == score, later, 0.0)
            rank_ref[...] = cnt
    sel_ref[0, 0] = jnp.where((rank_ref[...] < n_top) & (blk <= cur), 1.0, 0.0)


def _cmp_select(q, kcmp, vcmp, n_top):
    b, t, w = q.shape
    g, d = kcmp.shape[1], kcmp.shape[3]
    gw = w // g
    assert kcmp.shape == (b, g, LANE, d) and t // CMP_BLOCK == LANE
    return pl.pallas_call(
        functools.partial(_cmp_select_body, n_top=n_top),
        grid=(b, g, t // QT),
        in_specs=[pl.BlockSpec((1, QT, gw), lambda i, j, k: (i, k, j)),
                  pl.BlockSpec((1, 1, LANE, d), lambda i, j, k: (i, j, 0, 0)),
                  pl.BlockSpec((1, 1, LANE, d), lambda i, j, k: (i, j, 0, 0))],
        out_specs=[pl.BlockSpec((1, QT, gw), lambda i, j, k: (i, k, j)),
                   pl.BlockSpec((1, 1, QT, LANE), lambda i, j, k: (i, j, k, 0))],
        out_shape=[jax.ShapeDtypeStruct((b, t, w), F32),
                   jax.ShapeDtypeStruct((b, g, t, LANE), F32)],
        scratch_shapes=[pltpu.VMEM((QT, LANE), F32)],
        compiler_params=pltpu.CompilerParams(dimension_semantics=("arbitrary",) * 3, vmem_limit_bytes=VMEM_LIMIT),
        name="nsa_cmp_select",
    )(q, kcmp, vcmp)


def _slc_attend_body(q_ref, sel_ref, k_ref, v_ref, o_ref):
    qi = pl.program_id(2)
    r = q_ref.shape[2] // HEAD_DIM
    rows = r * QT
    q = (_heads_to_rows(q_ref[0]) * (HEAD_DIM ** -0.5 * LOG2E)).astype(BF16)
    sel = sel_ref[0, 0]
    e_row = lax.broadcasted_iota(jnp.int32, (LANE, KT), 0)
    e_col = lax.broadcasted_iota(jnp.int32, (LANE, KT), 1)
    expand = jnp.where(e_row == 2 * (e_col >> 6), 1.0, 0.0).astype(BF16)
    t_pos = qi * QT + lax.broadcasted_iota(jnp.int32, (QT, KT), 0)
    k_off = lax.broadcasted_iota(jnp.int32, (QT, KT), 1)
    n_kt = (qi * QT + QT - 1) // KT + 1
    lanes_per_kt = 2 * (KT // SLC_BLOCK)
    ones = jnp.ones((KT, LANE - HEAD_DIM), BF16)

    def tile_scores(kt):
        start = pl.multiple_of(kt * KT, KT)
        k = k_ref[0, 0, pl.ds(start, KT), :].astype(BF16)
        v = v_ref[0, 0, pl.ds(start, KT), :].astype(BF16)
        v1 = jnp.concatenate([v, ones], axis=-1)
        s = lax.dot_general(q, k, (((1,), (1,)), ((), ())), preferred_element_type=F32)
        sel_kt = pltpu.roll(sel, (LANE - kt * lanes_per_kt) % LANE, 1).astype(BF16)
        picked = jnp.dot(sel_kt, expand, preferred_element_type=F32)
        bias = jnp.where((picked > 0.5) & (start + k_off <= t_pos), 0.0, NEG)
        return s + jnp.concatenate([bias] * r, axis=0), v1

    def fold(carry, s, v1):
        m, acc = carry
        m_new = jnp.maximum(m, jnp.max(s, axis=-1, keepdims=True))
        alpha = jnp.exp2(m - m_new)
        p = jnp.exp2(s - m_new).astype(BF16)
        return m_new, alpha * acc + jnp.dot(p, v1, preferred_element_type=F32)

    def body(i, carry):
        s0, v0 = tile_scores(2 * i)
        s1, v1 = tile_scores(2 * i + 1)
        return fold(fold(carry, s0, v0), s1, v1)

    init = (jnp.full((rows, 1), NEG, F32), jnp.zeros((rows, LANE), F32))
    m, acc = lax.fori_loop(0, (n_kt + 1) // 2, body, init)
    o_ref[0] = _rows_to_heads(acc[:, :HEAD_DIM] / acc[:, HEAD_DIM:HEAD_DIM + 1], r)


def _slc_attend(q, sel, ks, vs):
    b, t, w = q.shape
    g, d = ks.shape[1], ks.shape[3]
    gw = w // g
    return pl.pallas_call(
        _slc_attend_body,
        grid=(b, g, t // QT),
        in_specs=[pl.BlockSpec((1, QT, gw), lambda i, j, k: (i, k, j)),
                  pl.BlockSpec((1, 1, QT, LANE), lambda i, j, k: (i, j, k, 0)),
                  pl.BlockSpec((1, 1, t, d), lambda i, j, k: (i, j, 0, 0)),
                  pl.BlockSpec((1, 1, t, d), lambda i, j, k: (i, j, 0, 0))],
        out_specs=pl.BlockSpec((1, QT, gw), lambda i, j, k: (i, k, j)),
        out_shape=jax.ShapeDtypeStruct((b, t, w), F32),
        compiler_params=pltpu.CompilerParams(dimension_semantics=("arbitrary",) * 3, vmem_limit_bytes=VMEM_LIMIT),
        name="nsa_slc_attend",
    )(q, sel, ks, vs)


CMP_HIDDEN = 128
CMP_W = 2 * KVH_D * HEAD_DIM
SLC_W = 2 * KVH_D * HEAD_DIM
BLK_PER_PAGE = PAGE_SIZE // CMP_BLOCK
PAGES_PER_STEP = 64


def _page_copy(pool_ref, buf_ref, sem_ref, layer, page, slot, j):
    return pltpu.make_async_copy(pool_ref.at[layer, page, pl.ds(0, CMP_W), :], buf_ref.at[slot, j], sem_ref.at[slot])


def _compress_pages_body(pt_ref, pool_ref, post_ref, perm_ref, w1_ref, w2_ref, o_ref, buf_ref, xk_ref, xv_ref, sem_ref, *,
                         layer, n_steps):
    x_ref = (xk_ref, xv_ref)
    s = pl.program_id(0)
    slot = s % 2

    def start(step, slot_):
        for j in range(PAGES_PER_STEP):
            _page_copy(pool_ref, buf_ref, sem_ref, layer, pt_ref[step * PAGES_PER_STEP + j], slot_, j).start()

    @pl.when(s == 0)
    def _():
        start(0, 0)

    @pl.when(s + 1 < n_steps)
    def _():
        start(s + 1, 1 - slot)

    for j in range(PAGES_PER_STEP):
        _page_copy(pool_ref, buf_ref, sem_ref, layer, 0, slot, j).wait()

    for pp in range(PAGES_PER_STEP // 2):
        for br in range(2):
            t0 = buf_ref[slot, 2 * pp, br * LANE:(br + 1) * LANE, :] + post_ref[br]
            t1 = buf_ref[slot, 2 * pp + 1, br * LANE:(br + 1) * LANE, :] + post_ref[br]
            tc = jnp.concatenate([t0, t1], axis=-1).astype(BF16)
            tp = jnp.dot(tc, perm_ref[...], preferred_element_type=F32)
            x_ref[br][pp] = tp.T

    m = PAGES_PER_STEP * BLK_PER_PAGE
    for br in range(2):
        acc = jnp.zeros((m, KVH_D * CMP_HIDDEN), F32)
        for r in range(0, CMP_BLOCK, 2):
            x = jnp.concatenate([x_ref[br][:, rr * 8:(rr + 1) * 8, :].reshape(m, LANE) for rr in (r, r + 1)], axis=-1)
            acc = acc + jnp.dot(x.astype(BF16), w1_ref[br, r // 2], preferred_element_type=F32)
        h = acc * jax.nn.sigmoid(acc)
        o_ref[:, br * LANE:(br + 1) * LANE] = jnp.dot(h.astype(BF16), w2_ref[br], preferred_element_type=F32)


def _block_diag2(w):
    z = jnp.zeros_like(w)
    return jnp.concatenate([jnp.concatenate([w, z], axis=-1), jnp.concatenate([z, w], axis=-1)], axis=-2)


def _compress_pages(pool, page_table, cmp_pos, cmp_w1, cmp_w2, layer):
    bsz, n_pages = page_table.shape
    total = bsz * n_pages
    n_steps = total // PAGES_PER_STEP
    assert total % PAGES_PER_STEP == 0 and KVH_D == 2
    w1 = cmp_w1.reshape(2, CMP_BLOCK, HEAD_DIM, CMP_HIDDEN)
    w1_bd = _block_diag2(w1).astype(BF16).reshape(2, CMP_BLOCK // 2, 2 * LANE, KVH_D * CMP_HIDDEN)
    w2_bd = _block_diag2(cmp_w2).astype(BF16)
    pos_t = jnp.tile(jnp.transpose(cmp_pos, (0, 2, 1)), (1, KVH_D, BLK_PER_PAGE))
    lane_in = np.arange(2 * PAGE_SIZE)
    page2, blk, r = lane_in // PAGE_SIZE, (lane_in % PAGE_SIZE) // CMP_BLOCK, lane_in % CMP_BLOCK
    perm = np.zeros((2 * PAGE_SIZE, 2 * PAGE_SIZE), np.float32)
    perm[lane_in, r * (2 * BLK_PER_PAGE) + page2 * BLK_PER_PAGE + blk] = 1.0
    perm = jnp.asarray(perm, BF16)
    m = PAGES_PER_STEP * BLK_PER_PAGE
    grid_spec = pltpu.PrefetchScalarGridSpec(
        num_scalar_prefetch=1,
        grid=(n_steps,),
        in_specs=[pl.BlockSpec(memory_space=pl.ANY),
                  pl.BlockSpec((2, LANE, PAGE_SIZE), lambda s, pt: (0, 0, 0)),
                  pl.BlockSpec((2 * PAGE_SIZE, 2 * PAGE_SIZE), lambda s, pt: (0, 0)),
                  pl.BlockSpec((2, CMP_BLOCK // 2, 2 * LANE, KVH_D * CMP_HIDDEN), lambda s, pt: (0, 0, 0, 0)),
                  pl.BlockSpec((2, KVH_D * CMP_HIDDEN, LANE), lambda s, pt: (0, 0, 0))],
        out_specs=pl.BlockSpec((m, CMP_W), lambda s, pt: (s, 0)),
        scratch_shapes=[pltpu.VMEM((2, PAGES_PER_STEP, CMP_W, PAGE_SIZE), F32),
                        pltpu.VMEM((PAGES_PER_STEP // 2, 2 * PAGE_SIZE, LANE), F32),
                        pltpu.VMEM((PAGES_PER_STEP // 2, 2 * PAGE_SIZE, LANE), F32),
                        pltpu.SemaphoreType.DMA((2,))],
    )
    return pl.pallas_call(
        functools.partial(_compress_pages_body, layer=layer, n_steps=n_steps),
        grid_spec=grid_spec,
        out_shape=jax.ShapeDtypeStruct((total * BLK_PER_PAGE, CMP_W), F32),
        compiler_params=pltpu.CompilerParams(dimension_semantics=("arbitrary",), vmem_limit_bytes=VMEM_LIMIT),
        name="nsa_compress_pages",
    )(page_table.reshape(-1), pool, pos_t, perm, w1_bd, w2_bd)


def _compress_prompt_body(xk_ref, xv_ref, pos_ref, w1_ref, w2_ref, o_ref):
    x_ref = (xk_ref, xv_ref)
    nb = xk_ref.shape[0] // CMP_BLOCK
    for br in range(2):
        acc = jnp.zeros((nb, KVH_D * CMP_HIDDEN), F32)
        for r in range(CMP_BLOCK):
            x = x_ref[br][pl.ds(r, nb, stride=CMP_BLOCK), :] + pos_ref[br, r:r + 1, :]
            acc = acc + jnp.dot(x.astype(BF16), w1_ref[br, r], preferred_element_type=F32)
        h = acc * jax.nn.sigmoid(acc)
        o_ref[:, br * LANE:(br + 1) * LANE] = jnp.dot(h.astype(BF16), w2_ref[br], preferred_element_type=F32)


def _compress_prompt(rows, cmp_pos, cmp_w1, cmp_w2):
    b, t, w = rows.shape
    nb = t // CMP_BLOCK
    w1 = cmp_w1.reshape(2, CMP_BLOCK, HEAD_DIM, CMP_HIDDEN)
    w1_bd = _block_diag2(w1).astype(BF16)
    w2_bd = _block_diag2(cmp_w2).astype(BF16)
    pos = jnp.concatenate([cmp_pos, cmp_pos], axis=-1)
    return pl.pallas_call(
        _compress_prompt_body,
        grid=(b,),
        in_specs=[pl.BlockSpec((None, t, LANE), lambda i: (i, 0, 0)),
                  pl.BlockSpec((None, t, LANE), lambda i: (i, 0, 1)),
                  pl.BlockSpec((2, CMP_BLOCK, LANE), lambda i: (0, 0, 0)),
                  pl.BlockSpec((2, CMP_BLOCK, LANE, KVH_D * CMP_HIDDEN), lambda i: (0, 0, 0, 0)),
                  pl.BlockSpec((2, KVH_D * CMP_HIDDEN, LANE), lambda i: (0, 0, 0))],
        out_specs=pl.BlockSpec((None, nb, w), lambda i: (i, 0, 0)),
        out_shape=jax.ShapeDtypeStruct((b, nb, w), F32),
        compiler_params=pltpu.CompilerParams(dimension_semantics=("arbitrary",), vmem_limit_bytes=VMEM_LIMIT),
        name="nsa_compress_prompt",
    )(rows, rows, pos, w1_bd, w2_bd)


def _slc_copy(pool_ref, buf_ref, sem_ref, layer, page, feat0, slot, idx):
    return pltpu.make_async_copy(pool_ref.at[layer, page, pl.ds(feat0, HEAD_DIM), :], buf_ref.at[slot, idx], sem_ref.at[slot])


def _slc_step_body(pt_ref, sel_ref, pool_ref, q_ref, kn_ref, vn_ref, o_ref, kbuf_ref, vbuf_ref, sem_ref, *,
                   layer, n_seq, n_pages, new_blk):
    b = pl.program_id(0)
    slot = b % 2
    n_sel = KVH_D * TOP_N

    def copies(seq, slot_, from_table):
        out = []
        for g in range(KVH_D):
            for i in range(TOP_N):
                idx = g * TOP_N + i
                page = 0
                if from_table:
                    blk = jnp.minimum(sel_ref[seq * n_sel + idx], new_blk - 1)
                    page = pt_ref[seq * n_pages + (blk >> 1)]
                out.append(_slc_copy(pool_ref, kbuf_ref, sem_ref, layer, page, CMP_W + g * HEAD_DIM, slot_, idx))
                out.append(_slc_copy(pool_ref, vbuf_ref, sem_ref, layer, page, CMP_W + (KVH_D + g) * HEAD_DIM, slot_, idx))
        return out

    @pl.when(b == 0)
    def _():
        for c in copies(0, 0, True):
            c.start()

    @pl.when(b + 1 < n_seq)
    def _():
        for c in copies(b + 1, 1 - slot, True):
            c.start()

    for c in copies(0, slot, False):
        c.wait()

    half = lax.broadcasted_iota(jnp.int32, (8, LANE), 1) >> 6
    for g in range(KVH_D):
        q = q_ref[0, g] * (HEAD_DIM ** -0.5)
        qb = q.astype(BF16)
        s_parts = []
        for i in range(TOP_N):
            blk = sel_ref[b * n_sel + g * TOP_N + i]
            s_i = jnp.dot(qb, kbuf_ref[slot, g * TOP_N + i].astype(BF16), preferred_element_type=F32)
            ok = (half == (blk & 1)) & (blk < new_blk)
            s_parts.append(jnp.where(ok, s_i, NEG))
        s = jnp.concatenate(s_parts, axis=-1)
        s_new = jnp.sum(q * kn_ref[0, g], axis=-1, keepdims=True)
        m = jnp.maximum(jnp.max(s, axis=-1, keepdims=True), s_new)
        p = jnp.exp(s - m)
        p_new = jnp.exp(s_new - m)
        den = jnp.sum(p, axis=-1, keepdims=True) + p_new
        o = p_new * vn_ref[0, g]
        for i in range(TOP_N):
            p_i = p[:, i * LANE:(i + 1) * LANE].astype(BF16)
            o = o + lax.dot_general(p_i, vbuf_ref[slot, g * TOP_N + i].astype(BF16), (((1,), (1,)), ((), ())),
                                    preferred_element_type=F32)
        o_ref[0, g] = o / den


def _slc_step(pool, page_table, sel, q, k_new, v_new, layer):
    bsz, n_pages = page_table.shape
    n_sel = KVH_D * TOP_N
    grid_spec = pltpu.PrefetchScalarGridSpec(
        num_scalar_prefetch=2,
        grid=(bsz,),
        in_specs=[pl.BlockSpec(memory_space=pl.ANY),
                  pl.BlockSpec((1, KVH_D, 8, HEAD_DIM), lambda b, pt, sl: (b, 0, 0, 0)),
                  pl.BlockSpec((1, KVH_D, 1, HEAD_DIM), lambda b, pt, sl: (b, 0, 0, 0)),
                  pl.BlockSpec((1, KVH_D, 1, HEAD_DIM), lambda b, pt, sl: (b, 0, 0, 0))],
        out_specs=pl.BlockSpec((1, KVH_D, 8, HEAD_DIM), lambda b, pt, sl: (b, 0, 0, 0)),
        scratch_shapes=[pltpu.VMEM((2, n_sel, HEAD_DIM, PAGE_SIZE), F32),
                        pltpu.VMEM((2, n_sel, HEAD_DIM, PAGE_SIZE), F32),
                        pltpu.SemaphoreType.DMA((2,))],
    )
    return pl.pallas_call(
        functools.partial(_slc_step_body, layer=layer, n_seq=bsz, n_pages=n_pages, new_blk=2 * n_pages),
        grid_spec=grid_spec,
        out_shape=jax.ShapeDtypeStruct((bsz, KVH_D, 8, HEAD_DIM), F32),
        compiler_params=pltpu.CompilerParams(dimension_semantics=("arbitrary",), vmem_limit_bytes=VMEM_LIMIT),
        name="nsa_slc_step",
    )(page_table.reshape(-1), sel.reshape(-1), pool, q, k_new, v_new)


def _window_attend_body(sink_ref, q_ref, k_ref, v_ref, o_ref, *, window, use_sinks):
    g = pl.program_id(1)
    qi = pl.program_id(2)
    r = q_ref.shape[2] // HEAD_DIM
    rows = r * QT
    span = window + QT
    q = (_heads_to_rows(q_ref[0]) * (HEAD_DIM ** -0.5)).astype(BF16)
    start = pl.multiple_of(jnp.maximum(qi * QT - window, 0), QT)
    k = k_ref[0, 0, pl.ds(start, span), :].astype(BF16)
    v = v_ref[0, 0, pl.ds(start, span), :].astype(BF16)
    s = lax.dot_general(q, k, (((1,), (1,)), ((), ())), preferred_element_type=F32)
    row = lax.broadcasted_iota(jnp.int32, (rows, span), 0)
    diff = qi * QT + (row & (QT - 1)) - (start + lax.broadcasted_iota(jnp.int32, (rows, span), 1))
    s = jnp.where((diff >= 0) & (diff < window), s, -jnp.inf)
    m = jnp.max(s, axis=-1, keepdims=True)
    if use_sinks:
        head = lax.broadcasted_iota(jnp.int32, (rows, 1), 0) // QT
        sink = jnp.zeros((rows, 1), F32)
        for h in range(r):
            sink = jnp.where(head == h, sink_ref[g * r + h], sink)
        m = jnp.maximum(m, sink)
    p = jnp.exp(s - m)
    den = jnp.sum(p, axis=-1, keepdims=True)
    if use_sinks:
        den = den + jnp.exp(sink - m)
    o = jnp.dot(p.astype(BF16), v, preferred_element_type=F32) / den
    o_ref[0] = _rows_to_heads(o, r)


def _window_attend(q, k, v, window, sinks=None):
    b, t, w = q.shape
    g, d = k.shape[1], k.shape[3]
    gw = w // g
    use_sinks = sinks is not None
    if sinks is None:
        sinks = jnp.zeros((w // d,), F32)
    grid_spec = pltpu.PrefetchScalarGridSpec(
        num_scalar_prefetch=1,
        grid=(b, g, t // QT),
        in_specs=[pl.BlockSpec((1, QT, gw), lambda i, j, n, sk: (i, n, j)),
                  pl.BlockSpec((1, 1, t, d), lambda i, j, n, sk: (i, j, 0, 0)),
                  pl.BlockSpec((1, 1, t, d), lambda i, j, n, sk: (i, j, 0, 0))],
        out_specs=pl.BlockSpec((1, QT, gw), lambda i, j, n, sk: (i, n, j)),
    )
    return pl.pallas_call(
        functools.partial(_window_attend_body, window=window, use_sinks=use_sinks),
        grid_spec=grid_spec,
        out_shape=jax.ShapeDtypeStruct((b, t, w), F32),
        compiler_params=pltpu.CompilerParams(dimension_semantics=("arbitrary",) * 3, vmem_limit_bytes=VMEM_LIMIT),
        name="window_attend_%d" % window,
    )(sinks.astype(F32), q, k, v)


MLSTM_CHUNK = 128
MLSTM_NB = 2
HIGHEST = lax.Precision.HIGHEST


def _mlstm_body(q_ref, kt_ref, v_ref, g_ref, ob_ref, hn_ref, o_ref, c_out, n_out, m_out, c_ref, n_ref, m_ref):
    ci = pl.program_id(1)
    L = MLSTM_CHUNK

    @pl.when(ci == 0)
    def _():
        c_ref[...] = jnp.zeros_like(c_ref)
        n_ref[...] = jnp.zeros_like(n_ref)
        m_ref[...] = jnp.zeros_like(m_ref)

    row = lax.broadcasted_iota(jnp.int32, (L, L), 0)
    col = lax.broadcasted_iota(jnp.int32, (L, L), 1)
    causal = col <= row
    upper = jnp.where(row <= col, 1.0, 0.0)
    for bb, h in [(bb, h) for bb in range(MLSTM_NB) for h in range(H_B)]:
        q = q_ref[bb, :, h * DK_B:(h + 1) * DK_B]
        kt = kt_ref[bb, h * DK_B:(h + 1) * DK_B, :] * (DK_B ** -0.5)
        v = v_ref[bb, :, h * DV_B:(h + 1) * DV_B]
        ig = g_ref[bb, h:h + 1, :]
        lf = g_ref[bb, H_B + h:H_B + h + 1, :]
        m_prev = m_ref[bb, h, 0:1, 0:1]
        f_col = jnp.sum(jnp.where(causal, jnp.broadcast_to(lf, (L, L)), 0.0), axis=-1, keepdims=True)
        f_row = jnp.dot(jnp.broadcast_to(lf, (8, L)), upper, preferred_element_type=F32, precision=HIGHEST)[0:1]
        log_d = jnp.where(causal, f_col - f_row + ig, -jnp.inf)
        m_inter = f_col + m_prev
        m_t = jnp.maximum(m_inter, jnp.max(log_d, axis=-1, keepdims=True))
        d_mat = jnp.exp(log_d - m_t)
        w_inter = jnp.exp(m_inter - m_t)
        qb = q.astype(BF16)
        qk = jnp.dot(qb, kt.astype(BF16), preferred_element_type=F32) * d_mat
        num = jnp.dot(qk.astype(BF16), v.astype(BF16), preferred_element_type=F32)
        num = num + w_inter * jnp.dot(qb, c_ref[bb, h].astype(BF16), preferred_element_type=F32)
        qn = jnp.dot(qb, n_ref[bb, h].astype(BF16), preferred_element_type=F32)[:, 0:1]
        den = jnp.sum(qk, axis=-1, keepdims=True) + w_inter * qn
        hh = num / jnp.maximum(jnp.abs(den), jnp.exp(-m_t))
        hh = hh * lax.rsqrt(jnp.mean(hh * hh, axis=-1, keepdims=True) + RMS_EPS) * hn_ref[...]
        o_ref[bb, :, h * DV_B:(h + 1) * DV_B] = hh * jax.nn.sigmoid(ob_ref[bb, :, h * DV_B:(h + 1) * DV_B])
        m_new = m_t[L - 1:L, :]
        f_last = f_col[L - 1:L, :]
        w_end = jnp.exp(f_last - f_row + ig - m_new)
        decay = jnp.exp(f_last + m_prev - m_new)
        ktw = kt * w_end
        c_ref[bb, h] = decay * c_ref[bb, h] + jnp.dot(ktw.astype(BF16), v.astype(BF16), preferred_element_type=F32)
        n_ref[bb, h] = decay * n_ref[bb, h] + jnp.sum(ktw, axis=-1, keepdims=True)
        m_ref[bb, h] = jnp.broadcast_to(m_new, m_ref.shape[2:])

    @pl.when(ci == pl.num_programs(1) - 1)
    def _():
        c_out[...] = c_ref[...]
        n_out[...] = n_ref[...]
        m_out[...] = m_ref[...]


def _mlstm_prompt(q, kt, v, gates, ob, h_norm):
    b, t, _ = q.shape
    L = MLSTM_CHUNK
    nc = t // L
    return pl.pallas_call(
        _mlstm_body,
        grid=(b // MLSTM_NB, nc),
        in_specs=[pl.BlockSpec((MLSTM_NB, L, H_B * DK_B), lambda i, c: (i, c, 0)),
                  pl.BlockSpec((MLSTM_NB, H_B * DK_B, L), lambda i, c: (i, 0, c)),
                  pl.BlockSpec((MLSTM_NB, L, H_B * DV_B), lambda i, c: (i, c, 0)),
                  pl.BlockSpec((MLSTM_NB, 2 * H_B, L), lambda i, c: (i, 0, c)),
                  pl.BlockSpec((MLSTM_NB, L, H_B * DV_B), lambda i, c: (i, c, 0)),
                  pl.BlockSpec((1, DV_B), lambda i, c: (0, 0))],
        out_specs=[pl.BlockSpec((MLSTM_NB, L, H_B * DV_B), lambda i, c: (i, c, 0)),
                   pl.BlockSpec((MLSTM_NB, H_B, DK_B, DV_B), lambda i, c: (i, 0, 0, 0)),
                   pl.BlockSpec((MLSTM_NB, H_B, DK_B, LANE), lambda i, c: (i, 0, 0, 0)),
                   pl.BlockSpec((MLSTM_NB, H_B, 8, LANE), lambda i, c: (i, 0, 0, 0))],
        out_shape=[jax.ShapeDtypeStruct((b, t, H_B * DV_B), F32),
                   jax.ShapeDtypeStruct((b, H_B, DK_B, DV_B), F32),
                   jax.ShapeDtypeStruct((b, H_B, DK_B, LANE), F32),
                   jax.ShapeDtypeStruct((b, H_B, 8, LANE), F32)],
        scratch_shapes=[pltpu.VMEM((MLSTM_NB, H_B, DK_B, DV_B), F32), pltpu.VMEM((MLSTM_NB, H_B, DK_B, LANE), F32),
                        pltpu.VMEM((MLSTM_NB, H_B, 8, LANE), F32)],
        compiler_params=pltpu.CompilerParams(dimension_semantics=("arbitrary", "arbitrary"), vmem_limit_bytes=VMEM_LIMIT),
        name="mlstm_prompt",
    )(q, kt, v, gates, ob, h_norm.reshape(1, DV_B))


SSD_CHUNK = 128
CONV_PAD = 8


def _ssd_body(z_ref, xr_ref, bcr_ref, dtc_ref, dtr_ref, an_ref, cw_ref, cb_ref, ds_ref, yn_ref,
              o_ref, h_out, xp_ref, h_ref):
    ci = pl.program_id(1)
    L = SSD_CHUNK
    r = H_C // G_C

    @pl.when(ci == 0)
    def _():
        xp_ref[0:CONV_PAD, :] = jnp.zeros((CONV_PAD, CONV_DIM_C), F32)
        h_ref[...] = jnp.zeros_like(h_ref)

    xp_ref[CONV_PAD:CONV_PAD + L, 0:D_INNER_C] = xr_ref[0]
    xp_ref[CONV_PAD:CONV_PAD + L, D_INNER_C:CONV_DIM_C] = bcr_ref[0]
    conv = cb_ref[...]
    for j in range(CONV_W):
        off = CONV_PAD - (CONV_W - 1) + j
        conv = conv + xp_ref[off:off + L, :] * cw_ref[j:j + 1, :]
    tail = xp_ref[L:L + CONV_PAD, :]
    xp_ref[0:CONV_PAD, :] = tail
    xbc = conv * jax.nn.sigmoid(conv)
    x = xbc[:, 0:D_INNER_C]
    bm = xbc[:, D_INNER_C:D_INNER_C + G_C * N_C]
    cm = xbc[:, D_INNER_C + G_C * N_C:CONV_DIM_C]

    row = lax.broadcasted_iota(jnp.int32, (L, L), 0)
    col = lax.broadcasted_iota(jnp.int32, (L, L), 1)
    causal = col <= row
    lower = jnp.where(causal, 1.0, 0.0)
    upper = jnp.where(row <= col, 1.0, 0.0)
    dt_c = dtc_ref[0]
    dt_r = dtr_ref[0]
    a_c = dt_c * an_ref[0:1, 0:H_C]
    a_r = dt_r * an_ref[:, H_C:H_C + 1]
    cum_c = jnp.dot(lower, a_c, preferred_element_type=F32, precision=HIGHEST)
    cum_r = jnp.dot(a_r, upper, preferred_element_type=F32, precision=HIGHEST)
    ys = []
    for g in range(G_C):
        bg = bm[:, g * N_C:(g + 1) * N_C].astype(BF16)
        cg = cm[:, g * N_C:(g + 1) * N_C].astype(BF16)
        cbm = lax.dot_general(cg, bg, (((1,), (1,)), ((), ())), preferred_element_type=F32)
        dtxw = []
        for hh in range(r):
            h = g * r + hh
            cc = cum_c[:, h:h + 1]
            cr = cum_r[h:h + 1, :]
            decay = jnp.exp(jnp.where(causal, cc - cr, -jnp.inf))
            dtx = dt_c[:, h:h + 1] * x[:, h * P_C:(h + 1) * P_C]
            y = jnp.dot((cbm * decay).astype(BF16), dtx.astype(BF16), preferred_element_type=F32)
            hs = h_ref[h].astype(BF16)
            y = y + lax.dot_general(cg, hs, (((1,), (1,)), ((), ())), preferred_element_type=F32) * jnp.exp(cc)
            ys.append(y + ds_ref[0:1, h:h + 1] * x[:, h * P_C:(h + 1) * P_C])
            cl = cum_c[L - 1:L, h:h + 1]
            dtxw.append(dtx * jnp.exp(cl - cc))
        dtxw = jnp.concatenate(dtxw, axis=-1)
        upd = jnp.dot(dtxw.T.astype(BF16), bg, preferred_element_type=F32)
        for hh in range(r):
            h = g * r + hh
            cl = cum_c[L - 1:L, h:h + 1]
            h_ref[h] = h_ref[h] * jnp.exp(cl) + upd[hh * P_C:(hh + 1) * P_C, :]
    y = jnp.concatenate(ys, axis=-1)
    z = z_ref[0]
    y = y * (z * jax.nn.sigmoid(z))
    gw = D_INNER_C // G_C
    outs = []
    for g in range(G_C):
        yg = y[:, g * gw:(g + 1) * gw]
        outs.append(yg * lax.rsqrt(jnp.mean(yg * yg, axis=-1, keepdims=True) + RMS_EPS))
    o_ref[0] = jnp.concatenate(outs, axis=-1) * yn_ref[...]

    @pl.when(ci == pl.num_programs(1) - 1)
    def _():
        h_out[0] = h_ref[...]


def _ssd_prompt(proj, dt, a_neg, conv_w, conv_b, d_skip, y_norm):
    b, t, _ = proj.shape
    L = SSD_CHUNK
    nc = t // L
    an = jnp.concatenate([jnp.broadcast_to(a_neg[None, :], (H_C, H_C)), a_neg[:, None]], axis=1)
    an = jnp.pad(an, ((0, 0), (0, LANE - an.shape[1])))
    ds = jnp.pad(d_skip[None, :], ((0, 0), (0, LANE - H_C)))
    return pl.pallas_call(
        _ssd_body,
        grid=(b, nc),
        in_specs=[pl.BlockSpec((1, L, D_INNER_C), lambda i, c: (i, c, 0)),
                  pl.BlockSpec((1, L, D_INNER_C), lambda i, c: (i, c, 1)),
                  pl.BlockSpec((1, L, 2 * G_C * N_C), lambda i, c: (i, c, 2)),
                  pl.BlockSpec((1, L, H_C), lambda i, c: (i, c, 0)),
                  pl.BlockSpec((1, H_C, L), lambda i, c: (i, 0, c)),
                  pl.BlockSpec((H_C, LANE), lambda i, c: (0, 0)),
                  pl.BlockSpec((CONV_W, CONV_DIM_C), lambda i, c: (0, 0)),
                  pl.BlockSpec((1, CONV_DIM_C), lambda i, c: (0, 0)),
                  pl.BlockSpec((1, LANE), lambda i, c: (0, 0)),
                  pl.BlockSpec((1, D_INNER_C), lambda i, c: (0, 0))],
        out_specs=[pl.BlockSpec((1, L, D_INNER_C), lambda i, c: (i, c, 0)),
                   pl.BlockSpec((1, H_C, P_C, N_C), lambda i, c: (i, 0, 0, 0))],
        out_shape=[jax.ShapeDtypeStruct((b, t, D_INNER_C), F32),
                   jax.ShapeDtypeStruct((b, H_C, P_C, N_C), F32)],
        scratch_shapes=[pltpu.VMEM((CONV_PAD + SSD_CHUNK, CONV_DIM_C), F32), pltpu.VMEM((H_C, P_C, N_C), F32)],
        compiler_params=pltpu.CompilerParams(dimension_semantics=("arbitrary", "arbitrary"), vmem_limit_bytes=VMEM_LIMIT),
        name="ssd_prompt",
    )(proj, proj, proj, dt, dt.transpose(0, 2, 1), an, conv_w, conv_b.reshape(1, CONV_DIM_C), ds,
      y_norm.reshape(1, D_INNER_C))


def _rms_norm(x, g, eps=RMS_EPS):
    xf = x.astype(F32)
    y = xf * lax.rsqrt(jnp.mean(xf * xf, axis=-1, keepdims=True) + eps)
    return (y * g.astype(F32)).astype(x.dtype)


def _split_cols(x, sizes):
    return jnp.split(x, [int(s) for s in np.cumsum(sizes)[:-1]], axis=-1)


def _last_rows(a, n):
    t = a.shape[1]
    if t >= n:
        return a[:, t - n:]
    pad = [(0, 0)] * a.ndim
    pad[1] = (n - t, 0)
    return jnp.pad(a, pad)


def _softmax_attend(q, k, v, mask, sinks=None):
    s = jnp.einsum('...qgrd,...kgd->...grqk', q, k).astype(F32) * (HEAD_DIM ** -0.5)
    s = jnp.where(mask[..., None, None, :, :], s, -jnp.inf)
    m = jnp.max(s, axis=-1, keepdims=True)
    if sinks is not None:
        sk = sinks.astype(F32)[:, :, None, None]
        m = jnp.maximum(m, sk)
    m = jnp.where(jnp.isfinite(m), m, 0.0)
    p = jnp.exp(s - m)
    den = jnp.sum(p, axis=-1, keepdims=True)
    if sinks is not None:
        den = den + jnp.exp(sk - m)
    p = p / jnp.where(den > 0, den, 1.0)
    out = jnp.einsum('...grqk,...kgd->...qgrd', p.astype(v.dtype), v)
    return out, p


def _window_step_attention(q, k_new, v_new, buf, window, sinks):
    t = k_new.shape[1]
    k = jnp.concatenate([buf[:, :, 0], k_new], axis=1)
    v = jnp.concatenate([buf[:, :, 1], v_new], axis=1)
    q_pos = PAST_LEN + jnp.arange(t)
    k_pos = PAST_LEN - window + jnp.arange(window + t)
    diff = q_pos[:, None] - k_pos[None, :]
    mask = (k_pos[None, :] >= 0) & (diff >= 0) & (diff < window)
    out, _ = _softmax_attend(q, k, v, mask, sinks)
    new_buf = jnp.concatenate([buf, jnp.stack([k_new, v_new], axis=2)], axis=1)[:, t:]
    return out, new_buf


def _mlstm_chunk(state, inputs):
    c, n, m = state
    q, k, v, ig, lf = inputs
    L = q.shape[1]
    f_cum = jnp.cumsum(lf, axis=1)
    causal = jnp.tril(jnp.ones((L, L), dtype=bool))
    log_d = jnp.where(causal[None, :, :, None], f_cum[:, :, None, :] - f_cum[:, None, :, :] + ig[:, None, :, :], -jnp.inf)
    m_inter = f_cum + m[:, None, :]
    m_t = jnp.maximum(m_inter, jnp.max(log_d, axis=2))
    d_mat = jnp.exp(log_d - m_t[:, :, None, :])
    w_inter = jnp.exp(m_inter - m_t)
    qk = jnp.einsum('bthd,bshd->btsh', q, k) * d_mat
    num = jnp.einsum('btsh,bshv->bthv', qk, v) + w_inter[..., None] * jnp.einsum('bhvd,bthd->bthv', c, q)
    den = jnp.sum(qk, axis=2) + w_inter * jnp.einsum('bhd,bthd->bth', n, q)
    h = num / jnp.maximum(jnp.abs(den), jnp.exp(-m_t))[..., None]
    m_new = m_t[:, -1]
    w_end = jnp.exp(f_cum[:, -1:] - f_cum + ig - m_new[:, None, :])
    decay = jnp.exp(f_cum[:, -1] + m - m_new)
    c_new = decay[..., None, None] * c + jnp.einsum('bsh,bshv,bshd->bhvd', w_end, v, k)
    n_new = decay[..., None] * n + jnp.einsum('bsh,bshd->bhd', w_end, k)
    return (c_new, n_new, m_new), h


def _ssd_chunk(h, inputs, a_neg):
    x, dt, bm, cm = inputs
    bsz, L = x.shape[:2]
    r = H_C // G_C
    cum = jnp.cumsum(dt * a_neg, axis=1)
    causal = jnp.tril(jnp.ones((L, L), dtype=bool))
    seg = jnp.where(causal[None, :, :, None], cum[:, :, None, :] - cum[:, None, :, :], -jnp.inf)
    decay = jnp.exp(seg).reshape(bsz, L, L, G_C, r)
    dtx = (dt[..., None] * x).reshape(bsz, L, G_C, r, P_C)
    cb = jnp.einsum('btgn,bsgn->btsg', cm, bm)
    y = jnp.einsum('btsg,btsgr,bsgrp->btgrp', cb, decay, dtx)
    hg = h.reshape(bsz, G_C, r, P_C, N_C)
    y = y + jnp.einsum('btgn,bgrpn->btgrp', cm, hg) * jnp.exp(cum).reshape(bsz, L, G_C, r)[..., None]
    w_end = jnp.exp(cum[:, -1:] - cum).reshape(bsz, L, G_C, r)
    h_new = hg * jnp.exp(cum[:, -1]).reshape(bsz, G_C, r)[..., None, None] + jnp.einsum('bsgr,bsgrp,bsgn->bgrpn', w_end, dtx, bm)
    return h_new.reshape(bsz, H_C, P_C, N_C), y.reshape(bsz, L, H_C, P_C)


def _causal_conv(xbc, buf, w, b):
    t = xbc.shape[1]
    xp = jnp.concatenate([buf, xbc], axis=1)
    out = b + sum(xp[:, j:j + t] * w[j] for j in range(CONV_W))
    return jax.nn.silu(out), xp[:, t:]


def _compress(rows, pos, w1, w2):
    bsz, t, g, d = rows.shape
    nb = t // CMP_BLOCK
    blk = rows.reshape(bsz, nb, CMP_BLOCK, g, d) + pos[:, None, :]
    blk = blk.transpose(0, 1, 3, 2, 4).reshape(bsz, nb, g, CMP_BLOCK * d)
    return jax.nn.silu(blk @ w1) @ w2


def _select_blocks(p_cmp, q_pos, n_top):
    bsz, g, r, t, nbc = p_cmp.shape
    ratio = SLC_BLOCK // CMP_BLOCK
    imp = p_cmp.sum(axis=2).reshape(bsz, g, t, nbc // ratio, ratio).sum(axis=-1)
    blk = jnp.arange(nbc // ratio)[None, :]
    cur = (q_pos // SLC_BLOCK)[:, None]
    forced = (blk == 0) | (blk == cur) | (blk == cur - 1)
    score = jnp.where(blk > cur, -jnp.inf, jnp.where(forced, jnp.inf, imp))
    _, sel = lax.top_k(score, n_top)
    return sel.astype(jnp.int32)


def _pad_cols(w, n):
    return jnp.pad(w, ((0, 0), (0, n - w.shape[1])))


def _even_layer(x, w, st, prompt):
    bsz, t, _ = x.shape
    r_a = H_A // KVH_A
    n_in = sum(IN_E_SIZES)
    proj = _norm_matmul(x.reshape(bsz * t, D_MODEL), w['norm_mix'], w['w_in_p'])[:, :n_in].reshape(bsz, t, n_in)
    qa, ka, va, qb, kb, vb, ib, fb, ob = _split_cols(proj, IN_E_SIZES)
    qa = _rms_norm(qa.reshape(bsz, t, KVH_A, r_a, HEAD_DIM), w['q_norm'])
    ka = _rms_norm(ka.reshape(bsz, t, KVH_A, HEAD_DIM), w['k_norm'])
    va = va.reshape(bsz, t, KVH_A, HEAD_DIM)
    sinks = w['sinks'].reshape(KVH_A, r_a)
    if prompt:
        o_a = _window_attend(qa.reshape(bsz, t, H_A * HEAD_DIM), ka.transpose(0, 2, 1, 3), va.transpose(0, 2, 1, 3),
                             WINDOW_A, w['sinks'])
        new_win = _last_rows(jnp.stack([ka, va], axis=2), WINDOW_A)
    else:
        o_a, new_win = _window_step_attention(qa, ka, va, st['a_win'], WINDOW_A, sinks)
    ig = ib + w['b_igate']
    lf = jax.nn.log_sigmoid(fb + w['b_fgate'])
    if prompt:
        gates = jnp.concatenate([ig.transpose(0, 2, 1), lf.transpose(0, 2, 1)], axis=1)
        hb, c_t, n_rep, m_rep = _mlstm_prompt(qb, kb.transpose(0, 2, 1), vb, gates, ob, w['h_norm'])
        c_new, n_new, m_new = c_t.transpose(0, 1, 3, 2), n_rep[..., 0], m_rep[:, :, 0, 0]
    else:
        qb = qb.reshape(bsz, t, H_B, DK_B)
        kb = kb.reshape(bsz, t, H_B, DK_B) * (DK_B ** -0.5)
        vb = vb.reshape(bsz, t, H_B, DV_B)
        init = (st['b_c'], st['b_n'], st['b_m'])
        (c_new, n_new, m_new), hb = _mlstm_chunk(init, (qb, kb, vb, ig, lf))
        hb = _rms_norm(hb, w['h_norm']) * jax.nn.sigmoid(ob.reshape(bsz, t, H_B, DV_B))
    xr = _out_proj(o_a.reshape(bsz * t, H_A * HEAD_DIM), hb.reshape(bsz * t, H_B * DV_B), w['w_out'],
                   x.reshape(bsz * t, D_MODEL))
    xr = _ffn(xr, w['norm_ffn'], w['w_gu'], w['w_down'])
    return xr.reshape(bsz, t, D_MODEL), (new_win, c_new, n_new, m_new)


def _odd_mixers(x, w, st, page_table, prompt):
    bsz, t, _ = x.shape
    r_d = H_D // KVH_D
    n_in = sum(IN_O_SIZES)
    proj_full = _norm_matmul(x.reshape(bsz * t, D_MODEL), w['norm_mix'], w['w_in_p'])
    proj = proj_full[:, :n_in].reshape(bsz, t, n_in)
    zc, xbc, dtc, qd, kvd, gd = _split_cols(proj, IN_O_SIZES)
    dt = jax.nn.softplus(dtc + w['dt_bias'])
    if prompt:
        yc, h_new = _ssd_prompt(proj_full.reshape(bsz, t, -1), dt, -jnp.exp(w['a_log']), w['conv_w'], w['conv_b'],
                                w['d_skip'], w['y_norm'])
        new_conv = xbc[:, t - (CONV_W - 1):]
    else:
        xbc, new_conv = _causal_conv(xbc, st['c_conv'], w['conv_w'], w['conv_b'])
        xc, bc, cc = _split_cols(xbc, (D_INNER_C, G_C * N_C, G_C * N_C))
        xc = xc.reshape(bsz, t, H_C, P_C)
        bc = bc.reshape(bsz, t, G_C, N_C)
        cc = cc.reshape(bsz, t, G_C, N_C)
        h_new, yc = _ssd_chunk(st['c_ssm'], (xc, dt, bc, cc), -jnp.exp(w['a_log']))
        yc = yc + w['d_skip'][:, None] * xc
        yc = yc.reshape(bsz, t, D_INNER_C) * jax.nn.silu(zc)
        yc = _rms_norm(yc.reshape(bsz, t, G_C, D_INNER_C // G_C), w['y_norm'].reshape(G_C, D_INNER_C // G_C))
    qd = _rms_norm(qd.reshape(bsz, t, KVH_D, r_d, HEAD_DIM), w['q_norm'])
    kc, vc, ks, vs, kw, vw = [a.reshape(bsz, t, KVH_D, HEAD_DIM) for a in _split_cols(kvd, (KVH_D * HEAD_DIM,) * 6)]
    ks = _rms_norm(ks, w['k_norm'])
    kw = _rms_norm(kw, w['k_norm'])
    new_kv = jnp.stack([kc, vc, ks, vs], axis=2)
    q_pos = jnp.arange(t) + (0 if prompt else PAST_LEN)
    ks_t, vs_t = ks.transpose(0, 2, 1, 3), vs.transpose(0, 2, 1, 3)

    def comp(rows, j):
        return _compress(rows, w['cmp_pos'][j], w['cmp_w1'][j], w['cmp_w2'][j])
    if prompt:
        cmp_blocks = _compress_prompt(kvd[..., :CMP_W], w['cmp_pos'], w['cmp_w1'], w['cmp_w2'])
        kcmp = cmp_blocks[..., :KVH_D * HEAD_DIM].reshape(bsz, -1, KVH_D, HEAD_DIM)
        vcmp = cmp_blocks[..., KVH_D * HEAD_DIM:].reshape(bsz, -1, KVH_D, HEAD_DIM)
    else:
        assert t == 1
        pool = jnp.transpose(st['d_kv'], (0, 1, 3, 4, 5, 2)).reshape(
            st['d_kv'].shape[0], st['d_kv'].shape[1], CMP_W + SLC_W, PAGE_SIZE)
        n_pages = page_table.shape[1]
        past = _compress_pages(pool, page_table, w['cmp_pos'], w['cmp_w1'], w['cmp_w2'], st['layer'])
        past = past.reshape(bsz, n_pages * BLK_PER_PAGE, 2, KVH_D, HEAD_DIM)
        pad_t = -(-t // SLC_BLOCK) * SLC_BLOCK
        padw = ((0, 0), (0, pad_t - t), (0, 0), (0, 0))
        kcmp = jnp.concatenate([past[:, :, 0], comp(jnp.pad(kc, padw), 0)], axis=1)
        vcmp = jnp.concatenate([past[:, :, 1], comp(jnp.pad(vc, padw), 1)], axis=1)
    kcmp = _rms_norm(kcmp, w['k_norm'])
    n_top = min(TOP_N, kcmp.shape[1] // (SLC_BLOCK // CMP_BLOCK))
    if prompt:
        q_tok = qd.reshape(bsz, t, H_D * HEAD_DIM)
        hshape = (bsz, t, KVH_D, r_d, HEAD_DIM)
        o_cmp, sel = _cmp_select(q_tok, kcmp.transpose(0, 2, 1, 3), vcmp.transpose(0, 2, 1, 3), n_top)
        o_slc = _slc_attend(q_tok, sel, ks_t, vs_t).reshape(hshape)
        o_cmp = o_cmp.reshape(hshape)
        o_win = _window_attend(q_tok, kw.transpose(0, 2, 1, 3), vw.transpose(0, 2, 1, 3), WINDOW_D).reshape(hshape)
        new_win = _last_rows(jnp.stack([kw, vw], axis=2), WINDOW_D)
    else:
        blk_end = jnp.arange(kcmp.shape[1]) * CMP_BLOCK + (CMP_BLOCK - 1)
        o_cmp, p_cmp = _softmax_attend(qd, kcmp, vcmp, blk_end[None, :] <= q_pos[:, None])
        sel = _select_blocks(p_cmp, q_pos, n_top)[:, :, 0]
        q_rows = jnp.pad(qd[:, 0], ((0, 0), (0, 0), (0, 8 - r_d), (0, 0)))
        o8 = _slc_step(pool, page_table, sel, q_rows, ks_t, vs_t, st['layer'])
        o_slc = o8[:, None, :, :r_d]
        o_win, new_win = _window_step_attention(qd, kw, vw, st['d_win'], WINDOW_D, None)
    g = jax.nn.sigmoid(gd).reshape(bsz, t, KVH_D, r_d, 3)
    o_d = g[..., 0:1] * o_cmp + g[..., 1:2] * o_slc + g[..., 2:3] * o_win
    return (yc.reshape(bsz * t, D_INNER_C), o_d.reshape(bsz * t, H_D * HEAD_DIM)), (h_new, new_conv, new_kv, new_win)


def kernel(x_prompt, x_sample, cache_a_win, state_b_c, state_b_n, state_b_m, state_c_ssm, state_c_conv, cache_d_kv, cache_d_win, page_table, e_norm_mix, e_w_in, e_q_norm, e_k_norm, e_sinks, e_b_igate, e_b_fgate, e_h_norm, e_w_out, e_norm_ffn, e_w_gu, e_w_down, o_norm_mix, o_w_in, o_conv_w, o_conv_b, o_dt_bias, o_a_log, o_d_skip, o_y_norm, o_q_norm, o_k_norm, o_cmp_pos, o_cmp_w1, o_cmp_w2, o_w_out, o_norm_ffn, o_router, o_w_gu, o_w_down):
    i = 0
    we = {'norm_mix': e_norm_mix[i], 'w_in_p': _pad_cols(e_w_in[i], _round_up(sum(IN_E_SIZES), LANE)).astype(BF16),
          'q_norm': e_q_norm[i], 'k_norm': e_k_norm[i],
          'sinks': e_sinks[i], 'b_igate': e_b_igate[i], 'b_fgate': e_b_fgate[i], 'h_norm': e_h_norm[i],
          'w_out': e_w_out[i].astype(BF16), 'norm_ffn': e_norm_ffn[i], 'w_gu': e_w_gu[i].astype(BF16),
          'w_down': e_w_down[i].astype(BF16)}
    wo = {'norm_mix': o_norm_mix[i], 'w_in_p': _pad_cols(o_w_in[i], _round_up(sum(IN_O_SIZES), LANE)).astype(BF16),
          'conv_w': o_conv_w[i], 'conv_b': o_conv_b[i],
          'dt_bias': o_dt_bias[i], 'a_log': o_a_log[i], 'd_skip': o_d_skip[i], 'y_norm': o_y_norm[i],
          'q_norm': o_q_norm[i], 'k_norm': o_k_norm[i], 'cmp_pos': o_cmp_pos[i], 'cmp_w1': o_cmp_w1[i],
          'cmp_w2': o_cmp_w2[i], 'w_out': o_w_out[i].astype(BF16), 'norm_ffn': o_norm_ffn[i],
          'router': _pad_cols(o_router[i], LANE), 'w_gu': o_w_gu[i].astype(BF16), 'w_down': o_w_down[i].astype(BF16)}
    st_e = {'a_win': cache_a_win[i], 'b_c': state_b_c[i], 'b_n': state_b_n[i], 'b_m': state_b_m[i]}
    st_o = {'c_ssm': state_c_ssm[i], 'c_conv': state_c_conv[i], 'd_kv': cache_d_kv, 'layer': i,
            'd_win': cache_d_win[i]}

    xp, sp_e = _even_layer(x_prompt, we, None, True)
    xs, ss_e = _even_layer(x_sample, we, st_e, False)

    mix_p, sp_o = _odd_mixers(xp, wo, None, page_table, True)
    mix_s, ss_o = _odd_mixers(xs, wo, st_o, page_table, False)
    xp2, xnp, lgp = _out_proj_route(mix_p[0], mix_p[1], wo['w_out'], xp.reshape(-1, D_MODEL), wo['norm_ffn'], wo['router'])
    xs2, xns, lgs = _out_proj_route(mix_s[0], mix_s[1], wo['w_out'], xs.reshape(-1, D_MODEL), wo['norm_ffn'], wo['router'])
    xn_all = jnp.concatenate([xnp, xns], axis=0)
    lg_all = jnp.concatenate([lgp, lgs], axis=0)[:, :N_EXP]
    y_p, y_s = _moe_layer([xp2, xs2], xn_all, lg_all, wo['w_gu'], wo['w_down'])
    y_prompt = y_p.reshape(x_prompt.shape)
    y_sample = y_s.reshape(x_sample.shape)

    def one(a):
        return a[None]
    return (y_prompt, y_sample, one(sp_e[0]), one(ss_e[0]), one(sp_e[1]), one(ss_e[1]), one(sp_e[2]), one(ss_e[2]),
            one(sp_e[3]), one(ss_e[3]), one(sp_o[0]), one(ss_o[0]), one(sp_o[1]), one(ss_o[1]),
            one(sp_o[2]), one(ss_o[2]), one(sp_o[3]), one(ss_o[3]))
```

```python
import functools
import math

import numpy as np
import jax
import jax.numpy as jnp
from jax import lax
from jax.experimental import pallas as pl
from jax.experimental.pallas import tpu as pltpu

F32 = jnp.float32
BF16 = jnp.bfloat16

D_MODEL = 1024
PAST_LEN = 16384
PAGE_SIZE = 128
HEAD_DIM = 64
QBLOCK = 128
H_A, KVH_A, WINDOW_A = 8, 2, 128
H_B, DK_B, DV_B, CHUNK_B = 4, 64, 128, 64
P_C, H_C, G_C, N_C, CONV_W, CHUNK_C = 64, 8, 2, 128, 4, 128
D_INNER_C = H_C * P_C
CONV_DIM_C = D_INNER_C + 2 * G_C * N_C
H_D, KVH_D = 8, 2
CMP_BLOCK, SLC_BLOCK, TOP_N, WINDOW_D, SLC_QBLOCK = 32, 64, 16, 512, 32
D_FF, N_EXP, TOP_K = 3584, 8, 2
IN_E_SIZES = (H_A * HEAD_DIM, KVH_A * HEAD_DIM, KVH_A * HEAD_DIM, H_B * DK_B, H_B * DK_B, H_B * DV_B, H_B, H_B, H_B * DV_B)
IN_O_SIZES = (D_INNER_C, CONV_DIM_C, H_C, H_D * HEAD_DIM, 6 * KVH_D * HEAD_DIM, 3 * H_D)

LANE = 128
VMEM_LIMIT = 48 * 1024 * 1024
RMS_EPS = 1e-6
FF_TILE = 1792
ROW_TILE = 512


def _round_up(n, m):
    return -(-n // m) * m


def _rms(x, g):
    return x * lax.rsqrt(jnp.mean(x * x, axis=-1, keepdims=True) + RMS_EPS) * g


def _norm_matmul_body(x_ref, g_ref, w_ref, o_ref):
    xn = _rms(x_ref[...], g_ref[...])
    o_ref[...] = jnp.dot(xn.astype(BF16), w_ref[...], preferred_element_type=F32)


def _norm_matmul(x, gain, w):
    m, k = x.shape
    n = w.shape[1]
    tm = min(ROW_TILE, m)
    return pl.pallas_call(
        _norm_matmul_body,
        grid=(m // tm,),
        in_specs=[pl.BlockSpec((tm, k), lambda i: (i, 0)),
                  pl.BlockSpec((1, k), lambda i: (0, 0)),
                  pl.BlockSpec((k, n), lambda i: (0, 0))],
        out_specs=pl.BlockSpec((tm, n), lambda i: (i, 0)),
        out_shape=jax.ShapeDtypeStruct((m, n), F32),
        compiler_params=pltpu.CompilerParams(dimension_semantics=("arbitrary",), vmem_limit_bytes=VMEM_LIMIT),
        name="norm_in_proj",
    )(x, gain.reshape(1, k), w)


def _mix_matmul(a_ref, b_ref, w_ref):
    ka = a_ref.shape[1]
    return (jnp.dot(a_ref[...].astype(BF16), w_ref[0:ka, :], preferred_element_type=F32)
            + jnp.dot(b_ref[...].astype(BF16), w_ref[ka:, :], preferred_element_type=F32))


def _out_proj_body(a_ref, b_ref, w_ref, res_ref, o_ref):
    o_ref[...] = res_ref[...] + _mix_matmul(a_ref, b_ref, w_ref)


def _out_proj(a, b, w, res):
    m, ka = a.shape
    kb = b.shape[1]
    n = w.shape[1]
    tm = min(ROW_TILE, m)
    return pl.pallas_call(
        _out_proj_body,
        grid=(m // tm,),
        in_specs=[pl.BlockSpec((tm, ka), lambda i: (i, 0)),
                  pl.BlockSpec((tm, kb), lambda i: (i, 0)),
                  pl.BlockSpec((ka + kb, n), lambda i: (0, 0)),
                  pl.BlockSpec((tm, n), lambda i: (i, 0))],
        out_specs=pl.BlockSpec((tm, n), lambda i: (i, 0)),
        out_shape=jax.ShapeDtypeStruct((m, n), F32),
        compiler_params=pltpu.CompilerParams(dimension_semantics=("arbitrary",), vmem_limit_bytes=VMEM_LIMIT),
        name="out_proj",
    )(a, b, w, res)


def _out_proj_route_body(a_ref, b_ref, w_ref, res_ref, g_ref, r_ref, o_ref, xn_ref, lg_ref):
    x = res_ref[...] + _mix_matmul(a_ref, b_ref, w_ref)
    o_ref[...] = x
    xn = _rms(x, g_ref[...])
    xn_ref[...] = xn.astype(BF16)
    lg_ref[...] = jnp.dot(xn, r_ref[...], preferred_element_type=F32, precision=lax.Precision.HIGHEST)


def _out_proj_route(a, b, w, res, gain, router_pad):
    m, ka = a.shape
    kb = b.shape[1]
    n = w.shape[1]
    tm = min(ROW_TILE, m)
    return pl.pallas_call(
        _out_proj_route_body,
        grid=(m // tm,),
        in_specs=[pl.BlockSpec((tm, ka), lambda i: (i, 0)),
                  pl.BlockSpec((tm, kb), lambda i: (i, 0)),
                  pl.BlockSpec((ka + kb, n), lambda i: (0, 0)),
                  pl.BlockSpec((tm, n), lambda i: (i, 0)),
                  pl.BlockSpec((1, n), lambda i: (0, 0)),
                  pl.BlockSpec((n, LANE), lambda i: (0, 0))],
        out_specs=[pl.BlockSpec((tm, n), lambda i: (i, 0)),
                   pl.BlockSpec((tm, n), lambda i: (i, 0)),
                   pl.BlockSpec((tm, LANE), lambda i: (i, 0))],
        out_shape=[jax.ShapeDtypeStruct((m, n), F32),
                   jax.ShapeDtypeStruct((m, n), BF16),
                   jax.ShapeDtypeStruct((m, LANE), F32)],
        compiler_params=pltpu.CompilerParams(dimension_semantics=("arbitrary",), vmem_limit_bytes=VMEM_LIMIT),
        name="out_proj_route",
    )(a, b, w, res, gain.reshape(1, n), router_pad)


def _ffn_body(x_ref, g_ref, wg_ref, wu_ref, wd_ref, o_ref, xn_ref, acc_ref):
    c = pl.program_id(1)

    @pl.when(c == 0)
    def _():
        xn_ref[...] = _rms(x_ref[...], g_ref[...]).astype(BF16)
        acc_ref[...] = jnp.zeros_like(acc_ref)

    xn = xn_ref[...]
    gate = jnp.dot(xn, wg_ref[...], preferred_element_type=F32)
    up = jnp.dot(xn, wu_ref[...], preferred_element_type=F32)
    h = (gate * jax.nn.sigmoid(gate) * up).astype(BF16)
    acc_ref[...] += jnp.dot(h, wd_ref[...], preferred_element_type=F32)

    @pl.when(c == pl.num_programs(1) - 1)
    def _():
        o_ref[...] = x_ref[...] + acc_ref[...]


def _ffn(x, gain, w_gu, w_down):
    m, d = x.shape
    tm = min(ROW_TILE, m)
    nc = D_FF // FF_TILE
    return pl.pallas_call(
        _ffn_body,
        grid=(m // tm, nc),
        in_specs=[pl.BlockSpec((tm, d), lambda i, c: (i, 0)),
                  pl.BlockSpec((1, d), lambda i, c: (0, 0)),
                  pl.BlockSpec((d, FF_TILE), lambda i, c: (0, c)),
                  pl.BlockSpec((d, FF_TILE), lambda i, c: (0, c + nc)),
                  pl.BlockSpec((FF_TILE, d), lambda i, c: (c, 0))],
        out_specs=pl.BlockSpec((tm, d), lambda i, c: (i, 0)),
        out_shape=jax.ShapeDtypeStruct((m, d), F32),
        scratch_shapes=[pltpu.VMEM((tm, d), BF16), pltpu.VMEM((tm, d), F32)],
        compiler_params=pltpu.CompilerParams(dimension_semantics=("arbitrary", "arbitrary"), vmem_limit_bytes=VMEM_LIMIT),
        name="ffn_dense",
    )(x, gain.reshape(1, d), w_gu, w_gu, w_down)


def _moe_body(te_ref, tv_ref, x_ref, s_ref, wg_ref, wu_ref, wd_ref, o_ref, acc_ref):
    i = pl.program_id(0)
    c = pl.program_id(1)
    valid = tv_ref[i] > 0

    @pl.when(c == 0)
    def _():
        acc_ref[...] = jnp.zeros_like(acc_ref)

    @pl.when(valid)
    def _():
        x = x_ref[...]
        gate = jnp.dot(x, wg_ref[0], preferred_element_type=F32)
        up = jnp.dot(x, wu_ref[0], preferred_element_type=F32)
        h = (gate * jax.nn.sigmoid(gate) * up).astype(BF16)
        acc_ref[...] += jnp.dot(h, wd_ref[0], preferred_element_type=F32)

    @pl.when(c == pl.num_programs(1) - 1)
    def _():
        o_ref[...] = s_ref[...] * acc_ref[...]


def _moe_grouped(xs, scale, tile_expert, tile_valid, w_gu, w_down, tm):
    p, d = xs.shape
    nc = D_FF // FF_TILE
    grid_spec = pltpu.PrefetchScalarGridSpec(
        num_scalar_prefetch=2,
        grid=(p // tm, nc),
        in_specs=[pl.BlockSpec((tm, d), lambda i, c, te, tv: (i, 0)),
                  pl.BlockSpec((tm, 1), lambda i, c, te, tv: (i, 0)),
                  pl.BlockSpec((1, d, FF_TILE), lambda i, c, te, tv: (te[i], 0, c)),
                  pl.BlockSpec((1, d, FF_TILE), lambda i, c, te, tv: (te[i], 0, c + nc)),
                  pl.BlockSpec((1, FF_TILE, d), lambda i, c, te, tv: (te[i], c, 0))],
        out_specs=pl.BlockSpec((tm, d), lambda i, c, te, tv: (i, 0)),
        scratch_shapes=[pltpu.VMEM((tm, d), F32)],
    )
    return pl.pallas_call(
        _moe_body,
        grid_spec=grid_spec,
        out_shape=jax.ShapeDtypeStruct((p, d), F32),
        compiler_params=pltpu.CompilerParams(dimension_semantics=("arbitrary", "arbitrary"), vmem_limit_bytes=VMEM_LIMIT),
        name="moe_grouped",
    )(tile_expert, tile_valid, xs, scale, w_gu, w_gu, w_down)


def _moe_route(logits, tm):
    n = logits.shape[0]
    top_v, top_i = lax.top_k(logits, TOP_K)
    gate = jax.nn.softmax(top_v, axis=-1)
    flat_e = top_i.reshape(-1).astype(jnp.int32)
    na = n * TOP_K
    n_tiles = na // tm + N_EXP
    p = n_tiles * tm
    n_pad = p - na
    counts = jnp.sum(jax.nn.one_hot(flat_e, N_EXP, dtype=jnp.int32), axis=0)
    tiles_per_e = (counts + tm - 1) // tm
    cum_pad = jnp.cumsum(tiles_per_e * tm - counts)
    d = jnp.arange(n_pad, dtype=jnp.int32)
    pad_e = jnp.sum((d[:, None] >= cum_pad[None, :]).astype(jnp.int32), axis=1)
    keys = jnp.concatenate([2 * flat_e, 2 * pad_e + 1])
    ids = jnp.arange(p, dtype=jnp.int32)
    gates_all = jnp.concatenate([gate.reshape(-1), jnp.zeros((n_pad,), F32)])
    _, id_sorted, gate_of_pos = lax.sort((keys, ids, gates_all), num_keys=1, is_stable=True)
    tok_of_pos = jnp.where(id_sorted < na, id_sorted // TOP_K, 0)
    _, pos_by_id = lax.sort((id_sorted, ids), num_keys=1)
    pos_of_a = pos_by_id[:na].reshape(n, TOP_K)
    tile_end = jnp.cumsum(tiles_per_e)
    total_tiles = tile_end[-1]
    tile_ids = jnp.arange(n_tiles, dtype=jnp.int32)
    tile_expert = jnp.searchsorted(tile_end, jnp.minimum(tile_ids, total_tiles - 1), side="right").astype(jnp.int32)
    tile_expert = jnp.clip(tile_expert, 0, N_EXP - 1)
    tile_valid = (tile_ids < total_tiles).astype(jnp.int32)
    return tok_of_pos, gate_of_pos, pos_of_a, tile_expert, tile_valid


def _moe_layer(x_parts, xn_rows, logits, w_gu, w_down, tm=ROW_TILE):
    tok_of_pos, gate_of_pos, pos_of_a, tile_expert, tile_valid = _moe_route(logits, tm)
    xs = xn_rows[tok_of_pos]
    ys = _moe_grouped(xs, gate_of_pos.reshape(-1, 1), tile_expert, tile_valid, w_gu, w_down, tm)
    outs, r0 = [], 0
    for xp in x_parts:
        pa = pos_of_a[r0:r0 + xp.shape[0]]
        outs.append(xp + (ys[pa[:, 0]] + ys[pa[:, 1]]))
        r0 += xp.shape[0]
    return outs


QT = 128
KT = 512
NEG = -1e30
RANK_GROUP = 16
LOG2E = 1.4426950408889634


def _heads_to_rows(qt):
    r = qt.shape[1] // HEAD_DIM
    return jnp.concatenate([qt[:, h * HEAD_DIM:(h + 1) * HEAD_DIM] for h in range(r)], axis=0)


def _rows_to_heads(o, r):
    return jnp.concatenate([o[h * QT:(h + 1) * QT] for h in range(r)], axis=-1)


def _cmp_select_body(q_ref, kc_ref, vc_ref, o_ref, sel_ref, rank_ref, *, n_top):
    qi = pl.program_id(2)
    r = q_ref.shape[2] // HEAD_DIM
    q = _heads_to_rows(q_ref[0])
    s = lax.dot_general(q.astype(BF16), kc_ref[0, 0].astype(BF16), (((1,), (1,)), ((), ())),
                        preferred_element_type=F32) * (HEAD_DIM ** -0.5)
    row = lax.broadcasted_iota(jnp.int32, (r * QT, LANE), 0)
    lane = lax.broadcasted_iota(jnp.int32, (r * QT, LANE), 1)
    t_pos = qi * QT + (row & (QT - 1))
    s = jnp.where(lane * CMP_BLOCK + (CMP_BLOCK - 1) <= t_pos, s, -jnp.inf)
    m = jnp.max(s, axis=-1, keepdims=True)
    m = jnp.where(m > -jnp.inf, m, 0.0)
    p = jnp.exp(s - m)
    den = jnp.sum(p, axis=-1, keepdims=True)
    p = p / jnp.where(den > 0, den, 1.0)
    o = jnp.dot(p.astype(BF16), vc_ref[0, 0].astype(BF16), preferred_element_type=F32)
    o_ref[0] = _rows_to_heads(o, r)
    pt = p[0:QT]
    for h in range(1, r):
        pt = pt + p[h * QT:(h + 1) * QT]
    lane1 = lax.broadcasted_iota(jnp.int32, (QT, LANE), 1)
    row1 = lax.broadcasted_iota(jnp.int32, (QT, LANE), 0)
    even = (lane1 & 1) == 0
    lo = jnp.where(even, pt, pltpu.roll(pt, 1, 1))
    hi = jnp.where(even, pltpu.roll(pt, LANE - 1, 1), pt)
    imp = lo + hi
    blk = lane1 >> 1
    cur = (qi * QT + row1) >> 6
    forced = (blk == 0) | (blk == cur) | (blk == cur - 1)
    score = jnp.where(blk > cur, -jnp.inf, jnp.where(forced, jnp.inf, imp))
    lane_f = lane1.astype(F32)
    rank_ref[...] = jnp.zeros((QT, LANE), F32)
    for grp in range(0, LANE // 2, RANK_GROUP):
        @pl.when(2 * qi + 1 >= grp)
        def _():
            cnt = rank_ref[...]
            for j in range(grp, grp + RANK_GROUP):
                col = jnp.broadcast_to(score[:, 2 * j:2 * j + 1], (QT, LANE))
                later = jnp.where(lane_f >= 2.0 * j + 2.0, 1.0, 0.0)
                cnt = cnt + jnp.where(col > score, 1.0, 0.0) + jnp.where(col == score, later, 0.0)
            rank_ref[...] = cnt
    sel_ref[0, 0] = jnp.where((rank_ref[...] < n_top) & (blk <= cur), 1.0, 0.0)


def _cmp_select(q, kcmp, vcmp, n_top):
    b, t, w = q.shape
    g, d = kcmp.shape[1], kcmp.shape[3]
    gw = w // g
    assert kcmp.shape == (b, g, LANE, d) and t // CMP_BLOCK == LANE
    return pl.pallas_call(
        functools.partial(_cmp_select_body, n_top=n_top),
        grid=(b, g, t // QT),
        in_specs=[pl.BlockSpec((1, QT, gw), lambda i, j, k: (i, k, j)),
                  pl.BlockSpec((1, 1, LANE, d), lambda i, j, k: (i, j, 0, 0)),
                  pl.BlockSpec((1, 1, LANE, d), lambda i, j, k: (i, j, 0, 0))],
        out_specs=[pl.BlockSpec((1, QT, gw), lambda i, j, k: (i, k, j)),
                   pl.BlockSpec((1, 1, QT, LANE), lambda i, j, k: (i, j, k, 0))],
        out_shape=[jax.ShapeDtypeStruct((b, t, w), F32),
                   jax.ShapeDtypeStruct((b, g, t, LANE), F32)],
        scratch_shapes=[pltpu.VMEM((QT, LANE), F32)],
        compiler_params=pltpu.CompilerParams(dimension_semantics=("arbitrary",) * 3, vmem_limit_bytes=VMEM_LIMIT),
        name="nsa_cmp_select",
    )(q, kcmp, vcmp)


def _slc_attend_body(q_ref, sel_ref, k_ref, v_ref, o_ref):
    qi = pl.program_id(2)
    r = q_ref.shape[2] // HEAD_DIM
    rows = r * QT
    q = (_heads_to_rows(q_ref[0]) * (HEAD_DIM ** -0.5 * LOG2E)).astype(BF16)
    sel = sel_ref[0, 0]
    e_row = lax.broadcasted_iota(jnp.int32, (LANE, KT), 0)
    e_col = lax.broadcasted_iota(jnp.int32, (LANE, KT), 1)
    expand = jnp.where(e_row == 2 * (e_col >> 6), 1.0, 0.0).astype(BF16)
    t_pos = qi * QT + lax.broadcasted_iota(jnp.int32, (QT, KT), 0)
    k_off = lax.broadcasted_iota(jnp.int32, (QT, KT), 1)
    n_kt = (qi * QT + QT - 1) // KT + 1
    lanes_per_kt = 2 * (KT // SLC_BLOCK)
    ones = jnp.ones((KT, LANE - HEAD_DIM), BF16)

    def tile_scores(kt):
        start = pl.multiple_of(kt * KT, KT)
        k = k_ref[0, 0, pl.ds(start, KT), :].astype(BF16)
        v = v_ref[0, 0, pl.ds(start, KT), :].astype(BF16)
        v1 = jnp.concatenate([v, ones], axis=-1)
        s = lax.dot_general(q, k, (((1,), (1,)), ((), ())), preferred_element_type=F32)
        sel_kt = pltpu.roll(sel, (LANE - kt * lanes_per_kt) % LANE, 1).astype(BF16)
        picked = jnp.dot(sel_kt, expand, preferred_element_type=F32)
        bias = jnp.where((picked > 0.5) & (start + k_off <= t_pos), 0.0, NEG)
        return s + jnp.concatenate([bias] * r, axis=0), v1

    def fold(carry, s, v1):
        m, acc = carry
        m_new = jnp.maximum(m, jnp.max(s, axis=-1, keepdims=True))
        alpha = jnp.exp2(m - m_new)
        p = jnp.exp2(s - m_new).astype(BF16)
        return m_new, alpha * acc + jnp.dot(p, v1, preferred_element_type=F32)

    def body(i, carry):
        s0, v0 = tile_scores(2 * i)
        s1, v1 = tile_scores(2 * i + 1)
        return fold(fold(carry, s0, v0), s1, v1)

    init = (jnp.full((rows, 1), NEG, F32), jnp.zeros((rows, LANE), F32))
    m, acc = lax.fori_loop(0, (n_kt + 1) // 2, body, init)
    o_ref[0] = _rows_to_heads(acc[:, :HEAD_DIM] / acc[:, HEAD_DIM:HEAD_DIM + 1], r)


def _slc_attend(q, sel, ks, vs):
    b, t, w = q.shape
    g, d = ks.shape[1], ks.shape[3]
    gw = w // g
    return pl.pallas_call(
        _slc_attend_body,
        grid=(b, g, t // QT),
        in_specs=[pl.BlockSpec((1, QT, gw), lambda i, j, k: (i, k, j)),
                  pl.BlockSpec((1, 1, QT, LANE), lambda i, j, k: (i, j, k, 0)),
                  pl.BlockSpec((1, 1, t, d), lambda i, j, k: (i, j, 0, 0)),
                  pl.BlockSpec((1, 1, t, d), lambda i, j, k: (i, j, 0, 0))],
        out_specs=pl.BlockSpec((1, QT, gw), lambda i, j, k: (i, k, j)),
        out_shape=jax.ShapeDtypeStruct((b, t, w), F32),
        compiler_params=pltpu.CompilerParams(dimension_semantics=("arbitrary",) * 3, vmem_limit_bytes=VMEM_LIMIT),
        name="nsa_slc_attend",
    )(q, sel, ks, vs)


CMP_HIDDEN = 128
CMP_W = 2 * KVH_D * HEAD_DIM
SLC_W = 2 * KVH_D * HEAD_DIM
BLK_PER_PAGE = PAGE_SIZE // CMP_BLOCK
PAGES_PER_STEP = 64


def _page_copy(pool_ref, buf_ref, sem_ref, layer, page, slot, j):
    return pltpu.make_async_copy(pool_ref.at[layer, page, pl.ds(0, CMP_W), :], buf_ref.at[slot, j], sem_ref.at[slot])


def _compress_pages_body(pt_ref, pool_ref, post_ref, perm_ref, w1_ref, w2_ref, o_ref, buf_ref, xk_ref, xv_ref, sem_ref, *,
                         layer, n_steps):
    x_ref = (xk_ref, xv_ref)
    s = pl.program_id(0)
    slot = s % 2

    def start(step, slot_):
        for j in range(PAGES_PER_STEP):
            _page_copy(pool_ref, buf_ref, sem_ref, layer, pt_ref[step * PAGES_PER_STEP + j], slot_, j).start()

    @pl.when(s == 0)
    def _():
        start(0, 0)

    @pl.when(s + 1 < n_steps)
    def _():
        start(s + 1, 1 - slot)

    for j in range(PAGES_PER_STEP):
        _page_copy(pool_ref, buf_ref, sem_ref, layer, 0, slot, j).wait()

    for pp in range(PAGES_PER_STEP // 2):
        for br in range(2):
            t0 = buf_ref[slot, 2 * pp, br * LANE:(br + 1) * LANE, :] + post_ref[br]
            t1 = buf_ref[slot, 2 * pp + 1, br * LANE:(br + 1) * LANE, :] + post_ref[br]
            tc = jnp.concatenate([t0, t1], axis=-1).astype(BF16)
            tp = jnp.dot(tc, perm_ref[...], preferred_element_type=F32)
            x_ref[br][pp] = tp.T

    m = PAGES_PER_STEP * BLK_PER_PAGE
    for br in range(2):
        acc = jnp.zeros((m, KVH_D * CMP_HIDDEN), F32)
        for r in range(0, CMP_BLOCK, 2):
            x = jnp.concatenate([x_ref[br][:, rr * 8:(rr + 1) * 8, :].reshape(m, LANE) for rr in (r, r + 1)], axis=-1)
            acc = acc + jnp.dot(x.astype(BF16), w1_ref[br, r // 2], preferred_element_type=F32)
        h = acc * jax.nn.sigmoid(acc)
        o_ref[:, br * LANE:(br + 1) * LANE] = jnp.dot(h.astype(BF16), w2_ref[br], preferred_element_type=F32)


def _block_diag2(w):
    z = jnp.zeros_like(w)
    return jnp.concatenate([jnp.concatenate([w, z], axis=-1), jnp.concatenate([z, w], axis=-1)], axis=-2)


def _compress_pages(pool, page_table, cmp_pos, cmp_w1, cmp_w2, layer):
    bsz, n_pages = page_table.shape
    total = bsz * n_pages
    n_steps = total // PAGES_PER_STEP
    assert total % PAGES_PER_STEP == 0 and KVH_D == 2
    w1 = cmp_w1.reshape(2, CMP_BLOCK, HEAD_DIM, CMP_HIDDEN)
    w1_bd = _block_diag2(w1).astype(BF16).reshape(2, CMP_BLOCK // 2, 2 * LANE, KVH_D * CMP_HIDDEN)
    w2_bd = _block_diag2(cmp_w2).astype(BF16)
    pos_t = jnp.tile(jnp.transpose(cmp_pos, (0, 2, 1)), (1, KVH_D, BLK_PER_PAGE))
    lane_in = np.arange(2 * PAGE_SIZE)
    page2, blk, r = lane_in // PAGE_SIZE, (lane_in % PAGE_SIZE) // CMP_BLOCK, lane_in % CMP_BLOCK
    perm = np.zeros((2 * PAGE_SIZE, 2 * PAGE_SIZE), np.float32)
    perm[lane_in, r * (2 * BLK_PER_PAGE) + page2 * BLK_PER_PAGE + blk] = 1.0
    perm = jnp.asarray(perm, BF16)
    m = PAGES_PER_STEP * BLK_PER_PAGE
    grid_spec = pltpu.PrefetchScalarGridSpec(
        num_scalar_prefetch=1,
        grid=(n_steps,),
        in_specs=[pl.BlockSpec(memory_space=pl.ANY),
                  pl.BlockSpec((2, LANE, PAGE_SIZE), lambda s, pt: (0, 0, 0)),
                  pl.BlockSpec((2 * PAGE_SIZE, 2 * PAGE_SIZE), lambda s, pt: (0, 0)),
                  pl.BlockSpec((2, CMP_BLOCK // 2, 2 * LANE, KVH_D * CMP_HIDDEN), lambda s, pt: (0, 0, 0, 0)),
                  pl.BlockSpec((2, KVH_D * CMP_HIDDEN, LANE), lambda s, pt: (0, 0, 0))],
        out_specs=pl.BlockSpec((m, CMP_W), lambda s, pt: (s, 0)),
        scratch_shapes=[pltpu.VMEM((2, PAGES_PER_STEP, CMP_W, PAGE_SIZE), F32),
                        pltpu.VMEM((PAGES_PER_STEP // 2, 2 * PAGE_SIZE, LANE), F32),
                        pltpu.VMEM((PAGES_PER_STEP // 2, 2 * PAGE_SIZE, LANE), F32),
                        pltpu.SemaphoreType.DMA((2,))],
    )
    return pl.pallas_call(
        functools.partial(_compress_pages_body, layer=layer, n_steps=n_steps),
        grid_spec=grid_spec,
        out_shape=jax.ShapeDtypeStruct((total * BLK_PER_PAGE, CMP_W), F32),
        compiler_params=pltpu.CompilerParams(dimension_semantics=("arbitrary",), vmem_limit_bytes=VMEM_LIMIT),
        name="nsa_compress_pages",
    )(page_table.reshape(-1), pool, pos_t, perm, w1_bd, w2_bd)


def _compress_prompt_body(xk_ref, xv_ref, pos_ref, w1_ref, w2_ref, o_ref):
    x_ref = (xk_ref, xv_ref)
    nb = xk_ref.shape[0] // CMP_BLOCK
    for br in range(2):
        acc = jnp.zeros((nb, KVH_D * CMP_HIDDEN), F32)
        for r in range(CMP_BLOCK):
            x = x_ref[br][pl.ds(r, nb, stride=CMP_BLOCK), :] + pos_ref[br, r:r + 1, :]
            acc = acc + jnp.dot(x.astype(BF16), w1_ref[br, r], preferred_element_type=F32)
        h = acc * jax.nn.sigmoid(acc)
        o_ref[:, br * LANE:(br + 1) * LANE] = jnp.dot(h.astype(BF16), w2_ref[br], preferred_element_type=F32)


def _compress_prompt(rows, cmp_pos, cmp_w1, cmp_w2):
    b, t, w = rows.shape
    nb = t // CMP_BLOCK
    w1 = cmp_w1.reshape(2, CMP_BLOCK, HEAD_DIM, CMP_HIDDEN)
    w1_bd = _block_diag2(w1).astype(BF16)
    w2_bd = _block_diag2(cmp_w2).astype(BF16)
    pos = jnp.concatenate([cmp_pos, cmp_pos], axis=-1)
    return pl.pallas_call(
        _compress_prompt_body,
        grid=(b,),
        in_specs=[pl.BlockSpec((None, t, LANE), lambda i: (i, 0, 0)),
                  pl.BlockSpec((None, t, LANE), lambda i: (i, 0, 1)),
                  pl.BlockSpec((2, CMP_BLOCK, LANE), lambda i: (0, 0, 0)),
                  pl.BlockSpec((2, CMP_BLOCK, LANE, KVH_D * CMP_HIDDEN), lambda i: (0, 0, 0, 0)),
                  pl.BlockSpec((2, KVH_D * CMP_HIDDEN, LANE), lambda i: (0, 0, 0))],
        out_specs=pl.BlockSpec((None, nb, w), lambda i: (i, 0, 0)),
        out_shape=jax.ShapeDtypeStruct((b, nb, w), F32),
        compiler_params=pltpu.CompilerParams(dimension_semantics=("arbitrary",), vmem_limit_bytes=VMEM_LIMIT),
        name="nsa_compress_prompt",
    )(rows, rows, pos, w1_bd, w2_bd)


def _slc_copy(pool_ref, buf_ref, sem_ref, layer, page, feat0, slot, idx):
    return pltpu.make_async_copy(pool_ref.at[layer, page, pl.ds(feat0, HEAD_DIM), :], buf_ref.at[slot, idx], sem_ref.at[slot])


def _slc_step_body(pt_ref, sel_ref, pool_ref, q_ref, kn_ref, vn_ref, o_ref, kbuf_ref, vbuf_ref, sem_ref, *,
                   layer, n_seq, n_pages, new_blk):
    b = pl.program_id(0)
    slot = b % 2
    n_sel = KVH_D * TOP_N

    def copies(seq, slot_, from_table):
        out = []
        for g in range(KVH_D):
            for i in range(TOP_N):
                idx = g * TOP_N + i
                page = 0
                if from_table:
                    blk = jnp.minimum(sel_ref[seq * n_sel + idx], new_blk - 1)
                    page = pt_ref[seq * n_pages + (blk >> 1)]
                out.append(_slc_copy(pool_ref, kbuf_ref, sem_ref, layer, page, CMP_W + g * HEAD_DIM, slot_, idx))
                out.append(_slc_copy(pool_ref, vbuf_ref, sem_ref, layer, page, CMP_W + (KVH_D + g) * HEAD_DIM, slot_, idx))
        return out

    @pl.when(b == 0)
    def _():
        for c in copies(0, 0, True):
            c.start()

    @pl.when(b + 1 < n_seq)
    def _():
        for c in copies(b + 1, 1 - slot, True):
            c.start()

    for c in copies(0, slot, False):
        c.wait()

    half = lax.broadcasted_iota(jnp.int32, (8, LANE), 1) >> 6
    for g in range(KVH_D):
        q = q_ref[0, g] * (HEAD_DIM ** -0.5)
        qb = q.astype(BF16)
        s_parts = []
        for i in range(TOP_N):
            blk = sel_ref[b * n_sel + g * TOP_N + i]
            s_i = jnp.dot(qb, kbuf_ref[slot, g * TOP_N + i].astype(BF16), preferred_element_type=F32)
            ok = (half == (blk & 1)) & (blk < new_blk)
            s_parts.append(jnp.where(ok, s_i, NEG))
        s = jnp.concatenate(s_parts, axis=-1)
        s_new = jnp.sum(q * kn_ref[0, g], axis=-1, keepdims=True)
        m = jnp.maximum(jnp.max(s, axis=-1, keepdims=True), s_new)
        p = jnp.exp(s - m)
        p_new = jnp.exp(s_new - m)
        den = jnp.sum(p, axis=-1, keepdims=True) + p_new
        o = p_new * vn_ref[0, g]
        for i in range(TOP_N):
            p_i = p[:, i * LANE:(i + 1) * LANE].astype(BF16)
            o = o + lax.dot_general(p_i, vbuf_ref[slot, g * TOP_N + i].astype(BF16), (((1,), (1,)), ((), ())),
                                    preferred_element_type=F32)
        o_ref[0, g] = o / den


def _slc_step(pool, page_table, sel, q, k_new, v_new, layer):
    bsz, n_pages = page_table.shape
    n_sel = KVH_D * TOP_N
    grid_spec = pltpu.PrefetchScalarGridSpec(
        num_scalar_prefetch=2,
        grid=(bsz,),
        in_specs=[pl.BlockSpec(memory_space=pl.ANY),
                  pl.BlockSpec((1, KVH_D, 8, HEAD_DIM), lambda b, pt, sl: (b, 0, 0, 0)),
                  pl.BlockSpec((1, KVH_D, 1, HEAD_DIM), lambda b, pt, sl: (b, 0, 0, 0)),
                  pl.BlockSpec((1, KVH_D, 1, HEAD_DIM), lambda b, pt, sl: (b, 0, 0, 0))],
        out_specs=pl.BlockSpec((1, KVH_D, 8, HEAD_DIM), lambda b, pt, sl: (b, 0, 0, 0)),
        scratch_shapes=[pltpu.VMEM((2, n_sel, HEAD_DIM, PAGE_SIZE), F32),
                        pltpu.VMEM((2, n_sel, HEAD_DIM, PAGE_SIZE), F32),
                        pltpu.SemaphoreType.DMA((2,))],
    )
    return pl.pallas_call(
        functools.partial(_slc_step_body, layer=layer, n_seq=bsz, n_pages=n_pages, new_blk=2 * n_pages),
        grid_spec=grid_spec,
        out_shape=jax.ShapeDtypeStruct((bsz, KVH_D, 8, HEAD_DIM), F32),
        compiler_params=pltpu.CompilerParams(dimension_semantics=("arbitrary",), vmem_limit_bytes=VMEM_LIMIT),
        name="nsa_slc_step",
    )(page_table.reshape(-1), sel.reshape(-1), pool, q, k_new, v_new)


WIN_TILES = 4


def _window_attend_body(sink_ref, q_ref, k_ref, v_ref, o_ref, *, window, use_sinks):
    g = pl.program_id(1)
    r = q_ref.shape[2] // HEAD_DIM
    rows = r * QT
    span = window + QT
    row = lax.broadcasted_iota(jnp.int32, (rows, span), 0)
    col = lax.broadcasted_iota(jnp.int32, (rows, span), 1)
    if use_sinks:
        head = lax.broadcasted_iota(jnp.int32, (rows, 1), 0) // QT
        sink = jnp.zeros((rows, 1), F32)
        for h in range(r):
            sink = jnp.where(head == h, sink_ref[g * r + h], sink)
    for j in range(WIN_TILES):
        qi = pl.program_id(2) * WIN_TILES + j
        q = (_heads_to_rows(q_ref[0, j * QT:(j + 1) * QT, :]) * (HEAD_DIM ** -0.5)).astype(BF16)
        start = pl.multiple_of(jnp.maximum(qi * QT - window, 0), QT)
        k = k_ref[0, 0, pl.ds(start, span), :].astype(BF16)
        v = v_ref[0, 0, pl.ds(start, span), :].astype(BF16)
        s = lax.dot_general(q, k, (((1,), (1,)), ((), ())), preferred_element_type=F32)
        diff = qi * QT + (row & (QT - 1)) - (start + col)
        s = jnp.where((diff >= 0) & (diff < window), s, -jnp.inf)
        m = jnp.max(s, axis=-1, keepdims=True)
        if use_sinks:
            m = jnp.maximum(m, sink)
        p = jnp.exp(s - m)
        den = jnp.sum(p, axis=-1, keepdims=True)
        if use_sinks:
            den = den + jnp.exp(sink - m)
        o = jnp.dot(p.astype(BF16), v, preferred_element_type=F32) / den
        o_ref[0, j * QT:(j + 1) * QT, :] = _rows_to_heads(o, r)


def _window_attend(q, k, v, window, sinks=None):
    b, t, w = q.shape
    g, d = k.shape[1], k.shape[3]
    gw = w // g
    use_sinks = sinks is not None
    if sinks is None:
        sinks = jnp.zeros((w // d,), F32)
    tq = WIN_TILES * QT
    grid_spec = pltpu.PrefetchScalarGridSpec(
        num_scalar_prefetch=1,
        grid=(b, g, t // tq),
        in_specs=[pl.BlockSpec((1, tq, gw), lambda i, j, n, sk: (i, n, j)),
                  pl.BlockSpec((1, 1, t, d), lambda i, j, n, sk: (i, j, 0, 0)),
                  pl.BlockSpec((1, 1, t, d), lambda i, j, n, sk: (i, j, 0, 0))],
        out_specs=pl.BlockSpec((1, tq, gw), lambda i, j, n, sk: (i, n, j)),
    )
    return pl.pallas_call(
        functools.partial(_window_attend_body, window=window, use_sinks=use_sinks),
        grid_spec=grid_spec,
        out_shape=jax.ShapeDtypeStruct((b, t, w), F32),
        compiler_params=pltpu.CompilerParams(dimension_semantics=("arbitrary",) * 3, vmem_limit_bytes=VMEM_LIMIT),
        name="window_attend_%d" % window,
    )(sinks.astype(F32), q, k, v)


MLSTM_CHUNK = 128
MLSTM_NB = 2
HIGHEST = lax.Precision.HIGHEST


def _mlstm_body(q_ref, kt_ref, v_ref, g_ref, ob_ref, hn_ref, o_ref, c_out, n_out, m_out, c_ref, n_ref, m_ref):
    ci = pl.program_id(1)
    L = MLSTM_CHUNK

    @pl.when(ci == 0)
    def _():
        c_ref[...] = jnp.zeros_like(c_ref)
        n_ref[...] = jnp.zeros_like(n_ref)
        m_ref[...] = jnp.zeros_like(m_ref)

    row = lax.broadcasted_iota(jnp.int32, (L, L), 0)
    col = lax.broadcasted_iota(jnp.int32, (L, L), 1)
    causal = col <= row
    upper = jnp.where(row <= col, 1.0, 0.0)
    for bb, h in [(bb, h) for bb in range(MLSTM_NB) for h in range(H_B)]:
        q = q_ref[bb, :, h * DK_B:(h + 1) * DK_B]
        kt = kt_ref[bb, h * DK_B:(h + 1) * DK_B, :] * (DK_B ** -0.5)
        v = v_ref[bb, :, h * DV_B:(h + 1) * DV_B]
        ig = g_ref[bb, h:h + 1, :]
        lf = g_ref[bb, H_B + h:H_B + h + 1, :]
        m_prev = m_ref[bb, h, 0:1, 0:1]
        f_col = jnp.sum(jnp.where(causal, jnp.broadcast_to(lf, (L, L)), 0.0), axis=-1, keepdims=True)
        f_row = jnp.dot(jnp.broadcast_to(lf, (8, L)), upper, preferred_element_type=F32, precision=HIGHEST)[0:1]
        log_d = jnp.where(causal, f_col - f_row + ig, -jnp.inf)
        m_inter = f_col + m_prev
        m_t = jnp.maximum(m_inter, jnp.max(log_d, axis=-1, keepdims=True))
        d_mat = jnp.exp(log_d - m_t)
        w_inter = jnp.exp(m_inter - m_t)
        qb = q.astype(BF16)
        qk = jnp.dot(qb, kt.astype(BF16), preferred_element_type=F32) * d_mat
        num = jnp.dot(qk.astype(BF16), v.astype(BF16), preferred_element_type=F32)
        num = num + w_inter * jnp.dot(qb, c_ref[bb, h].astype(BF16), preferred_element_type=F32)
        qn = jnp.dot(qb, n_ref[bb, h].astype(BF16), preferred_element_type=F32)[:, 0:1]
        den = jnp.sum(qk, axis=-1, keepdims=True) + w_inter * qn
        hh = num / jnp.maximum(jnp.abs(den), jnp.exp(-m_t))
        hh = hh * lax.rsqrt(jnp.mean(hh * hh, axis=-1, keepdims=True) + RMS_EPS) * hn_ref[...]
        o_ref[bb, :, h * DV_B:(h + 1) * DV_B] = hh * jax.nn.sigmoid(ob_ref[bb, :, h * DV_B:(h + 1) * DV_B])
        m_new = m_t[L - 1:L, :]
        f_last = f_col[L - 1:L, :]
        w_end = jnp.exp(f_last - f_row + ig - m_new)
        decay = jnp.exp(f_last + m_prev - m_new)
        ktw = kt * w_end
        c_ref[bb, h] = decay * c_ref[bb, h] + jnp.dot(ktw.astype(BF16), v.astype(BF16), preferred_element_type=F32)
        n_ref[bb, h] = decay * n_ref[bb, h] + jnp.sum(ktw, axis=-1, keepdims=True)
        m_ref[bb, h] = jnp.broadcast_to(m_new, m_ref.shape[2:])

    @pl.when(ci == pl.num_programs(1) - 1)
    def _():
        c_out[...] = c_ref[...]
        n_out[...] = n_ref[...]
        m_out[...] = m_ref[...]


def _mlstm_prompt(q, kt, v, gates, ob, h_norm):
    b, t, _ = q.shape
    L = MLSTM_CHUNK
    nc = t // L
    return pl.pallas_call(
        _mlstm_body,
        grid=(b // MLSTM_NB, nc),
        in_specs=[pl.BlockSpec((MLSTM_NB, L, H_B * DK_B), lambda i, c: (i, c, 0)),
                  pl.BlockSpec((MLSTM_NB, H_B * DK_B, L), lambda i, c: (i, 0, c)),
                  pl.BlockSpec((MLSTM_NB, L, H_B * DV_B), lambda i, c: (i, c, 0)),
                  pl.BlockSpec((MLSTM_NB, 2 * H_B, L), lambda i, c: (i, 0, c)),
                  pl.BlockSpec((MLSTM_NB, L, H_B * DV_B), lambda i, c: (i, c, 0)),
                  pl.BlockSpec((1, DV_B), lambda i, c: (0, 0))],
        out_specs=[pl.BlockSpec((MLSTM_NB, L, H_B * DV_B), lambda i, c: (i, c, 0)),
                   pl.BlockSpec((MLSTM_NB, H_B, DK_B, DV_B), lambda i, c: (i, 0, 0, 0)),
                   pl.BlockSpec((MLSTM_NB, H_B, DK_B, LANE), lambda i, c: (i, 0, 0, 0)),
                   pl.BlockSpec((MLSTM_NB, H_B, 8, LANE), lambda i, c: (i, 0, 0, 0))],
        out_shape=[jax.ShapeDtypeStruct((b, t, H_B * DV_B), F32),
                   jax.ShapeDtypeStruct((b, H_B, DK_B, DV_B), F32),
                   jax.ShapeDtypeStruct((b, H_B, DK_B, LANE), F32),
                   jax.ShapeDtypeStruct((b, H_B, 8, LANE), F32)],
        scratch_shapes=[pltpu.VMEM((MLSTM_NB, H_B, DK_B, DV_B), F32), pltpu.VMEM((MLSTM_NB, H_B, DK_B, LANE), F32),
                        pltpu.VMEM((MLSTM_NB, H_B, 8, LANE), F32)],
        compiler_params=pltpu.CompilerParams(dimension_semantics=("arbitrary", "arbitrary"), vmem_limit_bytes=VMEM_LIMIT),
        name="mlstm_prompt",
    )(q, kt, v, gates, ob, h_norm.reshape(1, DV_B))


SSD_CHUNK = 128
CONV_PAD = 8


def _ssd_body(z_ref, xr_ref, bcr_ref, dtc_ref, dtr_ref, an_ref, cw_ref, cb_ref, ds_ref, yn_ref,
              o_ref, h_out, xp_ref, h_ref):
    ci = pl.program_id(1)
    L = SSD_CHUNK
    r = H_C // G_C

    @pl.when(ci == 0)
    def _():
        xp_ref[0:CONV_PAD, :] = jnp.zeros((CONV_PAD, CONV_DIM_C), F32)
        h_ref[...] = jnp.zeros_like(h_ref)

    xp_ref[CONV_PAD:CONV_PAD + L, 0:D_INNER_C] = xr_ref[0]
    xp_ref[CONV_PAD:CONV_PAD + L, D_INNER_C:CONV_DIM_C] = bcr_ref[0]
    conv = cb_ref[...]
    for j in range(CONV_W):
        off = CONV_PAD - (CONV_W - 1) + j
        conv = conv + xp_ref[off:off + L, :] * cw_ref[j:j + 1, :]
    tail = xp_ref[L:L + CONV_PAD, :]
    xp_ref[0:CONV_PAD, :] = tail
    xbc = conv * jax.nn.sigmoid(conv)
    x = xbc[:, 0:D_INNER_C]
    bm = xbc[:, D_INNER_C:D_INNER_C + G_C * N_C]
    cm = xbc[:, D_INNER_C + G_C * N_C:CONV_DIM_C]

    row = lax.broadcasted_iota(jnp.int32, (L, L), 0)
    col = lax.broadcasted_iota(jnp.int32, (L, L), 1)
    causal = col <= row
    lower = jnp.where(causal, 1.0, 0.0)
    upper = jnp.where(row <= col, 1.0, 0.0)
    dt_c = dtc_ref[0]
    dt_r = dtr_ref[0]
    a_c = dt_c * an_ref[0:1, 0:H_C]
    a_r = dt_r * an_ref[:, H_C:H_C + 1]
    cum_c = jnp.dot(lower, a_c, preferred_element_type=F32, precision=HIGHEST)
    cum_r = jnp.dot(a_r, upper, preferred_element_type=F32, precision=HIGHEST)
    ys = []
    for g in range(G_C):
        bg = bm[:, g * N_C:(g + 1) * N_C].astype(BF16)
        cg = cm[:, g * N_C:(g + 1) * N_C].astype(BF16)
        cbm = lax.dot_general(cg, bg, (((1,), (1,)), ((), ())), preferred_element_type=F32)
        dtxw = []
        for hh in range(r):
            h = g * r + hh
            cc = cum_c[:, h:h + 1]
            cr = cum_r[h:h + 1, :]
            decay = jnp.exp(jnp.where(causal, cc - cr, -jnp.inf))
            dtx = dt_c[:, h:h + 1] * x[:, h * P_C:(h + 1) * P_C]
            y = jnp.dot((cbm * decay).astype(BF16), dtx.astype(BF16), preferred_element_type=F32)
            hs = h_ref[h].astype(BF16)
            y = y + lax.dot_general(cg, hs, (((1,), (1,)), ((), ())), preferred_element_type=F32) * jnp.exp(cc)
            ys.append(y + ds_ref[0:1, h:h + 1] * x[:, h * P_C:(h + 1) * P_C])
            cl = cum_c[L - 1:L, h:h + 1]
            dtxw.append(dtx * jnp.exp(cl - cc))
        dtxw = jnp.concatenate(dtxw, axis=-1)
        upd = jnp.dot(dtxw.T.astype(BF16), bg, preferred_element_type=F32)
        for hh in range(r):
            h = g * r + hh
            cl = cum_c[L - 1:L, h:h + 1]
            h_ref[h] = h_ref[h] * jnp.exp(cl) + upd[hh * P_C:(hh + 1) * P_C, :]
    y = jnp.concatenate(ys, axis=-1)
    z = z_ref[0]
    y = y * (z * jax.nn.sigmoid(z))
    gw = D_INNER_C // G_C
    outs = []
    for g in range(G_C):
        yg = y[:, g * gw:(g + 1) * gw]
        outs.append(yg * lax.rsqrt(jnp.mean(yg * yg, axis=-1, keepdims=True) + RMS_EPS))
    o_ref[0] = jnp.concatenate(outs, axis=-1) * yn_ref[...]

    @pl.when(ci == pl.num_programs(1) - 1)
    def _():
        h_out[0] = h_ref[...]


def _ssd_prompt(proj, dt, a_neg, conv_w, conv_b, d_skip, y_norm):
    b, t, _ = proj.shape
    L = SSD_CHUNK
    nc = t // L
    an = jnp.concatenate([jnp.broadcast_to(a_neg[None, :], (H_C, H_C)), a_neg[:, None]], axis=1)
    an = jnp.pad(an, ((0, 0), (0, LANE - an.shape[1])))
    ds = jnp.pad(d_skip[None, :], ((0, 0), (0, LANE - H_C)))
    return pl.pallas_call(
        _ssd_body,
        grid=(b, nc),
        in_specs=[pl.BlockSpec((1, L, D_INNER_C), lambda i, c: (i, c, 0)),
                  pl.BlockSpec((1, L, D_INNER_C), lambda i, c: (i, c, 1)),
                  pl.BlockSpec((1, L, 2 * G_C * N_C), lambda i, c: (i, c, 2)),
                  pl.BlockSpec((1, L, H_C), lambda i, c: (i, c, 0)),
                  pl.BlockSpec((1, H_C, L), lambda i, c: (i, 0, c)),
                  pl.BlockSpec((H_C, LANE), lambda i, c: (0, 0)),
                  pl.BlockSpec((CONV_W, CONV_DIM_C), lambda i, c: (0, 0)),
                  pl.BlockSpec((1, CONV_DIM_C), lambda i, c: (0, 0)),
                  pl.BlockSpec((1, LANE), lambda i, c: (0, 0)),
                  pl.BlockSpec((1, D_INNER_C), lambda i, c: (0, 0))],
        out_specs=[pl.BlockSpec((1, L, D_INNER_C), lambda i, c: (i, c, 0)),
                   pl.BlockSpec((1, H_C, P_C, N_C), lambda i, c: (i, 0, 0, 0))],
        out_shape=[jax.ShapeDtypeStruct((b, t, D_INNER_C), F32),
                   jax.ShapeDtypeStruct((b, H_C, P_C, N_C), F32)],
        scratch_shapes=[pltpu.VMEM((CONV_PAD + SSD_CHUNK, CONV_DIM_C), F32), pltpu.VMEM((H_C, P_C, N_C), F32)],
        compiler_params=pltpu.CompilerParams(dimension_semantics=("arbitrary", "arbitrary"), vmem_limit_bytes=VMEM_LIMIT),
        name="ssd_prompt",
    )(proj, proj, proj, dt, dt.transpose(0, 2, 1), an, conv_w, conv_b.reshape(1, CONV_DIM_C), ds,
      y_norm.reshape(1, D_INNER_C))


def _rms_norm(x, g, eps=RMS_EPS):
    xf = x.astype(F32)
    y = xf * lax.rsqrt(jnp.mean(xf * xf, axis=-1, keepdims=True) + eps)
    return (y * g.astype(F32)).astype(x.dtype)


def _split_cols(x, sizes):
    return jnp.split(x, [int(s) for s in np.cumsum(sizes)[:-1]], axis=-1)


def _last_rows(a, n):
    t = a.shape[1]
    if t >= n:
        return a[:, t - n:]
    pad = [(0, 0)] * a.ndim
    pad[1] = (n - t, 0)
    return jnp.pad(a, pad)


def _softmax_attend(q, k, v, mask, sinks=None):
    s = jnp.einsum('...qgrd,...kgd->...grqk', q, k).astype(F32) * (HEAD_DIM ** -0.5)
    s = jnp.where(mask[..., None, None, :, :], s, -jnp.inf)
    m = jnp.max(s, axis=-1, keepdims=True)
    if sinks is not None:
        sk = sinks.astype(F32)[:, :, None, None]
        m = jnp.maximum(m, sk)
    m = jnp.where(jnp.isfinite(m), m, 0.0)
    p = jnp.exp(s - m)
    den = jnp.sum(p, axis=-1, keepdims=True)
    if sinks is not None:
        den = den + jnp.exp(sk - m)
    p = p / jnp.where(den > 0, den, 1.0)
    out = jnp.einsum('...grqk,...kgd->...qgrd', p.astype(v.dtype), v)
    return out, p


def _window_step_attention(q, k_new, v_new, buf, window, sinks):
    t = k_new.shape[1]
    k = jnp.concatenate([buf[:, :, 0], k_new], axis=1)
    v = jnp.concatenate([buf[:, :, 1], v_new], axis=1)
    q_pos = PAST_LEN + jnp.arange(t)
    k_pos = PAST_LEN - window + jnp.arange(window + t)
    diff = q_pos[:, None] - k_pos[None, :]
    mask = (k_pos[None, :] >= 0) & (diff >= 0) & (diff < window)
    out, _ = _softmax_attend(q, k, v, mask, sinks)
    new_buf = jnp.concatenate([buf, jnp.stack([k_new, v_new], axis=2)], axis=1)[:, t:]
    return out, new_buf


def _mlstm_chunk(state, inputs):
    c, n, m = state
    q, k, v, ig, lf = inputs
    L = q.shape[1]
    f_cum = jnp.cumsum(lf, axis=1)
    causal = jnp.tril(jnp.ones((L, L), dtype=bool))
    log_d = jnp.where(causal[None, :, :, None], f_cum[:, :, None, :] - f_cum[:, None, :, :] + ig[:, None, :, :], -jnp.inf)
    m_inter = f_cum + m[:, None, :]
    m_t = jnp.maximum(m_inter, jnp.max(log_d, axis=2))
    d_mat = jnp.exp(log_d - m_t[:, :, None, :])
    w_inter = jnp.exp(m_inter - m_t)
    qk = jnp.einsum('bthd,bshd->btsh', q, k) * d_mat
    num = jnp.einsum('btsh,bshv->bthv', qk, v) + w_inter[..., None] * jnp.einsum('bhvd,bthd->bthv', c, q)
    den = jnp.sum(qk, axis=2) + w_inter * jnp.einsum('bhd,bthd->bth', n, q)
    h = num / jnp.maximum(jnp.abs(den), jnp.exp(-m_t))[..., None]
    m_new = m_t[:, -1]
    w_end = jnp.exp(f_cum[:, -1:] - f_cum + ig - m_new[:, None, :])
    decay = jnp.exp(f_cum[:, -1] + m - m_new)
    c_new = decay[..., None, None] * c + jnp.einsum('bsh,bshv,bshd->bhvd', w_end, v, k)
    n_new = decay[..., None] * n + jnp.einsum('bsh,bshd->bhd', w_end, k)
    return (c_new, n_new, m_new), h


def _ssd_chunk(h, inputs, a_neg):
    x, dt, bm, cm = inputs
    bsz, L = x.shape[:2]
    r = H_C // G_C
    cum = jnp.cumsum(dt * a_neg, axis=1)
    causal = jnp.tril(jnp.ones((L, L), dtype=bool))
    seg = jnp.where(causal[None, :, :, None], cum[:, :, None, :] - cum[:, None, :, :], -jnp.inf)
    decay = jnp.exp(seg).reshape(bsz, L, L, G_C, r)
    dtx = (dt[..., None] * x).reshape(bsz, L, G_C, r, P_C)
    cb = jnp.einsum('btgn,bsgn->btsg', cm, bm)
    y = jnp.einsum('btsg,btsgr,bsgrp->btgrp', cb, decay, dtx)
    hg = h.reshape(bsz, G_C, r, P_C, N_C)
    y = y + jnp.einsum('btgn,bgrpn->btgrp', cm, hg) * jnp.exp(cum).reshape(bsz, L, G_C, r)[..., None]
    w_end = jnp.exp(cum[:, -1:] - cum).reshape(bsz, L, G_C, r)
    h_new = hg * jnp.exp(cum[:, -1]).reshape(bsz, G_C, r)[..., None, None] + jnp.einsum('bsgr,bsgrp,bsgn->bgrpn', w_end, dtx, bm)
    return h_new.reshape(bsz, H_C, P_C, N_C), y.reshape(bsz, L, H_C, P_C)


def _causal_conv(xbc, buf, w, b):
    t = xbc.shape[1]
    xp = jnp.concatenate([buf, xbc], axis=1)
    out = b + sum(xp[:, j:j + t] * w[j] for j in range(CONV_W))
    return jax.nn.silu(out), xp[:, t:]


def _compress(rows, pos, w1, w2):
    bsz, t, g, d = rows.shape
    nb = t // CMP_BLOCK
    blk = rows.reshape(bsz, nb, CMP_BLOCK, g, d) + pos[:, None, :]
    blk = blk.transpose(0, 1, 3, 2, 4).reshape(bsz, nb, g, CMP_BLOCK * d)
    return jax.nn.silu(blk @ w1) @ w2


def _select_blocks(p_cmp, q_pos, n_top):
    bsz, g, r, t, nbc = p_cmp.shape
    ratio = SLC_BLOCK // CMP_BLOCK
    imp = p_cmp.sum(axis=2).reshape(bsz, g, t, nbc // ratio, ratio).sum(axis=-1)
    blk = jnp.arange(nbc // ratio)[None, :]
    cur = (q_pos // SLC_BLOCK)[:, None]
    forced = (blk == 0) | (blk == cur) | (blk == cur - 1)
    score = jnp.where(blk > cur, -jnp.inf, jnp.where(forced, jnp.inf, imp))
    _, sel = lax.top_k(score, n_top)
    return sel.astype(jnp.int32)


def _pad_cols(w, n):
    return jnp.pad(w, ((0, 0), (0, n - w.shape[1])))


def _even_layer(x, w, st, prompt):
    bsz, t, _ = x.shape
    r_a = H_A // KVH_A
    n_in = sum(IN_E_SIZES)
    proj = _norm_matmul(x.reshape(bsz * t, D_MODEL), w['norm_mix'], w['w_in_p'])[:, :n_in].reshape(bsz, t, n_in)
    qa, ka, va, qb, kb, vb, ib, fb, ob = _split_cols(proj, IN_E_SIZES)
    qa = _rms_norm(qa.reshape(bsz, t, KVH_A, r_a, HEAD_DIM), w['q_norm'])
    ka = _rms_norm(ka.reshape(bsz, t, KVH_A, HEAD_DIM), w['k_norm'])
    va = va.reshape(bsz, t, KVH_A, HEAD_DIM)
    sinks = w['sinks'].reshape(KVH_A, r_a)
    if prompt:
        o_a = _window_attend(qa.reshape(bsz, t, H_A * HEAD_DIM), ka.transpose(0, 2, 1, 3), va.transpose(0, 2, 1, 3),
                             WINDOW_A, w['sinks'])
        new_win = _last_rows(jnp.stack([ka, va], axis=2), WINDOW_A)
    else:
        o_a, new_win = _window_step_attention(qa, ka, va, st['a_win'], WINDOW_A, sinks)
    ig = ib + w['b_igate']
    lf = jax.nn.log_sigmoid(fb + w['b_fgate'])
    if prompt:
        gates = jnp.concatenate([ig.transpose(0, 2, 1), lf.transpose(0, 2, 1)], axis=1)
        hb, c_t, n_rep, m_rep = _mlstm_prompt(qb, kb.transpose(0, 2, 1), vb, gates, ob, w['h_norm'])
        c_new, n_new, m_new = c_t.transpose(0, 1, 3, 2), n_rep[..., 0], m_rep[:, :, 0, 0]
    else:
        qb = qb.reshape(bsz, t, H_B, DK_B)
        kb = kb.reshape(bsz, t, H_B, DK_B) * (DK_B ** -0.5)
        vb = vb.reshape(bsz, t, H_B, DV_B)
        init = (st['b_c'], st['b_n'], st['b_m'])
        (c_new, n_new, m_new), hb = _mlstm_chunk(init, (qb, kb, vb, ig, lf))
        hb = _rms_norm(hb, w['h_norm']) * jax.nn.sigmoid(ob.reshape(bsz, t, H_B, DV_B))
    xr = _out_proj(o_a.reshape(bsz * t, H_A * HEAD_DIM), hb.reshape(bsz * t, H_B * DV_B), w['w_out'],
                   x.reshape(bsz * t, D_MODEL))
    xr = _ffn(xr, w['norm_ffn'], w['w_gu'], w['w_down'])
    return xr.reshape(bsz, t, D_MODEL), (new_win, c_new, n_new, m_new)


def _odd_mixers(x, w, st, page_table, prompt):
    bsz, t, _ = x.shape
    r_d = H_D // KVH_D
    n_in = sum(IN_O_SIZES)
    proj_full = _norm_matmul(x.reshape(bsz * t, D_MODEL), w['norm_mix'], w['w_in_p'])
    proj = proj_full[:, :n_in].reshape(bsz, t, n_in)
    zc, xbc, dtc, qd, kvd, gd = _split_cols(proj, IN_O_SIZES)
    dt = jax.nn.softplus(dtc + w['dt_bias'])
    if prompt:
        yc, h_new = _ssd_prompt(proj_full.reshape(bsz, t, -1), dt, -jnp.exp(w['a_log']), w['conv_w'], w['conv_b'],
                                w['d_skip'], w['y_norm'])
        new_conv = xbc[:, t - (CONV_W - 1):]
    else:
        xbc, new_conv = _causal_conv(xbc, st['c_conv'], w['conv_w'], w['conv_b'])
        xc, bc, cc = _split_cols(xbc, (D_INNER_C, G_C * N_C, G_C * N_C))
        xc = xc.reshape(bsz, t, H_C, P_C)
        bc = bc.reshape(bsz, t, G_C, N_C)
        cc = cc.reshape(bsz, t, G_C, N_C)
        h_new, yc = _ssd_chunk(st['c_ssm'], (xc, dt, bc, cc), -jnp.exp(w['a_log']))
        yc = yc + w['d_skip'][:, None] * xc
        yc = yc.reshape(bsz, t, D_INNER_C) * jax.nn.silu(zc)
        yc = _rms_norm(yc.reshape(bsz, t, G_C, D_INNER_C // G_C), w['y_norm'].reshape(G_C, D_INNER_C // G_C))
    qd = _rms_norm(qd.reshape(bsz, t, KVH_D, r_d, HEAD_DIM), w['q_norm'])
    kc, vc, ks, vs, kw, vw = [a.reshape(bsz, t, KVH_D, HEAD_DIM) for a in _split_cols(kvd, (KVH_D * HEAD_DIM,) * 6)]
    ks = _rms_norm(ks, w['k_norm'])
    kw = _rms_norm(kw, w['k_norm'])
    new_kv = jnp.stack([kc, vc, ks, vs], axis=2)
    q_pos = jnp.arange(t) + (0 if prompt else PAST_LEN)
    ks_t, vs_t = ks.transpose(0, 2, 1, 3), vs.transpose(0, 2, 1, 3)

    def comp(rows, j):
        return _compress(rows, w['cmp_pos'][j], w['cmp_w1'][j], w['cmp_w2'][j])
    if prompt:
        cmp_blocks = _compress_prompt(kvd[..., :CMP_W], w['cmp_pos'], w['cmp_w1'], w['cmp_w2'])
        kcmp = cmp_blocks[..., :KVH_D * HEAD_DIM].reshape(bsz, -1, KVH_D, HEAD_DIM)
        vcmp = cmp_blocks[..., KVH_D * HEAD_DIM:].reshape(bsz, -1, KVH_D, HEAD_DIM)
    else:
        assert t == 1
        pool = jnp.transpose(st['d_kv'], (0, 1, 3, 4, 5, 2)).reshape(
            st['d_kv'].shape[0], st['d_kv'].shape[1], CMP_W + SLC_W, PAGE_SIZE)
        n_pages = page_table.shape[1]
        past = _compress_pages(pool, page_table, w['cmp_pos'], w['cmp_w1'], w['cmp_w2'], st['layer'])
        past = past.reshape(bsz, n_pages * BLK_PER_PAGE, 2, KVH_D, HEAD_DIM)
        pad_t = -(-t // SLC_BLOCK) * SLC_BLOCK
        padw = ((0, 0), (0, pad_t - t), (0, 0), (0, 0))
        kcmp = jnp.concatenate([past[:, :, 0], comp(jnp.pad(kc, padw), 0)], axis=1)
        vcmp = jnp.concatenate([past[:, :, 1], comp(jnp.pad(vc, padw), 1)], axis=1)
    kcmp = _rms_norm(kcmp, w['k_norm'])
    n_top = min(TOP_N, kcmp.shape[1] // (SLC_BLOCK // CMP_BLOCK))
    if prompt:
        q_tok = qd.reshape(bsz, t, H_D * HEAD_DIM)
        hshape = (bsz, t, KVH_D, r_d, HEAD_DIM)
        o_cmp, sel = _cmp_select(q_tok, kcmp.transpose(0, 2, 1, 3), vcmp.transpose(0, 2, 1, 3), n_top)
        o_slc = _slc_attend(q_tok, sel, ks_t, vs_t).reshape(hshape)
        o_cmp = o_cmp.reshape(hshape)
        o_win = _window_attend(q_tok, kw.transpose(0, 2, 1, 3), vw.transpose(0, 2, 1, 3), WINDOW_D).reshape(hshape)
        new_win = _last_rows(jnp.stack([kw, vw], axis=2), WINDOW_D)
    else:
        blk_end = jnp.arange(kcmp.shape[1]) * CMP_BLOCK + (CMP_BLOCK - 1)
        o_cmp, p_cmp = _softmax_attend(qd, kcmp, vcmp, blk_end[None, :] <= q_pos[:, None])
        sel = _select_blocks(p_cmp, q_pos, n_top)[:, :, 0]
        q_rows = jnp.pad(qd[:, 0], ((0, 0), (0, 0), (0, 8 - r_d), (0, 0)))
        o8 = _slc_step(pool, page_table, sel, q_rows, ks_t, vs_t, st['layer'])
        o_slc = o8[:, None, :, :r_d]
        o_win, new_win = _window_step_attention(qd, kw, vw, st['d_win'], WINDOW_D, None)
    g = jax.nn.sigmoid(gd).reshape(bsz, t, KVH_D, r_d, 3)
    o_d = g[..., 0:1] * o_cmp + g[..., 1:2] * o_slc + g[..., 2:3] * o_win
    return (yc.reshape(bsz * t, D_INNER_C), o_d.reshape(bsz * t, H_D * HEAD_DIM)), (h_new, new_conv, new_kv, new_win)


def kernel(x_prompt, x_sample, cache_a_win, state_b_c, state_b_n, state_b_m, state_c_ssm, state_c_conv, cache_d_kv, cache_d_win, page_table, e_norm_mix, e_w_in, e_q_norm, e_k_norm, e_sinks, e_b_igate, e_b_fgate, e_h_norm, e_w_out, e_norm_ffn, e_w_gu, e_w_down, o_norm_mix, o_w_in, o_conv_w, o_conv_b, o_dt_bias, o_a_log, o_d_skip, o_y_norm, o_q_norm, o_k_norm, o_cmp_pos, o_cmp_w1, o_cmp_w2, o_w_out, o_norm_ffn, o_router, o_w_gu, o_w_down):
    i = 0
    we = {'norm_mix': e_norm_mix[i], 'w_in_p': _pad_cols(e_w_in[i], _round_up(sum(IN_E_SIZES), LANE)).astype(BF16),
          'q_norm': e_q_norm[i], 'k_norm': e_k_norm[i],
          'sinks': e_sinks[i], 'b_igate': e_b_igate[i], 'b_fgate': e_b_fgate[i], 'h_norm': e_h_norm[i],
          'w_out': e_w_out[i].astype(BF16), 'norm_ffn': e_norm_ffn[i], 'w_gu': e_w_gu[i].astype(BF16),
          'w_down': e_w_down[i].astype(BF16)}
    wo = {'norm_mix': o_norm_mix[i], 'w_in_p': _pad_cols(o_w_in[i], _round_up(sum(IN_O_SIZES), LANE)).astype(BF16),
          'conv_w': o_conv_w[i], 'conv_b': o_conv_b[i],
          'dt_bias': o_dt_bias[i], 'a_log': o_a_log[i], 'd_skip': o_d_skip[i], 'y_norm': o_y_norm[i],
          'q_norm': o_q_norm[i], 'k_norm': o_k_norm[i], 'cmp_pos': o_cmp_pos[i], 'cmp_w1': o_cmp_w1[i],
          'cmp_w2': o_cmp_w2[i], 'w_out': o_w_out[i].astype(BF16), 'norm_ffn': o_norm_ffn[i],
          'router': _pad_cols(o_router[i], LANE), 'w_gu': o_w_gu[i].astype(BF16), 'w_down': o_w_down[i].astype(BF16)}
    st_e = {'a_win': cache_a_win[i], 'b_c': state_b_c[i], 'b_n': state_b_n[i], 'b_m': state_b_m[i]}
    st_o = {'c_ssm': state_c_ssm[i], 'c_conv': state_c_conv[i], 'd_kv': cache_d_kv, 'layer': i,
            'd_win': cache_d_win[i]}

    xp, sp_e = _even_layer(x_prompt, we, None, True)
    xs, ss_e = _even_layer(x_sample, we, st_e, False)

    mix_p, sp_o = _odd_mixers(xp, wo, None, page_table, True)
    mix_s, ss_o = _odd_mixers(xs, wo, st_o, page_table, False)
    xp2, xnp, lgp = _out_proj_route(mix_p[0], mix_p[1], wo['w_out'], xp.reshape(-1, D_MODEL), wo['norm_ffn'], wo['router'])
    xs2, xns, lgs = _out_proj_route(mix_s[0], mix_s[1], wo['w_out'], xs.reshape(-1, D_MODEL), wo['norm_ffn'], wo['router'])
    xn_all = jnp.concatenate([xnp, xns], axis=0)
    lg_all = jnp.concatenate([lgp, lgs], axis=0)[:, :N_EXP]
    y_p, y_s = _moe_layer([xp2, xs2], xn_all, lg_all, wo['w_gu'], wo['w_down'])
    y_prompt = y_p.reshape(x_prompt.shape)
    y_sample = y_s.reshape(x_sample.shape)

    def one(a):
        return a[None]
    return (y_prompt, y_sample, one(sp_e[0]), one(ss_e[0]), one(sp_e[1]), one(ss_e[1]), one(sp_e[2]), one(ss_e[2]),
            one(sp_e[3]), one(ss_e[3]), one(sp_o[0]), one(ss_o[0]), one(sp_o[1]), one(ss_o[1]),
            one(sp_o[2]), one(ss_o[2]), one(sp_o[3]), one(ss_o[3]))
```

```python
import functools
import math

import numpy as np
import jax
import jax.numpy as jnp
from jax import lax
from jax.experimental import pallas as pl
from jax.experimental.pallas import tpu as pltpu

F32 = jnp.float32
BF16 = jnp.bfloat16

D_MODEL = 1024
PAST_LEN = 16384
PAGE_SIZE = 128
HEAD_DIM = 64
QBLOCK = 128
H_A, KVH_A, WINDOW_A = 8, 2, 128
H_B, DK_B, DV_B, CHUNK_B = 4, 64, 128, 64
P_C, H_C, G_C, N_C, CONV_W, CHUNK_C = 64, 8, 2, 128, 4, 128
D_INNER_C = H_C * P_C
CONV_DIM_C = D_INNER_C + 2 * G_C * N_C
H_D, KVH_D = 8, 2
CMP_BLOCK, SLC_BLOCK, TOP_N, WINDOW_D, SLC_QBLOCK = 32, 64, 16, 512, 32
D_FF, N_EXP, TOP_K = 3584, 8, 2
IN_E_SIZES = (H_A * HEAD_DIM, KVH_A * HEAD_DIM, KVH_A * HEAD_DIM, H_B * DK_B, H_B * DK_B, H_B * DV_B, H_B, H_B, H_B * DV_B)
IN_O_SIZES = (D_INNER_C, CONV_DIM_C, H_C, H_D * HEAD_DIM, 6 * KVH_D * HEAD_DIM, 3 * H_D)

LANE = 128
VMEM_LIMIT = 48 * 1024 * 1024
RMS_EPS = 1e-6
FF_TILE = 1792
ROW_TILE = 512


def _round_up(n, m):
    return -(-n // m) * m


def _rms(x, g):
    return x * lax.rsqrt(jnp.mean(x * x, axis=-1, keepdims=True) + RMS_EPS) * g


def _norm_matmul_body(x_ref, g_ref, w_ref, o_ref):
    xn = _rms(x_ref[...], g_ref[...])
    o_ref[...] = jnp.dot(xn.astype(BF16), w_ref[...], preferred_element_type=F32)


def _norm_matmul(x, gain, w):
    m, k = x.shape
    n = w.shape[1]
    tm = min(ROW_TILE, m)
    return pl.pallas_call(
        _norm_matmul_body,
        grid=(m // tm,),
        in_specs=[pl.BlockSpec((tm, k), lambda i: (i, 0)),
                  pl.BlockSpec((1, k), lambda i: (0, 0)),
                  pl.BlockSpec((k, n), lambda i: (0, 0))],
        out_specs=pl.BlockSpec((tm, n), lambda i: (i, 0)),
        out_shape=jax.ShapeDtypeStruct((m, n), F32),
        compiler_params=pltpu.CompilerParams(dimension_semantics=("arbitrary",), vmem_limit_bytes=VMEM_LIMIT),
        name="norm_in_proj",
    )(x, gain.reshape(1, k), w)


def _mix_matmul(a_ref, b_ref, w_ref):
    ka = a_ref.shape[1]
    return (jnp.dot(a_ref[...].astype(BF16), w_ref[0:ka, :], preferred_element_type=F32)
            + jnp.dot(b_ref[...].astype(BF16), w_ref[ka:, :], preferred_element_type=F32))


def _out_proj_body(a_ref, b_ref, w_ref, res_ref, o_ref):
    o_ref[...] = res_ref[...] + _mix_matmul(a_ref, b_ref, w_ref)


def _out_proj(a, b, w, res):
    m, ka = a.shape
    kb = b.shape[1]
    n = w.shape[1]
    tm = min(ROW_TILE, m)
    return pl.pallas_call(
        _out_proj_body,
        grid=(m // tm,),
        in_specs=[pl.BlockSpec((tm, ka), lambda i: (i, 0)),
                  pl.BlockSpec((tm, kb), lambda i: (i, 0)),
                  pl.BlockSpec((ka + kb, n), lambda i: (0, 0)),
                  pl.BlockSpec((tm, n), lambda i: (i, 0))],
        out_specs=pl.BlockSpec((tm, n), lambda i: (i, 0)),
        out_shape=jax.ShapeDtypeStruct((m, n), F32),
        compiler_params=pltpu.CompilerParams(dimension_semantics=("arbitrary",), vmem_limit_bytes=VMEM_LIMIT),
        name="out_proj",
    )(a, b, w, res)


def _out_proj_route_body(a_ref, b_ref, w_ref, res_ref, g_ref, r_ref, o_ref, xn_ref, lg_ref):
    x = res_ref[...] + _mix_matmul(a_ref, b_ref, w_ref)
    o_ref[...] = x
    xn = _rms(x, g_ref[...])
    xn_ref[...] = xn.astype(BF16)
    lg_ref[...] = jnp.dot(xn, r_ref[...], preferred_element_type=F32, precision=lax.Precision.HIGHEST)


def _out_proj_route(a, b, w, res, gain, router_pad):
    m, ka = a.shape
    kb = b.shape[1]
    n = w.shape[1]
    tm = min(ROW_TILE, m)
    return pl.pallas_call(
        _out_proj_route_body,
        grid=(m // tm,),
        in_specs=[pl.BlockSpec((tm, ka), lambda i: (i, 0)),
                  pl.BlockSpec((tm, kb), lambda i: (i, 0)),
                  pl.BlockSpec((ka + kb, n), lambda i: (0, 0)),
                  pl.BlockSpec((tm, n), lambda i: (i, 0)),
                  pl.BlockSpec((1, n), lambda i: (0, 0)),
                  pl.BlockSpec((n, LANE), lambda i: (0, 0))],
        out_specs=[pl.BlockSpec((tm, n), lambda i: (i, 0)),
                   pl.BlockSpec((tm, n), lambda i: (i, 0)),
                   pl.BlockSpec((tm, LANE), lambda i: (i, 0))],
        out_shape=[jax.ShapeDtypeStruct((m, n), F32),
                   jax.ShapeDtypeStruct((m, n), BF16),
                   jax.ShapeDtypeStruct((m, LANE), F32)],
        compiler_params=pltpu.CompilerParams(dimension_semantics=("arbitrary",), vmem_limit_bytes=VMEM_LIMIT),
        name="out_proj_route",
    )(a, b, w, res, gain.reshape(1, n), router_pad)


def _ffn_body(x_ref, g_ref, wg_ref, wu_ref, wd_ref, o_ref, xn_ref, acc_ref):
    c = pl.program_id(1)

    @pl.when(c == 0)
    def _():
        xn_ref[...] = _rms(x_ref[...], g_ref[...]).astype(BF16)
        acc_ref[...] = jnp.zeros_like(acc_ref)

    xn = xn_ref[...]
    gate = jnp.dot(xn, wg_ref[...], preferred_element_type=F32)
    up = jnp.dot(xn, wu_ref[...], preferred_element_type=F32)
    h = (gate * jax.nn.sigmoid(gate) * up).astype(BF16)
    acc_ref[...] += jnp.dot(h, wd_ref[...], preferred_element_type=F32)

    @pl.when(c == pl.num_programs(1) - 1)
    def _():
        o_ref[...] = x_ref[...] + acc_ref[...]


def _ffn(x, gain, w_gu, w_down):
    m, d = x.shape
    tm = min(ROW_TILE, m)
    nc = D_FF // FF_TILE
    return pl.pallas_call(
        _ffn_body,
        grid=(m // tm, nc),
        in_specs=[pl.BlockSpec((tm, d), lambda i, c: (i, 0)),
                  pl.BlockSpec((1, d), lambda i, c: (0, 0)),
                  pl.BlockSpec((d, FF_TILE), lambda i, c: (0, c)),
                  pl.BlockSpec((d, FF_TILE), lambda i, c: (0, c + nc)),
                  pl.BlockSpec((FF_TILE, d), lambda i, c: (c, 0))],
        out_specs=pl.BlockSpec((tm, d), lambda i, c: (i, 0)),
        out_shape=jax.ShapeDtypeStruct((m, d), F32),
        scratch_shapes=[pltpu.VMEM((tm, d), BF16), pltpu.VMEM((tm, d), F32)],
        compiler_params=pltpu.CompilerParams(dimension_semantics=("arbitrary", "arbitrary"), vmem_limit_bytes=VMEM_LIMIT),
        name="ffn_dense",
    )(x, gain.reshape(1, d), w_gu, w_gu, w_down)


def _moe_body(te_ref, tv_ref, x_ref, s_ref, wg_ref, wu_ref, wd_ref, o_ref, acc_ref):
    i = pl.program_id(0)
    c = pl.program_id(1)
    valid = tv_ref[i] > 0

    @pl.when(c == 0)
    def _():
        acc_ref[...] = jnp.zeros_like(acc_ref)

    @pl.when(valid)
    def _():
        x = x_ref[...]
        gate = jnp.dot(x, wg_ref[0], preferred_element_type=F32)
        up = jnp.dot(x, wu_ref[0], preferred_element_type=F32)
        h = (gate * jax.nn.sigmoid(gate) * up).astype(BF16)
        acc_ref[...] += jnp.dot(h, wd_ref[0], preferred_element_type=F32)

    @pl.when(c == pl.num_programs(1) - 1)
    def _():
        o_ref[...] = s_ref[...] * acc_ref[...]


def _moe_grouped(xs, scale, tile_expert, tile_valid, w_gu, w_down, tm):
    p, d = xs.shape
    nc = D_FF // FF_TILE
    grid_spec = pltpu.PrefetchScalarGridSpec(
        num_scalar_prefetch=2,
        grid=(p // tm, nc),
        in_specs=[pl.BlockSpec((tm, d), lambda i, c, te, tv: (i, 0)),
                  pl.BlockSpec((tm, 1), lambda i, c, te, tv: (i, 0)),
                  pl.BlockSpec((1, d, FF_TILE), lambda i, c, te, tv: (te[i], 0, c)),
                  pl.BlockSpec((1, d, FF_TILE), lambda i, c, te, tv: (te[i], 0, c + nc)),
                  pl.BlockSpec((1, FF_TILE, d), lambda i, c, te, tv: (te[i], c, 0))],
        out_specs=pl.BlockSpec((tm, d), lambda i, c, te, tv: (i, 0)),
        scratch_shapes=[pltpu.VMEM((tm, d), F32)],
    )
    return pl.pallas_call(
        _moe_body,
        grid_spec=grid_spec,
        out_shape=jax.ShapeDtypeStruct((p, d), F32),
        compiler_params=pltpu.CompilerParams(dimension_semantics=("arbitrary", "arbitrary"), vmem_limit_bytes=VMEM_LIMIT),
        name="moe_grouped",
    )(tile_expert, tile_valid, xs, scale, w_gu, w_gu, w_down)


def _moe_route(logits, tm):
    n = logits.shape[0]
    top_v, top_i = lax.top_k(logits, TOP_K)
    gate = jax.nn.softmax(top_v, axis=-1)
    flat_e = top_i.reshape(-1).astype(jnp.int32)
    na = n * TOP_K
    n_tiles = na // tm + N_EXP
    p = n_tiles * tm
    n_pad = p - na
    counts = jnp.sum(jax.nn.one_hot(flat_e, N_EXP, dtype=jnp.int32), axis=0)
    tiles_per_e = (counts + tm - 1) // tm
    cum_pad = jnp.cumsum(tiles_per_e * tm - counts)
    d = jnp.arange(n_pad, dtype=jnp.int32)
    pad_e = jnp.sum((d[:, None] >= cum_pad[None, :]).astype(jnp.int32), axis=1)
    keys = jnp.concatenate([2 * flat_e, 2 * pad_e + 1])
    ids = jnp.arange(p, dtype=jnp.int32)
    gates_all = jnp.concatenate([gate.reshape(-1), jnp.zeros((n_pad,), F32)])
    _, id_sorted, gate_of_pos = lax.sort((keys, ids, gates_all), num_keys=1, is_stable=True)
    tok_of_pos = jnp.where(id_sorted < na, id_sorted // TOP_K, 0)
    _, pos_by_id = lax.sort((id_sorted, ids), num_keys=1)
    pos_of_a = pos_by_id[:na].reshape(n, TOP_K)
    tile_end = jnp.cumsum(tiles_per_e)
    total_tiles = tile_end[-1]
    tile_ids = jnp.arange(n_tiles, dtype=jnp.int32)
    tile_expert = jnp.searchsorted(tile_end, jnp.minimum(tile_ids, total_tiles - 1), side="right").astype(jnp.int32)
    tile_expert = jnp.clip(tile_expert, 0, N_EXP - 1)
    tile_valid = (tile_ids < total_tiles).astype(jnp.int32)
    return tok_of_pos, gate_of_pos, pos_of_a, tile_expert, tile_valid


def _moe_layer(x_parts, xn_rows, logits, w_gu, w_down, tm=ROW_TILE):
    tok_of_pos, gate_of_pos, pos_of_a, tile_expert, tile_valid = _moe_route(logits, tm)
    xs = xn_rows[tok_of_pos]
    ys = _moe_grouped(xs, gate_of_pos.reshape(-1, 1), tile_expert, tile_valid, w_gu, w_down, tm)
    outs, r0 = [], 0
    for xp in x_parts:
        pa = pos_of_a[r0:r0 + xp.shape[0]]
        outs.append(xp + (ys[pa[:, 0]] + ys[pa[:, 1]]))
        r0 += xp.shape[0]
    return outs


QT = 128
KT = 512
NEG = -1e30
RANK_GROUP = 16
LOG2E = 1.4426950408889634


def _heads_to_rows(qt):
    r = qt.shape[1] // HEAD_DIM
    return jnp.concatenate([qt[:, h * HEAD_DIM:(h + 1) * HEAD_DIM] for h in range(r)], axis=0)


def _rows_to_heads(o, r):
    return jnp.concatenate([o[h * QT:(h + 1) * QT] for h in range(r)], axis=-1)


def _cmp_select_body(q_ref, kc_ref, vc_ref, o_ref, sel_ref, rank_ref, *, n_top):
    qi = pl.program_id(2)
    r = q_ref.shape[2] // HEAD_DIM
    q = _heads_to_rows(q_ref[0])
    s = lax.dot_general(q.astype(BF16), kc_ref[0, 0].astype(BF16), (((1,), (1,)), ((), ())),
                        preferred_element_type=F32) * (HEAD_DIM ** -0.5)
    row = lax.broadcasted_iota(jnp.int32, (r * QT, LANE), 0)
    lane = lax.broadcasted_iota(jnp.int32, (r * QT, LANE), 1)
    t_pos = qi * QT + (row & (QT - 1))
    s = jnp.where(lane * CMP_BLOCK + (CMP_BLOCK - 1) <= t_pos, s, -jnp.inf)
    m = jnp.max(s, axis=-1, keepdims=True)
    m = jnp.where(m > -jnp.inf, m, 0.0)
    p = jnp.exp(s - m)
    den = jnp.sum(p, axis=-1, keepdims=True)
    p = p / jnp.where(den > 0, den, 1.0)
    o = jnp.dot(p.astype(BF16), vc_ref[0, 0].astype(BF16), preferred_element_type=F32)
    o_ref[0] = _rows_to_heads(o, r)
    pt = p[0:QT]
    for h in range(1, r):
        pt = pt + p[h * QT:(h + 1) * QT]
    lane1 = lax.broadcasted_iota(jnp.int32, (QT, LANE), 1)
    row1 = lax.broadcasted_iota(jnp.int32, (QT, LANE), 0)
    even = (lane1 & 1) == 0
    lo = jnp.where(even, pt, pltpu.roll(pt, 1, 1))
    hi = jnp.where(even, pltpu.roll(pt, LANE - 1, 1), pt)
    imp = lo + hi
    blk = lane1 >> 1
    cur = (qi * QT + row1) >> 6
    forced = (blk == 0) | (blk == cur) | (blk == cur - 1)
    score = jnp.where(blk > cur, -jnp.inf, jnp.where(forced, jnp.inf, imp))
    lane_f = lane1.astype(F32)
    rank_ref[...] = jnp.zeros((QT, LANE), F32)
    for grp in range(0, LANE // 2, RANK_GROUP):
        @pl.when(2 * qi + 1 >= grp)
        def _():
            cnt = rank_ref[...]
            for j in range(grp, grp + RANK_GROUP):
                col = jnp.broadcast_to(score[:, 2 * j:2 * j + 1], (QT, LANE))
                later = jnp.where(lane_f >= 2.0 * j + 2.0, 1.0, 0.0)
                cnt = cnt + jnp.where(col > score, 1.0, 0.0) + jnp.where(col == score, later, 0.0)
            rank_ref[...] = cnt
    sel_ref[0, 0] = jnp.where((rank_ref[...] < n_top) & (blk <= cur), 1.0, 0.0)


def _cmp_select(q, kcmp, vcmp, n_top):
    b, t, w = q.shape
    g, d = kcmp.shape[1], kcmp.shape[3]
    gw = w // g
    assert kcmp.shape == (b, g, LANE, d) and t // CMP_BLOCK == LANE
    return pl.pallas_call(
        functools.partial(_cmp_select_body, n_top=n_top),
        grid=(b, g, t // QT),
        in_specs=[pl.BlockSpec((1, QT, gw), lambda i, j, k: (i, k, j)),
                  pl.BlockSpec((1, 1, LANE, d), lambda i, j, k: (i, j, 0, 0)),
                  pl.BlockSpec((1, 1, LANE, d), lambda i, j, k: (i, j, 0, 0))],
        out_specs=[pl.BlockSpec((1, QT, gw), lambda i, j, k: (i, k, j)),
                   pl.BlockSpec((1, 1, QT, LANE), lambda i, j, k: (i, j, k, 0))],
        out_shape=[jax.ShapeDtypeStruct((b, t, w), F32),
                   jax.ShapeDtypeStruct((b, g, t, LANE), F32)],
        scratch_shapes=[pltpu.VMEM((QT, LANE), F32)],
        compiler_params=pltpu.CompilerParams(dimension_semantics=("arbitrary",) * 3, vmem_limit_bytes=VMEM_LIMIT),
        name="nsa_cmp_select",
    )(q, kcmp, vcmp)


def _slc_attend_body(q_ref, sel_ref, k_ref, v_ref, o_ref):
    qi = pl.program_id(2)
    r = q_ref.shape[2] // HEAD_DIM
    rows = r * QT
    q = (_heads_to_rows(q_ref[0]) * (HEAD_DIM ** -0.5 * LOG2E)).astype(BF16)
    sel = sel_ref[0, 0]
    e_row = lax.broadcasted_iota(jnp.int32, (LANE, KT), 0)
    e_col = lax.broadcasted_iota(jnp.int32, (LANE, KT), 1)
    expand = jnp.where(e_row == 2 * (e_col >> 6), 1.0, 0.0).astype(BF16)
    t_pos = qi * QT + lax.broadcasted_iota(jnp.int32, (QT, KT), 0)
    k_off = lax.broadcasted_iota(jnp.int32, (QT, KT), 1)
    n_kt = (qi * QT + QT - 1) // KT + 1
    lanes_per_kt = 2 * (KT // SLC_BLOCK)
    ones = jnp.ones((KT, LANE - HEAD_DIM), BF16)

    def tile_scores(kt):
        start = pl.multiple_of(kt * KT, KT)
        k = k_ref[0, 0, pl.ds(start, KT), :].astype(BF16)
        v = v_ref[0, 0, pl.ds(start, KT), :].astype(BF16)
        v1 = jnp.concatenate([v, ones], axis=-1)
        s = lax.dot_general(q, k, (((1,), (1,)), ((), ())), preferred_element_type=F32)
        sel_kt = pltpu.roll(sel, (LANE - kt * lanes_per_kt) % LANE, 1).astype(BF16)
        picked = jnp.dot(sel_kt, expand, preferred_element_type=F32)
        bias = jnp.where((picked > 0.5) & (start + k_off <= t_pos), 0.0, NEG)
        return s + jnp.concatenate([bias] * r, axis=0), v1

    def fold(carry, s, v1):
        m, acc = carry
        m_new = jnp.maximum(m, jnp.max(s, axis=-1, keepdims=True))
        alpha = jnp.exp2(m - m_new)
        p = jnp.exp2(s - m_new).astype(BF16)
        return m_new, alpha * acc + jnp.dot(p, v1, preferred_element_type=F32)

    def body(i, carry):
        s0, v0 = tile_scores(2 * i)
        s1, v1 = tile_scores(2 * i + 1)
        return fold(fold(carry, s0, v0), s1, v1)

    init = (jnp.full((rows, 1), NEG, F32), jnp.zeros((rows, LANE), F32))
    carry = lax.fori_loop(0, n_kt // 2, body, init)
    m, acc = lax.cond(n_kt % 2 == 1, lambda c: fold(c, *tile_scores(n_kt - 1)), lambda c: c, carry)
    o_ref[0] = _rows_to_heads(acc[:, :HEAD_DIM] / acc[:, HEAD_DIM:HEAD_DIM + 1], r)


def _slc_attend(q, sel, ks, vs):
    b, t, w = q.shape
    g, d = ks.shape[1], ks.shape[3]
    gw = w // g
    return pl.pallas_call(
        _slc_attend_body,
        grid=(b, g, t // QT),
        in_specs=[pl.BlockSpec((1, QT, gw), lambda i, j, k: (i, k, j)),
                  pl.BlockSpec((1, 1, QT, LANE), lambda i, j, k: (i, j, k, 0)),
                  pl.BlockSpec((1, 1, t, d), lambda i, j, k: (i, j, 0, 0)),
                  pl.BlockSpec((1, 1, t, d), lambda i, j, k: (i, j, 0, 0))],
        out_specs=pl.BlockSpec((1, QT, gw), lambda i, j, k: (i, k, j)),
        out_shape=jax.ShapeDtypeStruct((b, t, w), F32),
        compiler_params=pltpu.CompilerParams(dimension_semantics=("arbitrary",) * 3, vmem_limit_bytes=VMEM_LIMIT),
        name="nsa_slc_attend",
    )(q, sel, ks, vs)


CMP_HIDDEN = 128
CMP_W = 2 * KVH_D * HEAD_DIM
SLC_W = 2 * KVH_D * HEAD_DIM
BLK_PER_PAGE = PAGE_SIZE // CMP_BLOCK
PAGES_PER_STEP = 64


def _page_copy(pool_ref, buf_ref, sem_ref, layer, page, slot, j):
    return pltpu.make_async_copy(pool_ref.at[layer, page, pl.ds(0, CMP_W), :], buf_ref.at[slot, j], sem_ref.at[slot])


def _compress_pages_body(pt_ref, pool_ref, post_ref, perm_ref, w1_ref, w2_ref, o_ref, buf_ref, xk_ref, xv_ref, sem_ref, *,
                         layer, n_steps):
    x_ref = (xk_ref, xv_ref)
    s = pl.program_id(0)
    slot = s % 2

    def start(step, slot_):
        for j in range(PAGES_PER_STEP):
            _page_copy(pool_ref, buf_ref, sem_ref, layer, pt_ref[step * PAGES_PER_STEP + j], slot_, j).start()

    @pl.when(s == 0)
    def _():
        start(0, 0)

    @pl.when(s + 1 < n_steps)
    def _():
        start(s + 1, 1 - slot)

    for j in range(PAGES_PER_STEP):
        _page_copy(pool_ref, buf_ref, sem_ref, layer, 0, slot, j).wait()

    for pp in range(PAGES_PER_STEP // 2):
        for br in range(2):
            t0 = buf_ref[slot, 2 * pp, br * LANE:(br + 1) * LANE, :] + post_ref[br]
            t1 = buf_ref[slot, 2 * pp + 1, br * LANE:(br + 1) * LANE, :] + post_ref[br]
            tc = jnp.concatenate([t0, t1], axis=-1).astype(BF16)
            tp = jnp.dot(tc, perm_ref[...], preferred_element_type=F32)
            x_ref[br][pp] = tp.T

    m = PAGES_PER_STEP * BLK_PER_PAGE
    for br in range(2):
        acc = jnp.zeros((m, KVH_D * CMP_HIDDEN), F32)
        for r in range(0, CMP_BLOCK, 2):
            x = jnp.concatenate([x_ref[br][:, rr * 8:(rr + 1) * 8, :].reshape(m, LANE) for rr in (r, r + 1)], axis=-1)
            acc = acc + jnp.dot(x.astype(BF16), w1_ref[br, r // 2], preferred_element_type=F32)
        h = acc * jax.nn.sigmoid(acc)
        o_ref[:, br * LANE:(br + 1) * LANE] = jnp.dot(h.astype(BF16), w2_ref[br], preferred_element_type=F32)


def _block_diag2(w):
    z = jnp.zeros_like(w)
    return jnp.concatenate([jnp.concatenate([w, z], axis=-1), jnp.concatenate([z, w], axis=-1)], axis=-2)


def _compress_pages(pool, page_table, cmp_pos, cmp_w1, cmp_w2, layer):
    bsz, n_pages = page_table.shape
    total = bsz * n_pages
    n_steps = total // PAGES_PER_STEP
    assert total % PAGES_PER_STEP == 0 and KVH_D == 2
    w1 = cmp_w1.reshape(2, CMP_BLOCK, HEAD_DIM, CMP_HIDDEN)
    w1_bd = _block_diag2(w1).astype(BF16).reshape(2, CMP_BLOCK // 2, 2 * LANE, KVH_D * CMP_HIDDEN)
    w2_bd = _block_diag2(cmp_w2).astype(BF16)
    pos_t = jnp.tile(jnp.transpose(cmp_pos, (0, 2, 1)), (1, KVH_D, BLK_PER_PAGE))
    lane_in = np.arange(2 * PAGE_SIZE)
    page2, blk, r = lane_in // PAGE_SIZE, (lane_in % PAGE_SIZE) // CMP_BLOCK, lane_in % CMP_BLOCK
    perm = np.zeros((2 * PAGE_SIZE, 2 * PAGE_SIZE), np.float32)
    perm[lane_in, r * (2 * BLK_PER_PAGE) + page2 * BLK_PER_PAGE + blk] = 1.0
    perm = jnp.asarray(perm, BF16)
    m = PAGES_PER_STEP * BLK_PER_PAGE
    grid_spec = pltpu.PrefetchScalarGridSpec(
        num_scalar_prefetch=1,
        grid=(n_steps,),
        in_specs=[pl.BlockSpec(memory_space=pl.ANY),
                  pl.BlockSpec((2, LANE, PAGE_SIZE), lambda s, pt: (0, 0, 0)),
                  pl.BlockSpec((2 * PAGE_SIZE, 2 * PAGE_SIZE), lambda s, pt: (0, 0)),
                  pl.BlockSpec((2, CMP_BLOCK // 2, 2 * LANE, KVH_D * CMP_HIDDEN), lambda s, pt: (0, 0, 0, 0)),
                  pl.BlockSpec((2, KVH_D * CMP_HIDDEN, LANE), lambda s, pt: (0, 0, 0))],
        out_specs=pl.BlockSpec((m, CMP_W), lambda s, pt: (s, 0)),
        scratch_shapes=[pltpu.VMEM((2, PAGES_PER_STEP, CMP_W, PAGE_SIZE), F32),
                        pltpu.VMEM((PAGES_PER_STEP // 2, 2 * PAGE_SIZE, LANE), F32),
                        pltpu.VMEM((PAGES_PER_STEP // 2, 2 * PAGE_SIZE, LANE), F32),
                        pltpu.SemaphoreType.DMA((2,))],
    )
    return pl.pallas_call(
        functools.partial(_compress_pages_body, layer=layer, n_steps=n_steps),
        grid_spec=grid_spec,
        out_shape=jax.ShapeDtypeStruct((total * BLK_PER_PAGE, CMP_W), F32),
        compiler_params=pltpu.CompilerParams(dimension_semantics=("arbitrary",), vmem_limit_bytes=VMEM_LIMIT),
        name="nsa_compress_pages",
    )(page_table.reshape(-1), pool, pos_t, perm, w1_bd, w2_bd)


def _compress_prompt_body(xk_ref, xv_ref, pos_ref, w1_ref, w2_ref, o_ref):
    x_ref = (xk_ref, xv_ref)
    nb = xk_ref.shape[0] // CMP_BLOCK
    for br in range(2):
        acc = jnp.zeros((nb, KVH_D * CMP_HIDDEN), F32)
        for r in range(CMP_BLOCK):
            x = x_ref[br][pl.ds(r, nb, stride=CMP_BLOCK), :] + pos_ref[br, r:r + 1, :]
            acc = acc + jnp.dot(x.astype(BF16), w1_ref[br, r], preferred_element_type=F32)
        h = acc * jax.nn.sigmoid(acc)
        o_ref[:, br * LANE:(br + 1) * LANE] = jnp.dot(h.astype(BF16), w2_ref[br], preferred_element_type=F32)


def _compress_prompt(rows, cmp_pos, cmp_w1, cmp_w2):
    b, t, w = rows.shape
    nb = t // CMP_BLOCK
    w1 = cmp_w1.reshape(2, CMP_BLOCK, HEAD_DIM, CMP_HIDDEN)
    w1_bd = _block_diag2(w1).astype(BF16)
    w2_bd = _block_diag2(cmp_w2).astype(BF16)
    pos = jnp.concatenate([cmp_pos, cmp_pos], axis=-1)
    return pl.pallas_call(
        _compress_prompt_body,
        grid=(b,),
        in_specs=[pl.BlockSpec((None, t, LANE), lambda i: (i, 0, 0)),
                  pl.BlockSpec((None, t, LANE), lambda i: (i, 0, 1)),
                  pl.BlockSpec((2, CMP_BLOCK, LANE), lambda i: (0, 0, 0)),
                  pl.BlockSpec((2, CMP_BLOCK, LANE, KVH_D * CMP_HIDDEN), lambda i: (0, 0, 0, 0)),
                  pl.BlockSpec((2, KVH_D * CMP_HIDDEN, LANE), lambda i: (0, 0, 0))],
        out_specs=pl.BlockSpec((None, nb, w), lambda i: (i, 0, 0)),
        out_shape=jax.ShapeDtypeStruct((b, nb, w), F32),
        compiler_params=pltpu.CompilerParams(dimension_semantics=("arbitrary",), vmem_limit_bytes=VMEM_LIMIT),
        name="nsa_compress_prompt",
    )(rows, rows, pos, w1_bd, w2_bd)


def _slc_copy(pool_ref, buf_ref, sem_ref, layer, page, feat0, slot, idx):
    return pltpu.make_async_copy(pool_ref.at[layer, page, pl.ds(feat0, HEAD_DIM), :], buf_ref.at[slot, idx], sem_ref.at[slot])


def _slc_step_body(pt_ref, sel_ref, pool_ref, q_ref, kn_ref, vn_ref, o_ref, kbuf_ref, vbuf_ref, sem_ref, *,
                   layer, n_seq, n_pages, new_blk):
    b = pl.program_id(0)
    slot = b % 2
    n_sel = KVH_D * TOP_N

    def copies(seq, slot_, from_table):
        out = []
        for g in range(KVH_D):
            for i in range(TOP_N):
                idx = g * TOP_N + i
                page = 0
                if from_table:
                    blk = jnp.minimum(sel_ref[seq * n_sel + idx], new_blk - 1)
                    page = pt_ref[seq * n_pages + (blk >> 1)]
                out.append(_slc_copy(pool_ref, kbuf_ref, sem_ref, layer, page, CMP_W + g * HEAD_DIM, slot_, idx))
                out.append(_slc_copy(pool_ref, vbuf_ref, sem_ref, layer, page, CMP_W + (KVH_D + g) * HEAD_DIM, slot_, idx))
        return out

    @pl.when(b == 0)
    def _():
        for c in copies(0, 0, True):
            c.start()

    @pl.when(b + 1 < n_seq)
    def _():
        for c in copies(b + 1, 1 - slot, True):
            c.start()

    for c in copies(0, slot, False):
        c.wait()

    half = lax.broadcasted_iota(jnp.int32, (8, LANE), 1) >> 6
    for g in range(KVH_D):
        q = q_ref[0, g] * (HEAD_DIM ** -0.5)
        qb = q.astype(BF16)
        s_parts = []
        for i in range(TOP_N):
            blk = sel_ref[b * n_sel + g * TOP_N + i]
            s_i = jnp.dot(qb, kbuf_ref[slot, g * TOP_N + i].astype(BF16), preferred_element_type=F32)
            ok = (half == (blk & 1)) & (blk < new_blk)
            s_parts.append(jnp.where(ok, s_i, NEG))
        s = jnp.concatenate(s_parts, axis=-1)
        s_new = jnp.sum(q * kn_ref[0, g], axis=-1, keepdims=True)
        m = jnp.maximum(jnp.max(s, axis=-1, keepdims=True), s_new)
        p = jnp.exp(s - m)
        p_new = jnp.exp(s_new - m)
        den = jnp.sum(p, axis=-1, keepdims=True) + p_new
        o = p_new * vn_ref[0, g]
        for i in range(TOP_N):
            p_i = p[:, i * LANE:(i + 1) * LANE].astype(BF16)
            o = o + lax.dot_general(p_i, vbuf_ref[slot, g * TOP_N + i].astype(BF16), (((1,), (1,)), ((), ())),
                                    preferred_element_type=F32)
        o_ref[0, g] = o / den


def _slc_step(pool, page_table, sel, q, k_new, v_new, layer):
    bsz, n_pages = page_table.shape
    n_sel = KVH_D * TOP_N
    grid_spec = pltpu.PrefetchScalarGridSpec(
        num_scalar_prefetch=2,
        grid=(bsz,),
        in_specs=[pl.BlockSpec(memory_space=pl.ANY),
                  pl.BlockSpec((1, KVH_D, 8, HEAD_DIM), lambda b, pt, sl: (b, 0, 0, 0)),
                  pl.BlockSpec((1, KVH_D, 1, HEAD_DIM), lambda b, pt, sl: (b, 0, 0, 0)),
                  pl.BlockSpec((1, KVH_D, 1, HEAD_DIM), lambda b, pt, sl: (b, 0, 0, 0))],
        out_specs=pl.BlockSpec((1, KVH_D, 8, HEAD_DIM), lambda b, pt, sl: (b, 0, 0, 0)),
        scratch_shapes=[pltpu.VMEM((2, n_sel, HEAD_DIM, PAGE_SIZE), F32),
                        pltpu.VMEM((2, n_sel, HEAD_DIM, PAGE_SIZE), F32),
                        pltpu.SemaphoreType.DMA((2,))],
    )
    return pl.pallas_call(
        functools.partial(_slc_step_body, layer=layer, n_seq=bsz, n_pages=n_pages, new_blk=2 * n_pages),
        grid_spec=grid_spec,
        out_shape=jax.ShapeDtypeStruct((bsz, KVH_D, 8, HEAD_DIM), F32),
        compiler_params=pltpu.CompilerParams(dimension_semantics=("arbitrary",), vmem_limit_bytes=VMEM_LIMIT),
        name="nsa_slc_step",
    )(page_table.reshape(-1), sel.reshape(-1), pool, q, k_new, v_new)


WIN_TILES = 4


def _window_attend_body(sink_ref, q_ref, k_ref, v_ref, o_ref, *, window, use_sinks):
    g = pl.program_id(1)
    r = q_ref.shape[2] // HEAD_DIM
    rows = r * QT
    span = window + QT
    row = lax.broadcasted_iota(jnp.int32, (rows, span), 0)
    col = lax.broadcasted_iota(jnp.int32, (rows, span), 1)
    if use_sinks:
        head = lax.broadcasted_iota(jnp.int32, (rows, 1), 0) // QT
        sink = jnp.zeros((rows, 1), F32)
        for h in range(r):
            sink = jnp.where(head == h, sink_ref[g * r + h], sink)
    for j in range(WIN_TILES):
        qi = pl.program_id(2) * WIN_TILES + j
        q = (_heads_to_rows(q_ref[0, j * QT:(j + 1) * QT, :]) * (HEAD_DIM ** -0.5)).astype(BF16)
        start = pl.multiple_of(jnp.maximum(qi * QT - window, 0), QT)
        k = k_ref[0, 0, pl.ds(start, span), :].astype(BF16)
        v = v_ref[0, 0, pl.ds(start, span), :].astype(BF16)
        s = lax.dot_general(q, k, (((1,), (1,)), ((), ())), preferred_element_type=F32)
        diff = qi * QT + (row & (QT - 1)) - (start + col)
        s = jnp.where((diff >= 0) & (diff < window), s, -jnp.inf)
        m = jnp.max(s, axis=-1, keepdims=True)
        if use_sinks:
            m = jnp.maximum(m, sink)
        p = jnp.exp(s - m)
        den = jnp.sum(p, axis=-1, keepdims=True)
        if use_sinks:
            den = den + jnp.exp(sink - m)
        o = jnp.dot(p.astype(BF16), v, preferred_element_type=F32) / den
        o_ref[0, j * QT:(j + 1) * QT, :] = _rows_to_heads(o, r)


def _window_attend(q, k, v, window, sinks=None):
    b, t, w = q.shape
    g, d = k.shape[1], k.shape[3]
    gw = w // g
    use_sinks = sinks is not None
    if sinks is None:
        sinks = jnp.zeros((w // d,), F32)
    tq = WIN_TILES * QT
    grid_spec = pltpu.PrefetchScalarGridSpec(
        num_scalar_prefetch=1,
        grid=(b, g, t // tq),
        in_specs=[pl.BlockSpec((1, tq, gw), lambda i, j, n, sk: (i, n, j)),
                  pl.BlockSpec((1, 1, t, d), lambda i, j, n, sk: (i, j, 0, 0)),
                  pl.BlockSpec((1, 1, t, d), lambda i, j, n, sk: (i, j, 0, 0))],
        out_specs=pl.BlockSpec((1, tq, gw), lambda i, j, n, sk: (i, n, j)),
    )
    return pl.pallas_call(
        functools.partial(_window_attend_body, window=window, use_sinks=use_sinks),
        grid_spec=grid_spec,
        out_shape=jax.ShapeDtypeStruct((b, t, w), F32),
        compiler_params=pltpu.CompilerParams(dimension_semantics=("arbitrary",) * 3, vmem_limit_bytes=VMEM_LIMIT),
        name="window_attend_%d" % window,
    )(sinks.astype(F32), q, k, v)


MLSTM_CHUNK = 128
MLSTM_NB = 2
HIGHEST = lax.Precision.HIGHEST


def _mlstm_body(q_ref, kt_ref, v_ref, g_ref, ob_ref, hn_ref, o_ref, c_out, n_out, m_out, c_ref, n_ref, m_ref):
    ci = pl.program_id(1)
    L = MLSTM_CHUNK

    @pl.when(ci == 0)
    def _():
        c_ref[...] = jnp.zeros_like(c_ref)
        n_ref[...] = jnp.zeros_like(n_ref)
        m_ref[...] = jnp.zeros_like(m_ref)

    row = lax.broadcasted_iota(jnp.int32, (L, L), 0)
    col = lax.broadcasted_iota(jnp.int32, (L, L), 1)
    causal = col <= row
    upper = jnp.where(row <= col, 1.0, 0.0)
    for bb, h in [(bb, h) for bb in range(MLSTM_NB) for h in range(H_B)]:
        q = q_ref[bb, :, h * DK_B:(h + 1) * DK_B]
        kt = kt_ref[bb, h * DK_B:(h + 1) * DK_B, :] * (DK_B ** -0.5)
        v = v_ref[bb, :, h * DV_B:(h + 1) * DV_B]
        ig = g_ref[bb, h:h + 1, :]
        lf = g_ref[bb, H_B + h:H_B + h + 1, :]
        m_prev = m_ref[bb, h, 0:1, 0:1]
        f_col = jnp.sum(jnp.where(causal, jnp.broadcast_to(lf, (L, L)), 0.0), axis=-1, keepdims=True)
        f_row = jnp.dot(jnp.broadcast_to(lf, (8, L)), upper, preferred_element_type=F32, precision=HIGHEST)[0:1]
        log_d = jnp.where(causal, f_col - f_row + ig, -jnp.inf)
        m_inter = f_col + m_prev
        m_t = jnp.maximum(m_inter, jnp.max(log_d, axis=-1, keepdims=True))
        d_mat = jnp.exp(log_d - m_t)
        w_inter = jnp.exp(m_inter - m_t)
        qb = q.astype(BF16)
        qk = jnp.dot(qb, kt.astype(BF16), preferred_element_type=F32) * d_mat
        num = jnp.dot(qk.astype(BF16), v.astype(BF16), preferred_element_type=F32)
        num = num + w_inter * jnp.dot(qb, c_ref[bb, h].astype(BF16), preferred_element_type=F32)
        qn = jnp.dot(qb, n_ref[bb, h].astype(BF16), preferred_element_type=F32)[:, 0:1]
        den = jnp.sum(qk, axis=-1, keepdims=True) + w_inter * qn
        hh = num / jnp.maximum(jnp.abs(den), jnp.exp(-m_t))
        hh = hh * lax.rsqrt(jnp.mean(hh * hh, axis=-1, keepdims=True) + RMS_EPS) * hn_ref[...]
        o_ref[bb, :, h * DV_B:(h + 1) * DV_B] = hh * jax.nn.sigmoid(ob_ref[bb, :, h * DV_B:(h + 1) * DV_B])
        m_new = m_t[L - 1:L, :]
        f_last = f_col[L - 1:L, :]
        w_end = jnp.exp(f_last - f_row + ig - m_new)
        decay = jnp.exp(f_last + m_prev - m_new)
        ktw = kt * w_end
        c_ref[bb, h] = decay * c_ref[bb, h] + jnp.dot(ktw.astype(BF16), v.astype(BF16), preferred_element_type=F32)
        n_ref[bb, h] = decay * n_ref[bb, h] + jnp.sum(ktw, axis=-1, keepdims=True)
        m_ref[bb, h] = jnp.broadcast_to(m_new, m_ref.shape[2:])

    @pl.when(ci == pl.num_programs(1) - 1)
    def _():
        c_out[...] = c_ref[...]
        n_out[...] = n_ref[...]
        m_out[...] = m_ref[...]


def _mlstm_prompt(q, kt, v, gates, ob, h_norm):
    b, t, _ = q.shape
    L = MLSTM_CHUNK
    nc = t // L
    return pl.pallas_call(
        _mlstm_body,
        grid=(b // MLSTM_NB, nc),
        in_specs=[pl.BlockSpec((MLSTM_NB, L, H_B * DK_B), lambda i, c: (i, c, 0)),
                  pl.BlockSpec((MLSTM_NB, H_B * DK_B, L), lambda i, c: (i, 0, c)),
                  pl.BlockSpec((MLSTM_NB, L, H_B * DV_B), lambda i, c: (i, c, 0)),
                  pl.BlockSpec((MLSTM_NB, 2 * H_B, L), lambda i, c: (i, 0, c)),
                  pl.BlockSpec((MLSTM_NB, L, H_B * DV_B), lambda i, c: (i, c, 0)),
                  pl.BlockSpec((1, DV_B), lambda i, c: (0, 0))],
        out_specs=[pl.BlockSpec((MLSTM_NB, L, H_B * DV_B), lambda i, c: (i, c, 0)),
                   pl.BlockSpec((MLSTM_NB, H_B, DK_B, DV_B), lambda i, c: (i, 0, 0, 0)),
                   pl.BlockSpec((MLSTM_NB, H_B, DK_B, LANE), lambda i, c: (i, 0, 0, 0)),
                   pl.BlockSpec((MLSTM_NB, H_B, 8, LANE), lambda i, c: (i, 0, 0, 0))],
        out_shape=[jax.ShapeDtypeStruct((b, t, H_B * DV_B), F32),
                   jax.ShapeDtypeStruct((b, H_B, DK_B, DV_B), F32),
                   jax.ShapeDtypeStruct((b, H_B, DK_B, LANE), F32),
                   jax.ShapeDtypeStruct((b, H_B, 8, LANE), F32)],
        scratch_shapes=[pltpu.VMEM((MLSTM_NB, H_B, DK_B, DV_B), F32), pltpu.VMEM((MLSTM_NB, H_B, DK_B, LANE), F32),
                        pltpu.VMEM((MLSTM_NB, H_B, 8, LANE), F32)],
        compiler_params=pltpu.CompilerParams(dimension_semantics=("arbitrary", "arbitrary"), vmem_limit_bytes=VMEM_LIMIT),
        name="mlstm_prompt",
    )(q, kt, v, gates, ob, h_norm.reshape(1, DV_B))


SSD_CHUNK = 128
CONV_PAD = 8


def _ssd_body(z_ref, xr_ref, bcr_ref, dtc_ref, dtr_ref, an_ref, cw_ref, cb_ref, ds_ref, yn_ref,
              o_ref, h_out, xp_ref, h_ref):
    ci = pl.program_id(1)
    L = SSD_CHUNK
    r = H_C // G_C

    @pl.when(ci == 0)
    def _():
        xp_ref[0:CONV_PAD, :] = jnp.zeros((CONV_PAD, CONV_DIM_C), F32)
        h_ref[...] = jnp.zeros_like(h_ref)

    xp_ref[CONV_PAD:CONV_PAD + L, 0:D_INNER_C] = xr_ref[0]
    xp_ref[CONV_PAD:CONV_PAD + L, D_INNER_C:CONV_DIM_C] = bcr_ref[0]
    conv = cb_ref[...]
    for j in range(CONV_W):
        off = CONV_PAD - (CONV_W - 1) + j
        conv = conv + xp_ref[off:off + L, :] * cw_ref[j:j + 1, :]
    tail = xp_ref[L:L + CONV_PAD, :]
    xp_ref[0:CONV_PAD, :] = tail
    xbc = conv * jax.nn.sigmoid(conv)
    x = xbc[:, 0:D_INNER_C]
    bm = xbc[:, D_INNER_C:D_INNER_C + G_C * N_C]
    cm = xbc[:, D_INNER_C + G_C * N_C:CONV_DIM_C]

    row = lax.broadcasted_iota(jnp.int32, (L, L), 0)
    col = lax.broadcasted_iota(jnp.int32, (L, L), 1)
    causal = col <= row
    lower = jnp.where(causal, 1.0, 0.0)
    upper = jnp.where(row <= col, 1.0, 0.0)
    dt_c = dtc_ref[0]
    dt_r = dtr_ref[0]
    a_c = dt_c * an_ref[0:1, 0:H_C]
    a_r = dt_r * an_ref[:, H_C:H_C + 1]
    cum_c = jnp.dot(lower, a_c, preferred_element_type=F32, precision=HIGHEST)
    cum_r = jnp.dot(a_r, upper, preferred_element_type=F32, precision=HIGHEST)
    ys = []
    for g in range(G_C):
        bg = bm[:, g * N_C:(g + 1) * N_C].astype(BF16)
        cg = cm[:, g * N_C:(g + 1) * N_C].astype(BF16)
        cbm = lax.dot_general(cg, bg, (((1,), (1,)), ((), ())), preferred_element_type=F32)
        dtxw = []
        for hh in range(r):
            h = g * r + hh
            cc = cum_c[:, h:h + 1]
            cr = cum_r[h:h + 1, :]
            decay = jnp.exp(jnp.where(causal, cc - cr, -jnp.inf))
            dtx = dt_c[:, h:h + 1] * x[:, h * P_C:(h + 1) * P_C]
            y = jnp.dot((cbm * decay).astype(BF16), dtx.astype(BF16), preferred_element_type=F32)
            hs = h_ref[h].astype(BF16)
            y = y + lax.dot_general(cg, hs, (((1,), (1,)), ((), ())), preferred_element_type=F32) * jnp.exp(cc)
            ys.append(y + ds_ref[0:1, h:h + 1] * x[:, h * P_C:(h + 1) * P_C])
            cl = cum_c[L - 1:L, h:h + 1]
            dtxw.append(dtx * jnp.exp(cl - cc))
        dtxw = jnp.concatenate(dtxw, axis=-1)
        upd = jnp.dot(dtxw.T.astype(BF16), bg, preferred_element_type=F32)
        for hh in range(r):
            h = g * r + hh
            cl = cum_c[L - 1:L, h:h + 1]
            h_ref[h] = h_ref[h] * jnp.exp(cl) + upd[hh * P_C:(hh + 1) * P_C, :]
    y = jnp.concatenate(ys, axis=-1)
    z = z_ref[0]
    y = y * (z * jax.nn.sigmoid(z))
    gw = D_INNER_C // G_C
    outs = []
    for g in range(G_C):
        yg = y[:, g * gw:(g + 1) * gw]
        outs.append(yg * lax.rsqrt(jnp.mean(yg * yg, axis=-1, keepdims=True) + RMS_EPS))
    o_ref[0] = jnp.concatenate(outs, axis=-1) * yn_ref[...]

    @pl.when(ci == pl.num_programs(1) - 1)
    def _():
        h_out[0] = h_ref[...]


def _ssd_prompt(proj, dt, a_neg, conv_w, conv_b, d_skip, y_norm):
    b, t, _ = proj.shape
    L = SSD_CHUNK
    nc = t // L
    an = jnp.concatenate([jnp.broadcast_to(a_neg[None, :], (H_C, H_C)), a_neg[:, None]], axis=1)
    an = jnp.pad(an, ((0, 0), (0, LANE - an.shape[1])))
    ds = jnp.pad(d_skip[None, :], ((0, 0), (0, LANE - H_C)))
    return pl.pallas_call(
        _ssd_body,
        grid=(b, nc),
        in_specs=[pl.BlockSpec((1, L, D_INNER_C), lambda i, c: (i, c, 0)),
                  pl.BlockSpec((1, L, D_INNER_C), lambda i, c: (i, c, 1)),
                  pl.BlockSpec((1, L, 2 * G_C * N_C), lambda i, c: (i, c, 2)),
                  pl.BlockSpec((1, L, H_C), lambda i, c: (i, c, 0)),
                  pl.BlockSpec((1, H_C, L), lambda i, c: (i, 0, c)),
                  pl.BlockSpec((H_C, LANE), lambda i, c: (0, 0)),
                  pl.BlockSpec((CONV_W, CONV_DIM_C), lambda i, c: (0, 0)),
                  pl.BlockSpec((1, CONV_DIM_C), lambda i, c: (0, 0)),
                  pl.BlockSpec((1, LANE), lambda i, c: (0, 0)),
                  pl.BlockSpec((1, D_INNER_C), lambda i, c: (0, 0))],
        out_specs=[pl.BlockSpec((1, L, D_INNER_C), lambda i, c: (i, c, 0)),
                   pl.BlockSpec((1, H_C, P_C, N_C), lambda i, c: (i, 0, 0, 0))],
        out_shape=[jax.ShapeDtypeStruct((b, t, D_INNER_C), F32),
                   jax.ShapeDtypeStruct((b, H_C, P_C, N_C), F32)],
        scratch_shapes=[pltpu.VMEM((CONV_PAD + SSD_CHUNK, CONV_DIM_C), F32), pltpu.VMEM((H_C, P_C, N_C), F32)],
        compiler_params=pltpu.CompilerParams(dimension_semantics=("arbitrary", "arbitrary"), vmem_limit_bytes=VMEM_LIMIT),
        name="ssd_prompt",
    )(proj, proj, proj, dt, dt.transpose(0, 2, 1), an, conv_w, conv_b.reshape(1, CONV_DIM_C), ds,
      y_norm.reshape(1, D_INNER_C))


def _rms_norm(x, g, eps=RMS_EPS):
    xf = x.astype(F32)
    y = xf * lax.rsqrt(jnp.mean(xf * xf, axis=-1, keepdims=True) + eps)
    return (y * g.astype(F32)).astype(x.dtype)


def _split_cols(x, sizes):
    return jnp.split(x, [int(s) for s in np.cumsum(sizes)[:-1]], axis=-1)


def _last_rows(a, n):
    t = a.shape[1]
    if t >= n:
        return a[:, t - n:]
    pad = [(0, 0)] * a.ndim
    pad[1] = (n - t, 0)
    return jnp.pad(a, pad)


def _softmax_attend(q, k, v, mask, sinks=None):
    s = jnp.einsum('...qgrd,...kgd->...grqk', q, k).astype(F32) * (HEAD_DIM ** -0.5)
    s = jnp.where(mask[..., None, None, :, :], s, -jnp.inf)
    m = jnp.max(s, axis=-1, keepdims=True)
    if sinks is not None:
        sk = sinks.astype(F32)[:, :, None, None]
        m = jnp.maximum(m, sk)
    m = jnp.where(jnp.isfinite(m), m, 0.0)
    p = jnp.exp(s - m)
    den = jnp.sum(p, axis=-1, keepdims=True)
    if sinks is not None:
        den = den + jnp.exp(sk - m)
    p = p / jnp.where(den > 0, den, 1.0)
    out = jnp.einsum('...grqk,...kgd->...qgrd', p.astype(v.dtype), v)
    return out, p


def _window_step_attention(q, k_new, v_new, buf, window, sinks):
    t = k_new.shape[1]
    k = jnp.concatenate([buf[:, :, 0], k_new], axis=1)
    v = jnp.concatenate([buf[:, :, 1], v_new], axis=1)
    q_pos = PAST_LEN + jnp.arange(t)
    k_pos = PAST_LEN - window + jnp.arange(window + t)
    diff = q_pos[:, None] - k_pos[None, :]
    mask = (k_pos[None, :] >= 0) & (diff >= 0) & (diff < window)
    out, _ = _softmax_attend(q, k, v, mask, sinks)
    new_buf = jnp.concatenate([buf, jnp.stack([k_new, v_new], axis=2)], axis=1)[:, t:]
    return out, new_buf


def _mlstm_chunk(state, inputs):
    c, n, m = state
    q, k, v, ig, lf = inputs
    L = q.shape[1]
    f_cum = jnp.cumsum(lf, axis=1)
    causal = jnp.tril(jnp.ones((L, L), dtype=bool))
    log_d = jnp.where(causal[None, :, :, None], f_cum[:, :, None, :] - f_cum[:, None, :, :] + ig[:, None, :, :], -jnp.inf)
    m_inter = f_cum + m[:, None, :]
    m_t = jnp.maximum(m_inter, jnp.max(log_d, axis=2))
    d_mat = jnp.exp(log_d - m_t[:, :, None, :])
    w_inter = jnp.exp(m_inter - m_t)
    qk = jnp.einsum('bthd,bshd->btsh', q, k) * d_mat
    num = jnp.einsum('btsh,bshv->bthv', qk, v) + w_inter[..., None] * jnp.einsum('bhvd,bthd->bthv', c, q)
    den = jnp.sum(qk, axis=2) + w_inter * jnp.einsum('bhd,bthd->bth', n, q)
    h = num / jnp.maximum(jnp.abs(den), jnp.exp(-m_t))[..., None]
    m_new = m_t[:, -1]
    w_end = jnp.exp(f_cum[:, -1:] - f_cum + ig - m_new[:, None, :])
    decay = jnp.exp(f_cum[:, -1] + m - m_new)
    c_new = decay[..., None, None] * c + jnp.einsum('bsh,bshv,bshd->bhvd', w_end, v, k)
    n_new = decay[..., None] * n + jnp.einsum('bsh,bshd->bhd', w_end, k)
    return (c_new, n_new, m_new), h


def _ssd_chunk(h, inputs, a_neg):
    x, dt, bm, cm = inputs
    bsz, L = x.shape[:2]
    r = H_C // G_C
    cum = jnp.cumsum(dt * a_neg, axis=1)
    causal = jnp.tril(jnp.ones((L, L), dtype=bool))
    seg = jnp.where(causal[None, :, :, None], cum[:, :, None, :] - cum[:, None, :, :], -jnp.inf)
    decay = jnp.exp(seg).reshape(bsz, L, L, G_C, r)
    dtx = (dt[..., None] * x).reshape(bsz, L, G_C, r, P_C)
    cb = jnp.einsum('btgn,bsgn->btsg', cm, bm)
    y = jnp.einsum('btsg,btsgr,bsgrp->btgrp', cb, decay, dtx)
    hg = h.reshape(bsz, G_C, r, P_C, N_C)
    y = y + jnp.einsum('btgn,bgrpn->btgrp', cm, hg) * jnp.exp(cum).reshape(bsz, L, G_C, r)[..., None]
    w_end = jnp.exp(cum[:, -1:] - cum).reshape(bsz, L, G_C, r)
    h_new = hg * jnp.exp(cum[:, -1]).reshape(bsz, G_C, r)[..., None, None] + jnp.einsum('bsgr,bsgrp,bsgn->bgrpn', w_end, dtx, bm)
    return h_new.reshape(bsz, H_C, P_C, N_C), y.reshape(bsz, L, H_C, P_C)


def _causal_conv(xbc, buf, w, b):
    t = xbc.shape[1]
    xp = jnp.concatenate([buf, xbc], axis=1)
    out = b + sum(xp[:, j:j + t] * w[j] for j in range(CONV_W))
    return jax.nn.silu(out), xp[:, t:]


def _compress(rows, pos, w1, w2):
    bsz, t, g, d = rows.shape
    nb = t // CMP_BLOCK
    blk = rows.reshape(bsz, nb, CMP_BLOCK, g, d) + pos[:, None, :]
    blk = blk.transpose(0, 1, 3, 2, 4).reshape(bsz, nb, g, CMP_BLOCK * d)
    return jax.nn.silu(blk @ w1) @ w2


def _select_blocks(p_cmp, q_pos, n_top):
    bsz, g, r, t, nbc = p_cmp.shape
    ratio = SLC_BLOCK // CMP_BLOCK
    imp = p_cmp.sum(axis=2).reshape(bsz, g, t, nbc // ratio, ratio).sum(axis=-1)
    blk = jnp.arange(nbc // ratio)[None, :]
    cur = (q_pos // SLC_BLOCK)[:, None]
    forced = (blk == 0) | (blk == cur) | (blk == cur - 1)
    score = jnp.where(blk > cur, -jnp.inf, jnp.where(forced, jnp.inf, imp))
    _, sel = lax.top_k(score, n_top)
    return sel.astype(jnp.int32)


def _pad_cols(w, n):
    return jnp.pad(w, ((0, 0), (0, n - w.shape[1])))


def _even_layer(x, w, st, prompt):
    bsz, t, _ = x.shape
    r_a = H_A // KVH_A
    n_in = sum(IN_E_SIZES)
    proj = _norm_matmul(x.reshape(bsz * t, D_MODEL), w['norm_mix'], w['w_in_p'])[:, :n_in].reshape(bsz, t, n_in)
    qa, ka, va, qb, kb, vb, ib, fb, ob = _split_cols(proj, IN_E_SIZES)
    qa = _rms_norm(qa.reshape(bsz, t, KVH_A, r_a, HEAD_DIM), w['q_norm'])
    ka = _rms_norm(ka.reshape(bsz, t, KVH_A, HEAD_DIM), w['k_norm'])
    va = va.reshape(bsz, t, KVH_A, HEAD_DIM)
    sinks = w['sinks'].reshape(KVH_A, r_a)
    if prompt:
        o_a = _window_attend(qa.reshape(bsz, t, H_A * HEAD_DIM), ka.transpose(0, 2, 1, 3), va.transpose(0, 2, 1, 3),
                             WINDOW_A, w['sinks'])
        new_win = _last_rows(jnp.stack([ka, va], axis=2), WINDOW_A)
    else:
        o_a, new_win = _window_step_attention(qa, ka, va, st['a_win'], WINDOW_A, sinks)
    ig = ib + w['b_igate']
    lf = jax.nn.log_sigmoid(fb + w['b_fgate'])
    if prompt:
        gates = jnp.concatenate([ig.transpose(0, 2, 1), lf.transpose(0, 2, 1)], axis=1)
        hb, c_t, n_rep, m_rep = _mlstm_prompt(qb, kb.transpose(0, 2, 1), vb, gates, ob, w['h_norm'])
        c_new, n_new, m_new = c_t.transpose(0, 1, 3, 2), n_rep[..., 0], m_rep[:, :, 0, 0]
    else:
        qb = qb.reshape(bsz, t, H_B, DK_B)
        kb = kb.reshape(bsz, t, H_B, DK_B) * (DK_B ** -0.5)
        vb = vb.reshape(bsz, t, H_B, DV_B)
        init = (st['b_c'], st['b_n'], st['b_m'])
        (c_new, n_new, m_new), hb = _mlstm_chunk(init, (qb, kb, vb, ig, lf))
        hb = _rms_norm(hb, w['h_norm']) * jax.nn.sigmoid(ob.reshape(bsz, t, H_B, DV_B))
    xr = _out_proj(o_a.reshape(bsz * t, H_A * HEAD_DIM), hb.reshape(bsz * t, H_B * DV_B), w['w_out'],
                   x.reshape(bsz * t, D_MODEL))
    xr = _ffn(xr, w['norm_ffn'], w['w_gu'], w['w_down'])
    return xr.reshape(bsz, t, D_MODEL), (new_win, c_new, n_new, m_new)


def _odd_mixers(x, w, st, page_table, prompt):
    bsz, t, _ = x.shape
    r_d = H_D // KVH_D
    n_in = sum(IN_O_SIZES)
    proj_full = _norm_matmul(x.reshape(bsz * t, D_MODEL), w['norm_mix'], w['w_in_p'])
    proj = proj_full[:, :n_in].reshape(bsz, t, n_in)
    zc, xbc, dtc, qd, kvd, gd = _split_cols(proj, IN_O_SIZES)
    dt = jax.nn.softplus(dtc + w['dt_bias'])
    if prompt:
        yc, h_new = _ssd_prompt(proj_full.reshape(bsz, t, -1), dt, -jnp.exp(w['a_log']), w['conv_w'], w['conv_b'],
                                w['d_skip'], w['y_norm'])
        new_conv = xbc[:, t - (CONV_W - 1):]
    else:
        xbc, new_conv = _causal_conv(xbc, st['c_conv'], w['conv_w'], w['conv_b'])
        xc, bc, cc = _split_cols(xbc, (D_INNER_C, G_C * N_C, G_C * N_C))
        xc = xc.reshape(bsz, t, H_C, P_C)
        bc = bc.reshape(bsz, t, G_C, N_C)
        cc = cc.reshape(bsz, t, G_C, N_C)
        h_new, yc = _ssd_chunk(st['c_ssm'], (xc, dt, bc, cc), -jnp.exp(w['a_log']))
        yc = yc + w['d_skip'][:, None] * xc
        yc = yc.reshape(bsz, t, D_INNER_C) * jax.nn.silu(zc)
        yc = _rms_norm(yc.reshape(bsz, t, G_C, D_INNER_C // G_C), w['y_norm'].reshape(G_C, D_INNER_C // G_C))
    qd = _rms_norm(qd.reshape(bsz, t, KVH_D, r_d, HEAD_DIM), w['q_norm'])
    kc, vc, ks, vs, kw, vw = [a.reshape(bsz, t, KVH_D, HEAD_DIM) for a in _split_cols(kvd, (KVH_D * HEAD_DIM,) * 6)]
    ks = _rms_norm(ks, w['k_norm'])
    kw = _rms_norm(kw, w['k_norm'])
    new_kv = jnp.stack([kc, vc, ks, vs], axis=2)
    q_pos = jnp.arange(t) + (0 if prompt else PAST_LEN)
    ks_t, vs_t = ks.transpose(0, 2, 1, 3), vs.transpose(0, 2, 1, 3)

    def comp(rows, j):
        return _compress(rows, w['cmp_pos'][j], w['cmp_w1'][j], w['cmp_w2'][j])
    if prompt:
        cmp_blocks = _compress_prompt(kvd[..., :CMP_W], w['cmp_pos'], w['cmp_w1'], w['cmp_w2'])
        kcmp = cmp_blocks[..., :KVH_D * HEAD_DIM].reshape(bsz, -1, KVH_D, HEAD_DIM)
        vcmp = cmp_blocks[..., KVH_D * HEAD_DIM:].reshape(bsz, -1, KVH_D, HEAD_DIM)
    else:
        assert t == 1
        pool = jnp.transpose(st['d_kv'], (0, 1, 3, 4, 5, 2)).reshape(
            st['d_kv'].shape[0], st['d_kv'].shape[1], CMP_W + SLC_W, PAGE_SIZE)
        n_pages = page_table.shape[1]
        past = _compress_pages(pool, page_table, w['cmp_pos'], w['cmp_w1'], w['cmp_w2'], st['layer'])
        past = past.reshape(bsz, n_pages * BLK_PER_PAGE, 2, KVH_D, HEAD_DIM)
        pad_t = -(-t // SLC_BLOCK) * SLC_BLOCK
        padw = ((0, 0), (0, pad_t - t), (0, 0), (0, 0))
        kcmp = jnp.concatenate([past[:, :, 0], comp(jnp.pad(kc, padw), 0)], axis=1)
        vcmp = jnp.concatenate([past[:, :, 1], comp(jnp.pad(vc, padw), 1)], axis=1)
    kcmp = _rms_norm(kcmp, w['k_norm'])
    n_top = min(TOP_N, kcmp.shape[1] // (SLC_BLOCK // CMP_BLOCK))
    if prompt:
        q_tok = qd.reshape(bsz, t, H_D * HEAD_DIM)
        hshape = (bsz, t, KVH_D, r_d, HEAD_DIM)
        o_cmp, sel = _cmp_select(q_tok, kcmp.transpose(0, 2, 1, 3), vcmp.transpose(0, 2, 1, 3), n_top)
        o_slc = _slc_attend(q_tok, sel, ks_t, vs_t).reshape(hshape)
        o_cmp = o_cmp.reshape(hshape)
        o_win = _window_attend(q_tok, kw.transpose(0, 2, 1, 3), vw.transpose(0, 2, 1, 3), WINDOW_D).reshape(hshape)
        new_win = _last_rows(jnp.stack([kw, vw], axis=2), WINDOW_D)
    else:
        blk_end = jnp.arange(kcmp.shape[1]) * CMP_BLOCK + (CMP_BLOCK - 1)
        o_cmp, p_cmp = _softmax_attend(qd, kcmp, vcmp, blk_end[None, :] <= q_pos[:, None])
        sel = _select_blocks(p_cmp, q_pos, n_top)[:, :, 0]
        q_rows = jnp.pad(qd[:, 0], ((0, 0), (0, 0), (0, 8 - r_d), (0, 0)))
        o8 = _slc_step(pool, page_table, sel, q_rows, ks_t, vs_t, st['layer'])
        o_slc = o8[:, None, :, :r_d]
        o_win, new_win = _window_step_attention(qd, kw, vw, st['d_win'], WINDOW_D, None)
    g = jax.nn.sigmoid(gd).reshape(bsz, t, KVH_D, r_d, 3)
    o_d = g[..., 0:1] * o_cmp + g[..., 1:2] * o_slc + g[..., 2:3] * o_win
    return (yc.reshape(bsz * t, D_INNER_C), o_d.reshape(bsz * t, H_D * HEAD_DIM)), (h_new, new_conv, new_kv, new_win)


def kernel(x_prompt, x_sample, cache_a_win, state_b_c, state_b_n, state_b_m, state_c_ssm, state_c_conv, cache_d_kv, cache_d_win, page_table, e_norm_mix, e_w_in, e_q_norm, e_k_norm, e_sinks, e_b_igate, e_b_fgate, e_h_norm, e_w_out, e_norm_ffn, e_w_gu, e_w_down, o_norm_mix, o_w_in, o_conv_w, o_conv_b, o_dt_bias, o_a_log, o_d_skip, o_y_norm, o_q_norm, o_k_norm, o_cmp_pos, o_cmp_w1, o_cmp_w2, o_w_out, o_norm_ffn, o_router, o_w_gu, o_w_down):
    i = 0
    we = {'norm_mix': e_norm_mix[i], 'w_in_p': _pad_cols(e_w_in[i], _round_up(sum(IN_E_SIZES), LANE)).astype(BF16),
          'q_norm': e_q_norm[i], 'k_norm': e_k_norm[i],
          'sinks': e_sinks[i], 'b_igate': e_b_igate[i], 'b_fgate': e_b_fgate[i], 'h_norm': e_h_norm[i],
          'w_out': e_w_out[i].astype(BF16), 'norm_ffn': e_norm_ffn[i], 'w_gu': e_w_gu[i].astype(BF16),
          'w_down': e_w_down[i].astype(BF16)}
    wo = {'norm_mix': o_norm_mix[i], 'w_in_p': _pad_cols(o_w_in[i], _round_up(sum(IN_O_SIZES), LANE)).astype(BF16),
          'conv_w': o_conv_w[i], 'conv_b': o_conv_b[i],
          'dt_bias': o_dt_bias[i], 'a_log': o_a_log[i], 'd_skip': o_d_skip[i], 'y_norm': o_y_norm[i],
          'q_norm': o_q_norm[i], 'k_norm': o_k_norm[i], 'cmp_pos': o_cmp_pos[i], 'cmp_w1': o_cmp_w1[i],
          'cmp_w2': o_cmp_w2[i], 'w_out': o_w_out[i].astype(BF16), 'norm_ffn': o_norm_ffn[i],
          'router': _pad_cols(o_router[i], LANE), 'w_gu': o_w_gu[i].astype(BF16), 'w_down': o_w_down[i].astype(BF16)}
    st_e = {'a_win': cache_a_win[i], 'b_c': state_b_c[i], 'b_n': state_b_n[i], 'b_m': state_b_m[i]}
    st_o = {'c_ssm': state_c_ssm[i], 'c_conv': state_c_conv[i], 'd_kv': cache_d_kv, 'layer': i,
            'd_win': cache_d_win[i]}

    xp, sp_e = _even_layer(x_prompt, we, None, True)
    xs, ss_e = _even_layer(x_sample, we, st_e, False)

    mix_p, sp_o = _odd_mixers(xp, wo, None, page_table, True)
    mix_s, ss_o = _odd_mixers(xs, wo, st_o, page_table, False)
    xp2, xnp, lgp = _out_proj_route(mix_p[0], mix_p[1], wo['w_out'], xp.reshape(-1, D_MODEL), wo['norm_ffn'], wo['router'])
    xs2, xns, lgs = _out_proj_route(mix_s[0], mix_s[1], wo['w_out'], xs.reshape(-1, D_MODEL), wo['norm_ffn'], wo['router'])
    xn_all = jnp.concatenate([xnp, xns], axis=0)
    lg_all = jnp.concatenate([lgp, lgs], axis=0)[:, :N_EXP]
    y_p, y_s = _moe_layer([xp2, xs2], xn_all, lg_all, wo['w_gu'], wo['w_down'])
    y_prompt = y_p.reshape(x_prompt.shape)
    y_sample = y_s.reshape(x_sample.shape)

    def one(a):
        return a[None]
    return (y_prompt, y_sample, one(sp_e[0]), one(ss_e[0]), one(sp_e[1]), one(ss_e[1]), one(sp_e[2]), one(ss_e[2]),
            one(sp_e[3]), one(ss_e[3]), one(sp_o[0]), one(ss_o[0]), one(sp_o[1]), one(ss_o[1]),
            one(sp_o[2]), one(ss_o[2]), one(sp_o[3]), one(ss_o[3]))
```
